```python
import jax, jax.numpy as jnp
from jax import lax
import numpy as np

D_MODEL = 1024
BATCH = 8
SEQ = 4096
DEPTH = 2

GRID_W = 64
HEAD_DIM = 64
RW_HEADS = 4
RW_WIDTH = RW_HEADS * HEAD_DIM
RW_DECAY_RANK = 32
RW_ICLR_RANK = 32
RW_GATE_RANK = 64
RW_GN_EPS = 64e-5
NA_HEADS = 4
NA_WIDTH = NA_HEADS * HEAD_DIM
NA_WIN_ROWS = 8
NA_WIN_COLS = 16
MLA_HEADS = 8
MLA_Q_RANK = 256
MLA_KV_RANK = 128
MLA_NOPE_DIM = 64
MLA_ROPE_DIM = 32
MLA_V_DIM = 64
MLA_WIDTH = MLA_HEADS * MLA_V_DIM
ROPE_THETA = 10000.0
Q_BLOCK = 128

MIX_WIDTH = RW_WIDTH + NA_WIDTH + MLA_WIDTH
RW_IN = 3 * RW_WIDTH + 2 * RW_DECAY_RANK + 2 * RW_ICLR_RANK + RW_GATE_RANK
NA_IN = 3 * NA_WIDTH
MLA_IN = MLA_Q_RANK + MLA_KV_RANK + MLA_ROPE_DIM
IN_WIDTH = RW_IN + NA_IN + MLA_IN
D_FF = -(-8 * D_MODEL // (3 * 256)) * 256
DN_ALPHA = (2 * DEPTH) ** 0.25
DN_BETA = (8 * DEPTH) ** -0.25
LN_EPS = 1e-5
RMS_EPS = 1e-6
NEG_INF = -1e30

kernel_name = "hybrid_rwkv7_natten2d_mla_encoder"


def layer_norm(x, g, b):
    xf = x.astype(jnp.float32)
    mu = xf.mean(-1, keepdims=True)
    var = jnp.square(xf - mu).mean(-1, keepdims=True)
    return ((xf - mu) * lax.rsqrt(var + LN_EPS) * g + b).astype(x.dtype)


def rms_norm(x, g):
    xf = x.astype(jnp.float32)
    return (xf * lax.rsqrt(jnp.square(xf).mean(-1, keepdims=True) + RMS_EPS) * g).astype(x.dtype)


def rope_tables(T):
    inv_freq = jnp.power(ROPE_THETA, -jnp.arange(0, MLA_ROPE_DIM, 2, dtype=jnp.float32) / MLA_ROPE_DIM)
    ang = jnp.arange(T, dtype=jnp.float32)[:, None] * inv_freq[None, :]
    return jnp.cos(ang), jnp.sin(ang)


def apply_rope(x, cos, sin):
    x1, x2 = jnp.split(x, 2, axis=-1)
    return jnp.concatenate([x1 * cos - x2 * sin, x2 * cos + x1 * sin], -1).astype(x.dtype)


def centred_token_shift(p, mu):
    zero = jnp.zeros_like(p[:, :1])
    prev = jnp.concatenate([zero, p[:, :-1]], 1)
    nxt = jnp.concatenate([p[:, 1:], zero], 1)
    return p + mu[0] * (prev - p) + mu[1] * (nxt - p)


def wkv7_scan(r, decay, k, v, a, b, inclusive):
    B, T, H, N = r.shape
    xs = tuple(jnp.moveaxis(t.astype(jnp.float32), 1, 0) for t in (r, decay, k, v, a, b))

    def step(S, inp):
        r_t, w_t, k_t, v_t, a_t, b_t = inp
        sa = jnp.einsum('bhvk,bhk->bhv', S, a_t)
        S_new = S * w_t[:, :, None, :] + sa[..., None] * b_t[:, :, None, :] + v_t[..., None] * k_t[:, :, None, :]
        y = jnp.einsum('bhvk,bhk->bhv', S_new if inclusive else S, r_t)
        return S_new, y

    S0 = jnp.zeros((B, H, N, N), jnp.float32)
    _, ys = lax.scan(step, S0, xs)
    return jnp.moveaxis(ys, 0, 1)


def rwkv7_mixer(p, mu, w0, w2, a0, a2, g2, k_k, k_a, r_k, gn_g, gn_b):
    B, T, _ = p.shape
    f32 = jnp.float32
    p = centred_token_shift(p, mu)
    W, Rd, Ra = RW_WIDTH, RW_DECAY_RANK, RW_ICLR_RANK
    r, k, v, wd, ad, gd = jnp.split(p, [W, 2 * W, 3 * W, 3 * W + 2 * Rd, 3 * W + 2 * Rd + 2 * Ra], -1)
    wd = wd.reshape(B, T, 2, Rd)
    ad = ad.reshape(B, T, 2, Ra)
    log_w = -jax.nn.softplus(-(w0 + jnp.einsum('btdr,drc->btdc', jnp.tanh(wd), w2))) - 0.5
    decay = jnp.exp(-jnp.exp(log_w.astype(f32)))
    iclr = jax.nn.sigmoid(a0 + jnp.einsum('btdr,drc->btdc', ad, a2))
    gate = jax.nn.sigmoid(gd) @ g2

    def heads(t):
        return t.reshape(*t.shape[:-1], RW_HEADS, HEAD_DIM)

    kk = heads(k * k_k).astype(f32)
    kk = kk / jnp.maximum(jnp.sqrt(jnp.sum(kk * kk, -1, keepdims=True)), 1e-12)
    k_dir = heads(k[:, :, None, :] * (1.0 + (iclr - 1.0) * k_a))
    a_h = heads(iclr).astype(f32)
    dec_h = heads(decay)
    rh, vh = heads(r), heads(v)

    y_fwd = wkv7_scan(rh, dec_h[:, :, 0], k_dir[:, :, 0], vh, -kk, kk * a_h[:, :, 0], True)

    def flip(t):
        return jnp.flip(t, 1)

    y_bwd = flip(wkv7_scan(flip(rh), flip(dec_h[:, :, 1]), flip(k_dir[:, :, 1]), flip(vh),
                           flip(-kk), flip(kk * a_h[:, :, 1]), False))
    y = y_fwd + y_bwd
    mu_y = y.mean(-1, keepdims=True)
    var_y = jnp.square(y - mu_y).mean(-1, keepdims=True)
    yn = ((y - mu_y) * lax.rsqrt(var_y + RW_GN_EPS)).reshape(B, T, W) * gn_g + gn_b
    bonus = jnp.sum(rh.astype(f32) * k_dir[:, :, 0].astype(f32) * r_k, -1, keepdims=True) * vh.astype(f32)
    out = (yn + bonus.reshape(B, T, W)) * gate
    return out.astype(p.dtype)


def neighbourhood_attention(p, rpb):
    B, T, _ = p.shape
    rows = T // GRID_W
    kr = min(NA_WIN_ROWS, rows)
    kc = NA_WIN_COLS
    q, k, v = jnp.split(p, 3, -1)
    grid = lambda t: t.reshape(B, rows, GRID_W, NA_HEADS, HEAD_DIM)
    q, k, v = grid(q), grid(k), grid(v)
    i = jnp.arange(rows)
    row_start = jnp.clip(i - kr // 2, 0, rows - kr)
    key_rows = row_start[:, None] + jnp.arange(kr)[None, :]
    j = jnp.arange(GRID_W)
    col_start = jnp.clip(j - kc // 2, 0, GRID_W - kc)
    col_mask = (j[None, :] >= col_start[:, None]) & (j[None, :] < col_start[:, None] + kc)
    kg = jnp.take(k, key_rows, axis=1)
    vg = jnp.take(v, key_rows, axis=1)
    s = jnp.einsum('biqhd,birkhd->bhiqrk', q, kg).astype(jnp.float32) * (HEAD_DIM ** -0.5)
    dr = key_rows - i[:, None] + (NA_WIN_ROWS - 1)
    dc = jnp.clip(j[None, :] - j[:, None] + (kc - 1), 0, 2 * kc - 2)
    bias = rpb[:, dr[:, None, :, None], dc[None, :, None, :]]
    s = jnp.where(col_mask[:, None, :], s + bias[None].astype(jnp.float32), NEG_INF)
    prob = jax.nn.softmax(s, axis=(-2, -1))
    o = jnp.einsum('bhiqrk,birkhd->biqhd', prob.astype(v.dtype), vg)
    return o.reshape(B, T, NA_WIDTH)


def mla_attention(p, q_norm_g, w_q_b, kv_norm_g, w_kv_b, cos, sin):
    B, T, _ = p.shape
    c_q, c_kv, k_pe = jnp.split(p, [MLA_Q_RANK, MLA_Q_RANK + MLA_KV_RANK], -1)
    q = (rms_norm(c_q, q_norm_g) @ w_q_b).reshape(B, T, MLA_HEADS, MLA_NOPE_DIM + MLA_ROPE_DIM)
    kv = (rms_norm(c_kv, kv_norm_g) @ w_kv_b).reshape(B, T, MLA_HEADS, MLA_NOPE_DIM + MLA_V_DIM)
    q_nope, q_pe = jnp.split(q, [MLA_NOPE_DIM], -1)
    k_nope, v = jnp.split(kv, [MLA_NOPE_DIM], -1)
    q_pe = apply_rope(q_pe, cos[:, None, :], sin[:, None, :])
    k_pe = apply_rope(k_pe, cos, sin)
    q = jnp.concatenate([q_nope, q_pe], -1)
    k = jnp.concatenate([k_nope, jnp.broadcast_to(k_pe[:, :, None, :], (B, T, MLA_HEADS, MLA_ROPE_DIM))], -1)
    scale = (MLA_NOPE_DIM + MLA_ROPE_DIM) ** -0.5
    nb = T // Q_BLOCK
    qb = jnp.moveaxis(q.reshape(B, nb, Q_BLOCK, MLA_HEADS, -1), 1, 0)

    def attend(q_blk):
        s = jnp.einsum('bqhd,bkhd->bhqk', q_blk, k).astype(jnp.float32) * scale
        prob = jax.nn.softmax(s, axis=-1)
        return jnp.einsum('bhqk,bkhd->bqhd', prob.astype(v.dtype), v)

    o = lax.map(attend, qb)
    return jnp.moveaxis(o, 0, 1).reshape(B, T, MLA_WIDTH)


def setup_inputs(seed: int = 0) -> dict:
    key = jax.random.key(seed)
    ks = list(jax.random.split(key, 40))
    f32 = jnp.float32
    L = DEPTH

    def nrm(idx, shape, scale):
        return jax.random.normal(ks[idx], shape, f32) * scale

    return {
        "x": nrm(0, (BATCH, SEQ, D_MODEL), 1.0),
        "ln_in_g": 1.0 + nrm(1, (D_MODEL,), 0.05),
        "ln_in_b": nrm(2, (D_MODEL,), 0.02),
        "w_in": nrm(3, (L, D_MODEL, IN_WIDTH), D_MODEL ** -0.5),
        "rw_mu": jax.random.uniform(ks[4], (L, 2, RW_IN), f32, 0.0, 0.5),
        "rw_w0": jax.random.uniform(ks[5], (L, 2, RW_WIDTH), f32, -6.0, -1.0),
        "rw_w2": nrm(6, (L, 2, RW_DECAY_RANK, RW_WIDTH), 0.1),
        "rw_a0": nrm(7, (L, 2, RW_WIDTH), 0.1),
        "rw_a2": nrm(8, (L, 2, RW_ICLR_RANK, RW_WIDTH), 0.1),
        "rw_g2": nrm(9, (L, RW_GATE_RANK, RW_WIDTH), RW_GATE_RANK ** -0.5),
        "rw_k_k": 0.85 + nrm(10, (L, RW_WIDTH), 0.05),
        "rw_k_a": 1.0 + nrm(11, (L, RW_WIDTH), 0.05),
        "rw_r_k": nrm(12, (L, RW_HEADS, HEAD_DIM), 0.1),
        "rw_gn_g": 1.0 + nrm(13, (L, RW_WIDTH), 0.05),
        "rw_gn_b": nrm(14, (L, RW_WIDTH), 0.02),
        "na_rpb": nrm(15, (L, NA_HEADS, 2 * NA_WIN_ROWS - 1, 2 * NA_WIN_COLS - 1), 0.1),
        "mla_q_norm_g": 1.0 + nrm(16, (L, MLA_Q_RANK), 0.05),
        "mla_w_q_b": nrm(17, (L, MLA_Q_RANK, MLA_HEADS * (MLA_NOPE_DIM + MLA_ROPE_DIM)), MLA_Q_RANK ** -0.5),
        "mla_kv_norm_g": 1.0 + nrm(18, (L, MLA_KV_RANK), 0.05),
        "mla_w_kv_b": nrm(19, (L, MLA_KV_RANK, MLA_HEADS * (MLA_NOPE_DIM + MLA_V_DIM)), MLA_KV_RANK ** -0.5),
        "w_out": nrm(20, (L, MIX_WIDTH, D_MODEL), DN_BETA * MIX_WIDTH ** -0.5),
        "ln1_g": 1.0 + nrm(21, (L, D_MODEL), 0.05),
        "ln1_b": nrm(22, (L, D_MODEL), 0.02),
        "w_ffn_gate": nrm(23, (L, D_MODEL, D_FF), D_MODEL ** -0.5),
        "w_ffn_up": nrm(24, (L, D_MODEL, D_FF), D_MODEL ** -0.5),
        "w_ffn_down": nrm(25, (L, D_FF, D_MODEL), DN_BETA * D_FF ** -0.5),
        "ln2_g": 1.0 + nrm(26, (L, D_MODEL), 0.05),
        "ln2_b": nrm(27, (L, D_MODEL), 0.02),
    }


def reference(x, ln_in_g, ln_in_b, w_in, rw_mu, rw_w0, rw_w2, rw_a0, rw_a2, rw_g2, rw_k_k, rw_k_a,
              rw_r_k, rw_gn_g, rw_gn_b, na_rpb, mla_q_norm_g, mla_w_q_b, mla_kv_norm_g, mla_w_kv_b,
              w_out, ln1_g, ln1_b, w_ffn_gate, w_ffn_up, w_ffn_down, ln2_g, ln2_b):
    B, T, _ = x.shape
    cos, sin = rope_tables(T)
    h = layer_norm(x, ln_in_g, ln_in_b)
    for l in range(DEPTH):
        p = h @ w_in[l]
        p_rw, p_na, p_mla = jnp.split(p, [RW_IN, RW_IN + NA_IN], -1)
        o_rw = rwkv7_mixer(p_rw, rw_mu[l], rw_w0[l], rw_w2[l], rw_a0[l], rw_a2[l], rw_g2[l],
                           rw_k_k[l], rw_k_a[l], rw_r_k[l], rw_gn_g[l], rw_gn_b[l])
        o_na = neighbourhood_attention(p_na, na_rpb[l])
        o_mla = mla_attention(p_mla, mla_q_norm_g[l], mla_w_q_b[l], mla_kv_norm_g[l], mla_w_kv_b[l], cos, sin)
        mix = jnp.concatenate([o_rw.astype(h.dtype), o_na.astype(h.dtype), o_mla.astype(h.dtype)], -1) @ w_out[l]
        h = layer_norm(DN_ALPHA * h + mix, ln1_g[l], ln1_b[l])
        ff = (jax.nn.silu(h @ w_ffn_gate[l]) * (h @ w_ffn_up[l])) @ w_ffn_down[l]
        h = layer_norm(DN_ALPHA * h + ff, ln2_g[l], ln2_b[l])
    return h
```

```python
import functools

import jax
import jax.numpy as jnp
import numpy as np
from jax import lax
from jax.experimental import pallas as pl
from jax.experimental.pallas import tpu as pltpu

F32 = jnp.float32
BF16 = jnp.bfloat16

D_MODEL = 1024
HEAD_DIM = 64
GRID_W = 64
RW_HEADS = 4
RW_WIDTH = 256
RW_DECAY_RANK = 32
RW_ICLR_RANK = 32
RW_GATE_RANK = 64
RW_GN_EPS = 64e-5
NA_HEADS = 4
NA_WIDTH = 256
NA_WIN_ROWS = 8
NA_WIN_COLS = 16
MLA_HEADS = 8
MLA_Q_RANK = 256
MLA_KV_RANK = 128
MLA_NOPE_DIM = 64
MLA_ROPE_DIM = 32
MLA_V_DIM = 64
MLA_WIDTH = 512
ROPE_THETA = 10000.0
RW_IN = 960
NA_IN = 768
MLA_IN = 416
D_FF = 2816
LN_EPS = 1e-5
RMS_EPS = 1e-6
NEG_INF = -1e30

LANES = 128
RW_PAD = 1024
MLA_PAD = 640
IN_PAD = RW_PAD + NA_IN + MLA_PAD
MLA_HEAD_PAD = 128
CHUNK = 64
ROW_TILE = 512
FFN_TILE = 1408
Q_TILE = 256
VMEM_LIMIT = 56 * 1024 * 1024


def _cparams(n_axes):
    return pltpu.CompilerParams(dimension_semantics=("arbitrary",) * n_axes,
                                vmem_limit_bytes=VMEM_LIMIT)


def _split_bf16(x, parts):
    out = []
    rem = x
    for i in range(parts):
        p = rem.astype(BF16)
        out.append(p)
        if i + 1 < parts:
            rem = rem - p.astype(F32)
    return out


_NN = (((1,), (0,)), ((), ()))
_NT = (((1,), (1,)), ((), ()))


def _dot(a, b, dims=_NN):
    return lax.dot_general(a, b, dims, preferred_element_type=F32)


def _dot_x3(a, b, dims=_NN):
    ah, al = _split_bf16(a, 2)
    bh, bl = _split_bf16(b, 2)
    return _dot(ah, bh, dims) + (_dot(ah, bl, dims) + _dot(al, bh, dims))


def _dot_exact_rhs(a, b_bf16, dims=_NN):
    a0, a1, a2 = _split_bf16(a, 3)
    return _dot(a0, b_bf16, dims) + (_dot(a1, b_bf16, dims) + _dot(a2, b_bf16, dims))


def _sigmoid(x):
    return 1.0 / (1.0 + jnp.exp(-x))


def _softplus(x):
    return jnp.maximum(x, 0.0) + jnp.log(1.0 + jnp.exp(-jnp.abs(x)))


def _layer_norm(x, g, b):
    mu = jnp.mean(x, axis=-1, keepdims=True)
    xc = x - mu
    var = jnp.mean(xc * xc, axis=-1, keepdims=True)
    return xc * lax.rsqrt(var + LN_EPS) * g + b


def _head_sum_matrix(width, head):
    r = lax.broadcasted_iota(jnp.int32, (width, width), 0) // head
    c = lax.broadcasted_iota(jnp.int32, (width, width), 1) // head
    return jnp.where(r == c, 1.0, 0.0).astype(BF16)


def _token_shift(x, prev_row, next_row, mu0, mu1):
    rows = x.shape[0]
    ridx = lax.broadcasted_iota(jnp.int32, x.shape, 0)
    xp = jnp.where(ridx == 0, prev_row, pltpu.roll(x, 1, 0))
    xn = jnp.where(ridx == rows - 1, next_row, pltpu.roll(x, rows - 1, 0))
    return x + mu0 * (xp - x) + mu1 * (xn - x)


def _ln_body(x_ref, g_ref, b_ref, o_ref):
    o_ref[...] = _layer_norm(x_ref[...], g_ref[...], b_ref[...])


def _ln_call(x, g, b):
    n = x.shape[0]
    return pl.pallas_call(
        _ln_body,
        grid=(n // ROW_TILE,),
        in_specs=[pl.BlockSpec((ROW_TILE, D_MODEL), lambda i: (i, 0)),
                  pl.BlockSpec((1, D_MODEL), lambda i: (0, 0)),
                  pl.BlockSpec((1, D_MODEL), lambda i: (0, 0))],
        out_specs=pl.BlockSpec((ROW_TILE, D_MODEL), lambda i: (i, 0)),
        out_shape=jax.ShapeDtypeStruct((n, D_MODEL), F32),
        compiler_params=_cparams(1),
        name="ln_in",
    )(x, g, b)


def _inproj_body(h_ref, w_ref, rw_ref, na_ref, mla_ref):
    hb = h_ref[...].astype(BF16)
    rw_ref[...] = _dot(hb, w_ref[:, 0:RW_PAD])
    na_ref[...] = _dot(hb, w_ref[:, RW_PAD:RW_PAD + NA_IN]).astype(BF16)
    mla_ref[...] = _dot(hb, w_ref[:, RW_PAD + NA_IN:IN_PAD])


def _inproj_call(h, w):
    n = h.shape[0]
    return pl.pallas_call(
        _inproj_body,
        grid=(n // ROW_TILE,),
        in_specs=[pl.BlockSpec((ROW_TILE, D_MODEL), lambda i: (i, 0)),
                  pl.BlockSpec((D_MODEL, IN_PAD), lambda i: (0, 0))],
        out_specs=[pl.BlockSpec((ROW_TILE, RW_PAD), lambda i: (i, 0)),
                   pl.BlockSpec((ROW_TILE, NA_IN), lambda i: (i, 0)),
                   pl.BlockSpec((ROW_TILE, MLA_PAD), lambda i: (i, 0))],
        out_shape=[jax.ShapeDtypeStruct((n, RW_PAD), F32),
                   jax.ShapeDtypeStruct((n, NA_IN), BF16),
                   jax.ShapeDtypeStruct((n, MLA_PAD), F32)],
        compiler_params=_cparams(1),
        name="in_proj",
    )(h, w)


def _stack_heads(x, head_masks):
    return jnp.concatenate([x * m for m in head_masks], axis=0)


def _unstack_heads(x):
    c = CHUNK
    return (x[0:c] + x[c:2 * c]) + (x[2 * c:3 * c] + x[3 * c:4 * c])


def _neumann_inverse(l):
    n = l.shape[0]
    eye = jnp.where(lax.broadcasted_iota(jnp.int32, (n, n), 0) ==
                    lax.broadcasted_iota(jnp.int32, (n, n), 1), 1.0, 0.0).astype(F32)
    t = eye + l
    p = l
    steps = int(np.log2(CHUNK)) - 1
    for _ in range(steps):
        p = _dot_x3(p, p)
        t = t + _dot_x3(t, p)
    return t


def _rw_direction(x, prev_row, next_row, d, prm, h_ref):
    mu_ref, w0_ref, w2_ref, a0_ref, a2_ref, kk_ref, ka_ref = prm
    c = CHUNK
    w = RW_WIDTH
    xs = _token_shift(x, prev_row, next_row, mu_ref[0:1, :], mu_ref[1:2, :])
    r = xs[:, 0:w]
    k = xs[:, w:2 * w]
    v = xs[:, 2 * w:3 * w]
    lora = xs[:, 3 * w:3 * w + LANES]
    lw = _dot(jnp.tanh(lora).astype(BF16), w2_ref[d])
    la = _dot(lora.astype(BF16), a2_ref[d])
    log_w = -_softplus(-(w0_ref[d:d + 1, :] + lw)) - 0.5
    ldec = -jnp.exp(log_w)
    iclr = _sigmoid(a0_ref[d:d + 1, :] + la)
    kk = k * kk_ref[...]
    ss = _dot_exact_rhs(kk * kk, _head_sum_matrix(w, HEAD_DIM))
    kk = kk / jnp.maximum(jnp.sqrt(ss), 1e-12)
    k_dir = k * (1.0 + (iclr - 1.0) * ka_ref[...])
    a = -kk
    b = kk * iclr

    ti = lax.broadcasted_iota(jnp.int32, (c, c), 0)
    si = lax.broadcasted_iota(jnp.int32, (c, c), 1)
    cum = jnp.where(si <= ti, 1.0, 0.0).astype(BF16)
    l0, l1, l2 = _split_bf16(ldec, 3)
    cl_incl = _dot(cum, l0) + (_dot(cum, l1) + _dot(cum, l2))
    cl_excl = cl_incl - ldec
    tot = cl_incl[c - 1:c, :]
    if d == 0:
        e_a = jnp.exp(cl_excl)
        e_d = jnp.exp(-cl_incl)
        e_r = jnp.exp(cl_incl)
        e_h = jnp.exp(tot - cl_incl)
    else:
        e_a = jnp.exp(tot - cl_incl)
        e_d = jnp.exp(cl_excl - tot)
        e_r = e_a
        e_h = jnp.exp(cl_excl)
    at = a * e_a
    bt = b * e_d
    kt = k_dir * e_d
    rt = r * e_r
    bh = b * e_h
    kh = k_dir * e_h

    lane_head = lax.broadcasted_iota(jnp.int32, (1, w), 1) // HEAD_DIM
    head_masks = [jnp.where(lane_head == h, 1.0, 0.0).astype(F32) for h in range(RW_HEADS)]
    xa = _stack_heads(at, head_masks)
    xr = _stack_heads(rt, head_masks)
    xb = _stack_heads(bt, head_masks)
    xk = _stack_heads(kt, head_masks)
    vst = _stack_heads(v, head_masks)

    n = RW_HEADS * c
    ri = lax.broadcasted_iota(jnp.int32, (n, n), 0)
    ci = lax.broadcasted_iota(jnp.int32, (n, n), 1)
    same = (ri // c) == (ci // c)
    if d == 0:
        strict = same & ((ci % c) < (ri % c))
        incl = same & ((ci % c) <= (ri % c))
    else:
        strict = same & ((ci % c) > (ri % c))
        incl = strict
    lab = jnp.where(strict, _dot_x3(xa, xb, _NT), 0.0)
    lak = jnp.where(strict, _dot_x3(xa, xk, _NT), 0.0)
    mrb = jnp.where(incl, _dot_x3(xr, xb, _NT), 0.0)
    mrk = jnp.where(incl, _dot_x3(xr, xk, _NT), 0.0)

    t = _neumann_inverse(lab)
    q = _dot_x3(lak, vst)
    ta = _dot_x3(t, xa)
    u0 = _dot_x3(t, q)
    rhat = rt + _unstack_heads(_dot_x3(mrb, ta))
    y1 = _unstack_heads(_dot_x3(mrb, u0) + _dot_x3(mrk, vst))
    ta_c = _unstack_heads(ta)
    u0_c = _unstack_heads(u0)

    lane_k = lax.broadcasted_iota(jnp.int32, (w, w), 1)
    row_k = lax.broadcasted_iota(jnp.int32, (w, w), 0)
    bdm = (lane_k // HEAD_DIM) == (row_k // HEAD_DIM)
    bh_t = bh.T
    kh_t = kh.T
    a_bd = jnp.where(bdm, _dot_x3(bh_t, ta_c), 0.0)
    a_bd = a_bd + jnp.where(lane_k == row_k, jnp.exp(tot), 0.0)
    d_bd = jnp.where(bdm, _dot_x3(bh_t, u0_c) + _dot_x3(kh_t, v), 0.0)

    h = h_ref[...]
    y = _dot_x3(rhat, h) + y1
    h_ref[...] = _dot_x3(a_bd, h) + d_bd
    return y


def _rw_body(xf_ref, pf_ref, nf_ref, xb_ref, pb_ref, nb_ref,
             mu_ref, w0_ref, w2_ref, a0_ref, a2_ref, kk_ref, ka_ref,
             yf_ref, yb_ref, hf_ref, hb_ref, *, n_chunks):
    ch = pl.program_id(1)

    @pl.when(ch == 0)
    def _():
        hf_ref[...] = jnp.zeros_like(hf_ref)
        hb_ref[...] = jnp.zeros_like(hb_ref)

    prm = (mu_ref, w0_ref, w2_ref, a0_ref, a2_ref, kk_ref, ka_ref)
    zero = jnp.zeros((1, RW_PAD), F32)
    prev_f = jnp.where(ch == 0, zero, pf_ref[7:8, :])
    next_f = jnp.where(ch == n_chunks - 1, zero, nf_ref[0:1, :])
    yf_ref[...] = _rw_direction(xf_ref[...], prev_f, next_f, 0, prm, hf_ref)
    prev_b = jnp.where(ch == n_chunks - 1, zero, pb_ref[7:8, :])
    next_b = jnp.where(ch == 0, zero, nb_ref[0:1, :])
    yb_ref[...] = _rw_direction(xb_ref[...], prev_b, next_b, 1, prm, hb_ref)


def _rw_call(p_rw, prm, batch, seq):
    n = p_rw.shape[0]
    nc = seq // CHUNK
    c8 = CHUNK // 8
    last8 = n // 8 - 1

    def main_f(b, c):
        return (b * nc + c, 0)

    def prev_f(b, c):
        return (jnp.maximum((b * nc + c) * c8 - 1, 0), 0)

    def next_f(b, c):
        return (jnp.minimum((b * nc + c + 1) * c8, last8), 0)

    def main_b(b, c):
        return (b * nc + (nc - 1 - c), 0)

    def prev_b(b, c):
        return (jnp.maximum((b * nc + (nc - 1 - c)) * c8 - 1, 0), 0)

    def next_b(b, c):
        return (jnp.minimum((b * nc + (nc - c)) * c8, last8), 0)

    def full(shape):
        return pl.BlockSpec(shape, lambda b, c: (0,) * len(shape))

    mu, w0, w2, a0, a2, kk, ka = prm
    return pl.pallas_call(
        functools.partial(_rw_body, n_chunks=nc),
        grid=(batch, nc),
        in_specs=[pl.BlockSpec((CHUNK, RW_PAD), main_f),
                  pl.BlockSpec((8, RW_PAD), prev_f),
                  pl.BlockSpec((8, RW_PAD), next_f),
                  pl.BlockSpec((CHUNK, RW_PAD), main_b),
                  pl.BlockSpec((8, RW_PAD), prev_b),
                  pl.BlockSpec((8, RW_PAD), next_b),
                  full(mu.shape), full(w0.shape), full(w2.shape), full(a0.shape),
                  full(a2.shape), full(kk.shape), full(ka.shape)],
        out_specs=[pl.BlockSpec((CHUNK, RW_WIDTH), main_f),
                   pl.BlockSpec((CHUNK, RW_WIDTH), main_b)],
        out_shape=[jax.ShapeDtypeStruct((n, RW_WIDTH), F32),
                   jax.ShapeDtypeStruct((n, RW_WIDTH), F32)],
        scratch_shapes=[pltpu.VMEM((RW_WIDTH, RW_WIDTH), F32),
                        pltpu.VMEM((RW_WIDTH, RW_WIDTH), F32)],
        compiler_params=_cparams(2),
        name="rwkv_scan",
    )(p_rw, p_rw, p_rw, p_rw, p_rw, p_rw, mu, w0, w2, a0, a2, kk, ka)


def _rwout_body(x_ref, p_ref, n_ref, yf_ref, yb_ref, mu_ref, a0_ref, a2_ref, ka_ref, rk_ref,
                gg_ref, gb_ref, g2_ref, o_ref, *, tiles_per_seq):
    i = pl.program_id(0)
    w = RW_WIDTH
    zero = jnp.zeros((1, RW_PAD), F32)
    prev_row = jnp.where(i % tiles_per_seq == 0, zero, p_ref[7:8, :])
    next_row = jnp.where(i % tiles_per_seq == tiles_per_seq - 1, zero, n_ref[0:1, :])
    xs = _token_shift(x_ref[...], prev_row, next_row, mu_ref[0:1, :], mu_ref[1:2, :])
    r = xs[:, 0:w]
    k = xs[:, w:2 * w]
    v = xs[:, 2 * w:3 * w]
    lora = xs[:, 3 * w:3 * w + LANES]
    gblk = xs[:, 3 * w + LANES:3 * w + 2 * LANES]
    iclr0 = _sigmoid(a0_ref[0:1, :] + _dot(lora.astype(BF16), a2_ref[0]))
    k_dir0 = k * (1.0 + (iclr0 - 1.0) * ka_ref[...])
    gate = _dot(_sigmoid(gblk).astype(BF16), g2_ref[...])
    hsum = _head_sum_matrix(w, HEAD_DIM)
    y = yf_ref[...] + yb_ref[...]
    mean = _dot_exact_rhs(y, hsum) * (1.0 / HEAD_DIM)
    yc = y - mean
    var = _dot_exact_rhs(yc * yc, hsum) * (1.0 / HEAD_DIM)
    yn = yc * lax.rsqrt(var + RW_GN_EPS) * gg_ref[...] + gb_ref[...]
    bonus = _dot_exact_rhs(r * k_dir0 * rk_ref[...], hsum) * v
    o_ref[...] = ((yn + bonus) * gate).astype(BF16)


def _rwout_call(p_rw, yf, yb, prm, seq):
    n = p_rw.shape[0]
    t8 = ROW_TILE // 8
    last8 = n // 8 - 1
    mu, a0, a2, ka, rk, gg, gb, g2 = prm

    def full(shape):
        return pl.BlockSpec(shape, lambda i: (0,) * len(shape))

    return pl.pallas_call(
        functools.partial(_rwout_body, tiles_per_seq=seq // ROW_TILE),
        grid=(n // ROW_TILE,),
        in_specs=[pl.BlockSpec((ROW_TILE, RW_PAD), lambda i: (i, 0)),
                  pl.BlockSpec((8, RW_PAD), lambda i: (jnp.maximum(i * t8 - 1, 0), 0)),
                  pl.BlockSpec((8, RW_PAD), lambda i: (jnp.minimum((i + 1) * t8, last8), 0)),
                  pl.BlockSpec((ROW_TILE, RW_WIDTH), lambda i: (i, 0)),
                  pl.BlockSpec((ROW_TILE, RW_WIDTH), lambda i: (i, 0)),
                  full(mu.shape), full(a0.shape), full(a2.shape), full(ka.shape), full(rk.shape),
                  full(gg.shape), full(gb.shape), full(g2.shape)],
        out_specs=pl.BlockSpec((ROW_TILE, RW_WIDTH), lambda i: (i, 0)),
        out_shape=jax.ShapeDtypeStruct((n, RW_WIDTH), BF16),
        compiler_params=_cparams(1),
        name="rwkv_out",
    )(p_rw, p_rw, p_rw, yf, yb, mu, a0, a2, ka, rk, gg, gb, g2)


def _na_body(q_ref, k_ref, v_ref, bias_ref, o_ref, *, rows):
    i = pl.program_id(1)
    w = NA_WIDTH
    kr = NA_WIN_ROWS
    row_start = jnp.clip(i - kr // 2, 0, rows - kr)
    delta = i - row_start
    start = pl.multiple_of(row_start * GRID_W, GRID_W)
    q = q_ref[...]
    kwin = k_ref[pl.ds(start, kr * GRID_W), :]
    vwin = v_ref[pl.ds(start, kr * GRID_W), :]
    nk = kr * GRID_W
    qc = lax.broadcasted_iota(jnp.int32, (GRID_W, nk), 0)
    kc = lax.broadcasted_iota(jnp.int32, (GRID_W, nk), 1) % GRID_W
    cs = jnp.clip(qc - NA_WIN_COLS // 2, 0, GRID_W - NA_WIN_COLS)
    col_ok = (kc >= cs) & (kc < cs + NA_WIN_COLS)
    lane_head = lax.broadcasted_iota(jnp.int32, (1, w), 1) // HEAD_DIM
    out = jnp.zeros((GRID_W, w), F32)
    for h in range(NA_HEADS):
        hm = lane_head == h
        qh = jnp.where(hm, q, jnp.zeros_like(q))
        s = _dot(qh, kwin, _NT) * (HEAD_DIM ** -0.5)
        s = jnp.where(col_ok, s + bias_ref[delta, h], NEG_INF)
        m = jnp.max(s, axis=-1, keepdims=True)
        e = jnp.exp(s - m)
        prob = e / jnp.sum(e, axis=-1, keepdims=True)
        out = out + jnp.where(hm, _dot(prob.astype(BF16), vwin), 0.0)
    o_ref[...] = out.astype(BF16)


def _na_call(p_na, bias, batch, seq):
    n = p_na.shape[0]
    rows = seq // GRID_W
    return pl.pallas_call(
        functools.partial(_na_body, rows=rows),
        grid=(batch, rows),
        in_specs=[pl.BlockSpec((GRID_W, NA_WIDTH), lambda b, i: (b * rows + i, 0)),
                  pl.BlockSpec((seq, NA_WIDTH), lambda b, i: (b, 1)),
                  pl.BlockSpec((seq, NA_WIDTH), lambda b, i: (b, 2)),
                  pl.BlockSpec(bias.shape, lambda b, i: (0, 0, 0, 0))],
        out_specs=pl.BlockSpec((GRID_W, NA_WIDTH), lambda b, i: (b * rows + i, 0)),
        out_shape=jax.ShapeDtypeStruct((n, NA_WIDTH), BF16),
        compiler_params=_cparams(2),
        name="na_attn",
    )(p_na, p_na, p_na, bias)


def _mlaproj_body(p_ref, qg_ref, kvg_ref, wqa_ref, wqb_ref, wk_ref, wv_ref,
                  cq_ref, sq_ref, ck_ref, sk_ref, q_ref, k_ref, v_ref):
    p = p_ref[...]
    cq = p[:, 0:MLA_Q_RANK]
    ckv = p[:, MLA_Q_RANK:MLA_Q_RANK + MLA_KV_RANK]
    kpe = p[:, MLA_Q_RANK + MLA_KV_RANK:MLA_Q_RANK + MLA_KV_RANK + LANES]
    kpe_rot = p[:, MLA_Q_RANK + MLA_KV_RANK + LANES:MLA_PAD]
    xq = cq * lax.rsqrt(jnp.mean(cq * cq, axis=-1, keepdims=True) + RMS_EPS) * qg_ref[...]
    xq = xq.astype(BF16)
    xkv = ckv * lax.rsqrt(jnp.mean(ckv * ckv, axis=-1, keepdims=True) + RMS_EPS) * kvg_ref[...]
    xkv = xkv.astype(BF16)
    qa = _dot(xq, wqa_ref[...])
    qb = _dot(xq, wqb_ref[...])
    scale = (MLA_NOPE_DIM + MLA_ROPE_DIM) ** -0.5
    kn = _dot(xkv, wk_ref[...])
    kpe_r = kpe * ck_ref[...] + kpe_rot * sk_ref[...]
    cq_t = cq_ref[...]
    sq_t = sq_ref[...]
    for h in range(MLA_HEADS):
        sl = slice(h * MLA_HEAD_PAD, (h + 1) * MLA_HEAD_PAD)
        q_ref[:, sl] = ((qa[:, sl] * cq_t + qb[:, sl] * sq_t) * scale).astype(BF16)
        k_ref[:, sl] = (kn[:, sl] + kpe_r).astype(BF16)
    v_ref[...] = _dot(xkv, wv_ref[...]).astype(BF16)


def _mlaproj_call(p_mla, prm, tables, seq):
    n = p_mla.shape[0]
    qg, kvg, wqa, wqb, wk, wv = prm
    tps = seq // ROW_TILE
    hw = MLA_HEADS * MLA_HEAD_PAD

    def full(shape):
        return pl.BlockSpec(shape, lambda i: (0,) * len(shape))

    tab = pl.BlockSpec((ROW_TILE, LANES), lambda i: (i % tps, 0))
    return pl.pallas_call(
        _mlaproj_body,
        grid=(n // ROW_TILE,),
        in_specs=[pl.BlockSpec((ROW_TILE, MLA_PAD), lambda i: (i, 0)),
                  full(qg.shape), full(kvg.shape), full(wqa.shape), full(wqb.shape),
                  full(wk.shape), full(wv.shape), tab, tab, tab, tab],
        out_specs=[pl.BlockSpec((ROW_TILE, hw), lambda i: (i, 0)),
                   pl.BlockSpec((ROW_TILE, hw), lambda i: (i, 0)),
                   pl.BlockSpec((ROW_TILE, MLA_WIDTH), lambda i: (i, 0))],
        out_shape=[jax.ShapeDtypeStruct((n, hw), BF16),
                   jax.ShapeDtypeStruct((n, hw), BF16),
                   jax.ShapeDtypeStruct((n, MLA_WIDTH), BF16)],
        compiler_params=_cparams(1),
        name="mla_proj",
    )(p_mla, qg, kvg, wqa, wqb, wk, wv, *tables)


def _mlaattn_body(q_ref, k_ref, v_ref, o_ref):
    vv = v_ref[...]
    outs = []
    for h in range(2):
        sl = slice(h * MLA_HEAD_PAD, (h + 1) * MLA_HEAD_PAD)
        s = _dot(q_ref[:, sl], k_ref[:, sl], _NT)
        m = jnp.max(s, axis=-1, keepdims=True)
        e = jnp.exp(s - m)
        l = jnp.sum(e, axis=-1, keepdims=True)
        outs.append(_dot(e.astype(BF16), vv) / l)
    lane = lax.broadcasted_iota(jnp.int32, (Q_TILE, 2 * MLA_V_DIM), 1)
    o_ref[...] = jnp.where(lane < MLA_V_DIM, outs[0], outs[1]).astype(BF16)


def _mlaattn_call(q, k, v, batch, seq):
    n = q.shape[0]
    qt = seq // Q_TILE
    pairs = MLA_HEADS // 2
    return pl.pallas_call(
        _mlaattn_body,
        grid=(batch, pairs, qt),
        in_specs=[pl.BlockSpec((Q_TILE, 2 * MLA_HEAD_PAD), lambda b, hp, i: (b * qt + i, hp)),
                  pl.BlockSpec((seq, 2 * MLA_HEAD_PAD), lambda b, hp, i: (b, hp)),
                  pl.BlockSpec((seq, 2 * MLA_V_DIM), lambda b, hp, i: (b, hp))],
        out_specs=pl.BlockSpec((Q_TILE, 2 * MLA_V_DIM), lambda b, hp, i: (b * qt + i, hp)),
        out_shape=jax.ShapeDtypeStruct((n, MLA_WIDTH), BF16),
        compiler_params=_cparams(3),
        name="mla_attn",
    )(q, k, v)


def _outproj_body(rw_ref, na_ref, mla_ref, h_ref, w_ref, g_ref, b_ref, o_ref, *, alpha):
    mix = _dot(rw_ref[...], w_ref[0:RW_WIDTH, :])
    mix = mix + _dot(na_ref[...], w_ref[RW_WIDTH:RW_WIDTH + NA_WIDTH, :])
    mix = mix + _dot(mla_ref[...], w_ref[RW_WIDTH + NA_WIDTH:, :])
    o_ref[...] = _layer_norm(alpha * h_ref[...] + mix, g_ref[...], b_ref[...])


def _outproj_call(o_rw, o_na, o_mla, h, w, g, b, alpha):
    n = h.shape[0]
    row = lambda width: pl.BlockSpec((ROW_TILE, width), lambda i: (i, 0))
    return pl.pallas_call(
        functools.partial(_outproj_body, alpha=alpha),
        grid=(n // ROW_TILE,),
        in_specs=[row(RW_WIDTH), row(NA_WIDTH), row(MLA_WIDTH), row(D_MODEL),
                  pl.BlockSpec((D_MODEL, D_MODEL), lambda i: (0, 0)),
                  pl.BlockSpec((1, D_MODEL), lambda i: (0, 0)),
                  pl.BlockSpec((1, D_MODEL), lambda i: (0, 0))],
        out_specs=row(D_MODEL),
        out_shape=jax.ShapeDtypeStruct((n, D_MODEL), F32),
        compiler_params=_cparams(1),
        name="out_proj",
    )(o_rw, o_na, o_mla, h, w, g, b)


def _ffn_body(h_ref, wg_ref, wu_ref, wd_ref, g_ref, b_ref, o_ref, acc_ref, *, alpha, steps):
    j = pl.program_id(1)
    hb = h_ref[...].astype(BF16)
    gt = _dot(hb, wg_ref[...])
    up = _dot(hb, wu_ref[...])
    act = (gt * _sigmoid(gt) * up).astype(BF16)
    part = _dot(act, wd_ref[...])

    @pl.when(j == 0)
    def _():
        acc_ref[...] = part

    @pl.when(j > 0)
    def _():
        acc_ref[...] += part

    @pl.when(j == steps - 1)
    def _():
        o_ref[...] = _layer_norm(alpha * h_ref[...] + acc_ref[...], g_ref[...], b_ref[...])


def _ffn_call(h, wg, wu, wd, g, b, alpha):
    n = h.shape[0]
    steps = D_FF // FFN_TILE
    return pl.pallas_call(
        functools.partial(_ffn_body, alpha=alpha, steps=steps),
        grid=(n // ROW_TILE, steps),
        in_specs=[pl.BlockSpec((ROW_TILE, D_MODEL), lambda i, j: (i, 0)),
                  pl.BlockSpec((D_MODEL, FFN_TILE), lambda i, j: (0, j)),
                  pl.BlockSpec((D_MODEL, FFN_TILE), lambda i, j: (0, j)),
                  pl.BlockSpec((FFN_TILE, D_MODEL), lambda i, j: (j, 0)),
                  pl.BlockSpec((1, D_MODEL), lambda i, j: (0, 0)),
                  pl.BlockSpec((1, D_MODEL), lambda i, j: (0, 0))],
        out_specs=pl.BlockSpec((ROW_TILE, D_MODEL), lambda i, j: (i, 0)),
        out_shape=jax.ShapeDtypeStruct((n, D_MODEL), F32),
        scratch_shapes=[pltpu.VMEM((ROW_TILE, D_MODEL), F32)],
        compiler_params=_cparams(2),
        name="ffn",
    )(h, wg, wu, wd, g, b)


def _rope_rotate_cols(w_pe):
    half = MLA_ROPE_DIM // 2
    return jnp.concatenate([-w_pe[..., half:], w_pe[..., :half]], axis=-1)


def _prep_in_proj(w_in):
    d = w_in.shape[0]
    rw = w_in[:, :RW_IN]
    na = w_in[:, RW_IN:RW_IN + NA_IN]
    mla = w_in[:, RW_IN + NA_IN:]
    cq_ckv = mla[:, :MLA_Q_RANK + MLA_KV_RANK]
    kpe = mla[:, MLA_Q_RANK + MLA_KV_RANK:]
    z = lambda c: jnp.zeros((d, c), w_in.dtype)
    tail = LANES - MLA_NOPE_DIM - MLA_ROPE_DIM
    w = jnp.concatenate([rw, z(RW_PAD - RW_IN), na, cq_ckv,
                         z(MLA_NOPE_DIM), kpe, z(tail),
                         z(MLA_NOPE_DIM), _rope_rotate_cols(kpe), z(tail)], axis=1)
    return w.astype(BF16)


def _prep_lora(w2, row_offset):
    out = jnp.zeros((2, LANES, w2.shape[-1]), w2.dtype)
    rank = w2.shape[1]
    for d in range(2):
        out = out.at[d, row_offset + d * rank:row_offset + (d + 1) * rank].set(w2[d])
    return out.astype(BF16)


def _prep_mla_weights(w_q_b, w_kv_b):
    qd = MLA_NOPE_DIM + MLA_ROPE_DIM
    wq = w_q_b.reshape(MLA_Q_RANK, MLA_HEADS, qd)
    tail = MLA_HEAD_PAD - qd
    zq = lambda c: jnp.zeros((MLA_Q_RANK, MLA_HEADS, c), w_q_b.dtype)
    wqa = jnp.concatenate([wq, zq(tail)], axis=-1).reshape(MLA_Q_RANK, -1)
    wqb = jnp.concatenate([zq(MLA_NOPE_DIM), _rope_rotate_cols(wq[..., MLA_NOPE_DIM:]), zq(tail)],
                          axis=-1).reshape(MLA_Q_RANK, -1)
    wkv = w_kv_b.reshape(MLA_KV_RANK, MLA_HEADS, MLA_NOPE_DIM + MLA_V_DIM)
    zk = jnp.zeros((MLA_KV_RANK, MLA_HEADS, MLA_HEAD_PAD - MLA_NOPE_DIM), w_kv_b.dtype)
    wk = jnp.concatenate([wkv[..., :MLA_NOPE_DIM], zk], axis=-1).reshape(MLA_KV_RANK, -1)
    wv = wkv[..., MLA_NOPE_DIM:].reshape(MLA_KV_RANK, -1)
    return wqa.astype(BF16), wqb.astype(BF16), wk.astype(BF16), wv.astype(BF16)


def _rope_tables(seq):
    inv_freq = jnp.power(ROPE_THETA, -jnp.arange(0, MLA_ROPE_DIM, 2, dtype=F32) / MLA_ROPE_DIM)
    ang = jnp.arange(seq, dtype=F32)[:, None] * inv_freq[None, :]
    cos2 = jnp.concatenate([jnp.cos(ang), jnp.cos(ang)], axis=-1)
    sin2 = jnp.concatenate([jnp.sin(ang), jnp.sin(ang)], axis=-1)
    tail = LANES - MLA_NOPE_DIM - MLA_ROPE_DIM
    one = jnp.ones((seq, MLA_NOPE_DIM), F32)
    z = lambda c: jnp.zeros((seq, c), F32)
    cq = jnp.concatenate([one, cos2, z(tail)], axis=-1)
    sq = jnp.concatenate([z(MLA_NOPE_DIM), sin2, z(tail)], axis=-1)
    ck = jnp.concatenate([z(MLA_NOPE_DIM), cos2, z(tail)], axis=-1)
    return cq, sq, ck, sq


def _na_bias_table(rpb, rows):
    kr = min(NA_WIN_ROWS, rows)
    j = np.arange(GRID_W)
    dc = np.clip(j[None, :] - j[:, None] + (NA_WIN_COLS - 1), 0, 2 * NA_WIN_COLS - 2)
    delta = np.arange(kr)
    dr = np.arange(kr)[None, :] - delta[:, None] + (NA_WIN_ROWS - 1)
    t = rpb[:, dr[:, :, None, None], dc[None, None, :, :]]
    t = jnp.transpose(t, (1, 0, 3, 2, 4))
    return t.reshape(kr, rpb.shape[0], GRID_W, kr * GRID_W).astype(F32)


def kernel(x, ln_in_g, ln_in_b, w_in, rw_mu, rw_w0, rw_w2, rw_a0, rw_a2, rw_g2, rw_k_k, rw_k_a,
           rw_r_k, rw_gn_g, rw_gn_b, na_rpb, mla_q_norm_g, mla_w_q_b, mla_kv_norm_g, mla_w_kv_b,
           w_out, ln1_g, ln1_b, w_ffn_gate, w_ffn_up, w_ffn_down, ln2_g, ln2_b):
    batch, seq, d = x.shape
    depth = w_in.shape[0]
    assert d == D_MODEL and seq % ROW_TILE == 0 and seq % CHUNK == 0 and seq % GRID_W == 0
    assert seq // GRID_W >= NA_WIN_ROWS and D_FF % FFN_TILE == 0
    n = batch * seq
    alpha = float((2 * depth) ** 0.25)
    row = lambda a: a.reshape(1, -1)
    tables = _rope_tables(seq)

    h = _ln_call(x.reshape(n, d), row(ln_in_g), row(ln_in_b))
    for l in range(depth):
        p_rw, p_na, p_mla = _inproj_call(h, _prep_in_proj(w_in[l]))

        mu = jnp.pad(rw_mu[l], ((0, 0), (0, RW_PAD - RW_IN)))
        a2 = _prep_lora(rw_a2[l], 2 * RW_DECAY_RANK)
        scan_prm = (mu, rw_w0[l], _prep_lora(rw_w2[l], 0), rw_a0[l], a2,
                    row(rw_k_k[l]), row(rw_k_a[l]))
        y_f, y_b = _rw_call(p_rw, scan_prm, batch, seq)
        g2 = jnp.pad(rw_g2[l], ((0, LANES - RW_GATE_RANK), (0, 0))).astype(BF16)
        out_prm = (mu, rw_a0[l], a2, row(rw_k_a[l]), row(rw_r_k[l]),
                   row(rw_gn_g[l]), row(rw_gn_b[l]), g2)
        o_rw = _rwout_call(p_rw, y_f, y_b, out_prm, seq)

        o_na = _na_call(p_na, _na_bias_table(na_rpb[l], seq // GRID_W), batch, seq)

        mla_prm = (row(mla_q_norm_g[l]), row(mla_kv_norm_g[l]),
                   *_prep_mla_weights(mla_w_q_b[l], mla_w_kv_b[l]))
        q, k, v = _mlaproj_call(p_mla, mla_prm, tables, seq)
        o_mla = _mlaattn_call(q, k, v, batch, seq)

        h = _outproj_call(o_rw, o_na, o_mla, h, w_out[l].astype(BF16),
                          row(ln1_g[l]), row(ln1_b[l]), alpha)
        h = _ffn_call(h, w_ffn_gate[l].astype(BF16), w_ffn_up[l].astype(BF16),
                      w_ffn_down[l].astype(BF16), row(ln2_g[l]), row(ln2_b[l]), alpha)
    return h.reshape(batch, seq, d)
```

```python
import functools

import jax
import jax.numpy as jnp
import numpy as np
from jax import lax
from jax.experimental import pallas as pl
from jax.experimental.pallas import tpu as pltpu

F32 = jnp.float32
BF16 = jnp.bfloat16

D_MODEL = 1024
HEAD_DIM = 64
GRID_W = 64
RW_HEADS = 4
RW_WIDTH = 256
RW_DECAY_RANK = 32
RW_ICLR_RANK = 32
RW_GATE_RANK = 64
RW_GN_EPS = 64e-5
NA_HEADS = 4
NA_WIDTH = 256
NA_WIN_ROWS = 8
NA_WIN_COLS = 16
MLA_HEADS = 8
MLA_Q_RANK = 256
MLA_KV_RANK = 128
MLA_NOPE_DIM = 64
MLA_ROPE_DIM = 32
MLA_V_DIM = 64
MLA_WIDTH = 512
ROPE_THETA = 10000.0
RW_IN = 960
NA_IN = 768
MLA_IN = 416
D_FF = 2816
LN_EPS = 1e-5
RMS_EPS = 1e-6
NEG_INF = -1e30
LOG2_E = 1.4426950408889634

LANES = 128
RW_PAD = 1024
MLA_PAD = 640
IN_PAD = RW_PAD + NA_IN + MLA_PAD
MLA_HEAD_PAD = 128
CHUNK = 64
RW_BATCH_PER_STEP = 2
ROW_TILE = 512
FFN_TILE = 1408
Q_TILE = 256
NA_ROWS_PER_STEP = 4
VMEM_LIMIT = 56 * 1024 * 1024


def _cparams(n_axes):
    return pltpu.CompilerParams(dimension_semantics=("arbitrary",) * n_axes,
                                vmem_limit_bytes=VMEM_LIMIT)


def _split_bf16(x, parts):
    out = []
    rem = x
    for i in range(parts):
        p = rem.astype(BF16)
        out.append(p)
        if i + 1 < parts:
            rem = rem - p.astype(F32)
    return out


_NN = (((1,), (0,)), ((), ()))
_NT = (((1,), (1,)), ((), ()))


def _dot(a, b, dims=_NN):
    return lax.dot_general(a, b, dims, preferred_element_type=F32)


def _dot_x3(a, b, dims=_NN):
    ah, al = _split_bf16(a, 2)
    bh, bl = _split_bf16(b, 2)
    return _dot(ah, bh, dims) + (_dot(ah, bl, dims) + _dot(al, bh, dims))


def _dot1(a, b, dims=_NN):
    return _dot(a.astype(BF16), b.astype(BF16), dims)


def _dot_exact_rhs(a, b_bf16, dims=_NN):
    a0, a1, a2 = _split_bf16(a, 3)
    return _dot(a0, b_bf16, dims) + (_dot(a1, b_bf16, dims) + _dot(a2, b_bf16, dims))


def _sigmoid(x):
    return 1.0 / (1.0 + jnp.exp(-x))


def _softplus(x):
    return jnp.maximum(x, 0.0) + jnp.log(1.0 + jnp.exp(-jnp.abs(x)))


def _layer_norm(x, g, b):
    mu = jnp.mean(x, axis=-1, keepdims=True)
    xc = x - mu
    var = jnp.mean(xc * xc, axis=-1, keepdims=True)
    return xc * lax.rsqrt(var + LN_EPS) * g + b


def _head_sum_matrix(width, head):
    r = lax.broadcasted_iota(jnp.int32, (width, width), 0) // head
    c = lax.broadcasted_iota(jnp.int32, (width, width), 1) // head
    return jnp.where(r == c, 1.0, 0.0).astype(BF16)


def _token_shift(x, prev_row, next_row, mu0, mu1):
    rows = x.shape[0]
    ridx = lax.broadcasted_iota(jnp.int32, x.shape, 0)
    xp = jnp.where(ridx == 0, prev_row, pltpu.roll(x, 1, 0))
    xn = jnp.where(ridx == rows - 1, next_row, pltpu.roll(x, rows - 1, 0))
    return x + mu0 * (xp - x) + mu1 * (xn - x)


def _ln_body(x_ref, g_ref, b_ref, o_ref):
    o_ref[...] = _layer_norm(x_ref[...], g_ref[...], b_ref[...])


def _ln_call(x, g, b):
    n = x.shape[0]
    return pl.pallas_call(
        _ln_body,
        grid=(n // ROW_TILE,),
        in_specs=[pl.BlockSpec((ROW_TILE, D_MODEL), lambda i: (i, 0)),
                  pl.BlockSpec((1, D_MODEL), lambda i: (0, 0)),
                  pl.BlockSpec((1, D_MODEL), lambda i: (0, 0))],
        out_specs=pl.BlockSpec((ROW_TILE, D_MODEL), lambda i: (i, 0)),
        out_shape=jax.ShapeDtypeStruct((n, D_MODEL), F32),
        compiler_params=_cparams(1),
        name="ln_in",
    )(x, g, b)


def _inproj_body(h_ref, w_ref, rw_ref, na_ref, mla_ref):
    hb = h_ref[...].astype(BF16)
    rw_ref[...] = _dot(hb, w_ref[:, 0:RW_PAD])
    na_ref[...] = _dot(hb, w_ref[:, RW_PAD:RW_PAD + NA_IN]).astype(BF16)
    mla_ref[...] = _dot(hb, w_ref[:, RW_PAD + NA_IN:IN_PAD])


def _inproj_call(h, w):
    n = h.shape[0]
    return pl.pallas_call(
        _inproj_body,
        grid=(n // ROW_TILE,),
        in_specs=[pl.BlockSpec((ROW_TILE, D_MODEL), lambda i: (i, 0)),
                  pl.BlockSpec((D_MODEL, IN_PAD), lambda i: (0, 0))],
        out_specs=[pl.BlockSpec((ROW_TILE, RW_PAD), lambda i: (i, 0)),
                   pl.BlockSpec((ROW_TILE, NA_IN), lambda i: (i, 0)),
                   pl.BlockSpec((ROW_TILE, MLA_PAD), lambda i: (i, 0))],
        out_shape=[jax.ShapeDtypeStruct((n, RW_PAD), F32),
                   jax.ShapeDtypeStruct((n, NA_IN), BF16),
                   jax.ShapeDtypeStruct((n, MLA_PAD), F32)],
        compiler_params=_cparams(1),
        name="in_proj",
    )(h, w)


def _stack_heads(x, head_masks):
    xb = x.astype(BF16)
    return jnp.concatenate([jnp.where(m, xb, jnp.zeros_like(xb)) for m in head_masks], axis=0)


def _unstack_heads(x):
    c = x.shape[0] // 4
    return (x[0:c] + x[c:2 * c]) + (x[2 * c:3 * c] + x[3 * c:4 * c])


def _neumann_inverse(l):
    n = l.shape[0]
    eye = jnp.where(lax.broadcasted_iota(jnp.int32, (n, n), 0) ==
                    lax.broadcasted_iota(jnp.int32, (n, n), 1), 1.0, 0.0).astype(F32)
    t = eye + l
    p = l
    steps = int(np.log2(CHUNK)) - 1
    for _ in range(steps):
        pb = p.astype(BF16)
        p = _dot(pb, pb)
        t = t + _dot1(t, p)
        yield None
    return t


def _rw_direction(x, prev_row, next_row, d, prm, h_ref):
    mu_ref, w0_ref, w2_ref, a0_ref, a2_ref, kk_ref, ka_ref = prm
    c = CHUNK
    w = RW_WIDTH
    xs = _token_shift(x, prev_row, next_row, mu_ref[0:1, :], mu_ref[1:2, :])
    r = xs[:, 0:w]
    k = xs[:, w:2 * w]
    v = xs[:, 2 * w:3 * w]
    lora = xs[:, 3 * w:3 * w + LANES]
    lw = _dot(jnp.tanh(lora).astype(BF16), w2_ref[d])
    la = _dot(lora.astype(BF16), a2_ref[d])
    log_w = -_softplus(-(w0_ref[d:d + 1, :] + lw)) - 0.5
    ldec = -jnp.exp(log_w)
    iclr = _sigmoid(a0_ref[d:d + 1, :] + la)
    kk = k * kk_ref[...]
    ss = _dot_exact_rhs(kk * kk, _head_sum_matrix(w, HEAD_DIM))
    kk = kk / jnp.maximum(jnp.sqrt(ss), 1e-12)
    k_dir = k * (1.0 + (iclr - 1.0) * ka_ref[...])
    a = -kk
    b = kk * iclr

    ti = lax.broadcasted_iota(jnp.int32, (c, c), 0)
    si = lax.broadcasted_iota(jnp.int32, (c, c), 1)
    cum = jnp.where(si <= ti, 1.0, 0.0).astype(BF16)
    l0, l1, l2 = _split_bf16(ldec, 3)
    cl_incl = _dot(cum, l0) + (_dot(cum, l1) + _dot(cum, l2))
    cl_excl = cl_incl - ldec
    tot = cl_incl[c - 1:c, :]
    if d == 0:
        e_a = jnp.exp(cl_excl)
        e_d = jnp.exp(-cl_incl)
        e_r = jnp.exp(cl_incl)
        e_h = jnp.exp(tot - cl_incl)
    else:
        e_a = jnp.exp(tot - cl_incl)
        e_d = jnp.exp(cl_excl - tot)
        e_r = e_a
        e_h = jnp.exp(cl_excl)
    at = a * e_a
    bt = b * e_d
    kt = k_dir * e_d
    rt = r * e_r
    bh = b * e_h
    kh = k_dir * e_h

    lane_head = lax.broadcasted_iota(jnp.int32, (1, w), 1) // HEAD_DIM
    head_masks = [lane_head == h for h in range(RW_HEADS)]
    xa = _stack_heads(at, head_masks)
    xr = _stack_heads(rt, head_masks)
    xb = _stack_heads(bt, head_masks)
    xk = _stack_heads(kt, head_masks)
    vst = _stack_heads(v, head_masks)

    n = RW_HEADS * c
    ri = lax.broadcasted_iota(jnp.int32, (n, n), 0)
    ci = lax.broadcasted_iota(jnp.int32, (n, n), 1)
    same = (ri // c) == (ci // c)
    if d == 0:
        strict = same & ((ci % c) < (ri % c))
        incl = same & ((ci % c) <= (ri % c))
    else:
        strict = same & ((ci % c) > (ri % c))
        incl = strict
    yield None
    lab = jnp.where(strict, _dot(xa, xb, _NT), 0.0)
    lak = jnp.where(strict, _dot(xa, xk, _NT), 0.0).astype(BF16)
    mrb = jnp.where(incl, _dot(xr, xb, _NT), 0.0).astype(BF16)
    mrk = jnp.where(incl, _dot(xr, xk, _NT), 0.0).astype(BF16)
    yield None

    t = (yield from _neumann_inverse(lab)).astype(BF16)
    q = _dot(lak, vst)
    ta = _dot(t, xa)
    yield None
    u0 = _dot(t, q.astype(BF16))
    rhat = rt + _unstack_heads(_dot(mrb, ta.astype(BF16)))
    yield None
    y1 = _unstack_heads(_dot(mrb, u0.astype(BF16)) + _dot(mrk, vst))
    ta_c = _unstack_heads(ta)
    u0_c = _unstack_heads(u0)

    lane_k = lax.broadcasted_iota(jnp.int32, (w, w), 1)
    row_k = lax.broadcasted_iota(jnp.int32, (w, w), 0)
    bdm = (lane_k // HEAD_DIM) == (row_k // HEAD_DIM)
    bh_t = bh.T.astype(BF16)
    kh_t = kh.T.astype(BF16)
    a_bd = jnp.where(bdm, _dot1(bh_t, ta_c), 0.0)
    a_bd = a_bd + jnp.where(lane_k == row_k, jnp.exp(tot), 0.0)
    d_bd = jnp.where(bdm, _dot1(bh_t, u0_c) + _dot1(kh_t, v), 0.0)
    yield None

    h = h_ref[...]
    y = _dot_x3(rhat, h) + y1
    h_ref[...] = _dot_x3(a_bd, h) + d_bd
    return y


def _run_interleaved(gens):
    results = [None] * len(gens)
    live = list(range(len(gens)))
    while live:
        for i in list(live):
            try:
                next(gens[i])
            except StopIteration as stop:
                results[i] = stop.value
                live.remove(i)
    return results


def _rw_body(xf_ref, pf_ref, nf_ref, xb_ref, pb_ref, nb_ref,
             mu_ref, w0_ref, w2_ref, a0_ref, a2_ref, kk_ref, ka_ref,
             yf_ref, yb_ref, hf_ref, hb_ref, *, n_chunks):
    ch = pl.program_id(1)

    @pl.when(ch == 0)
    def _():
        hf_ref[...] = jnp.zeros_like(hf_ref)
        hb_ref[...] = jnp.zeros_like(hb_ref)

    prm = (mu_ref, w0_ref, w2_ref, a0_ref, a2_ref, kk_ref, ka_ref)
    zero = jnp.zeros((1, RW_PAD), F32)
    gens = []
    for bb in range(RW_BATCH_PER_STEP):
        prev_f = jnp.where(ch == 0, zero, pf_ref[bb, 7:8, :])
        next_f = jnp.where(ch == n_chunks - 1, zero, nf_ref[bb, 0:1, :])
        prev_b = jnp.where(ch == n_chunks - 1, zero, pb_ref[bb, 7:8, :])
        next_b = jnp.where(ch == 0, zero, nb_ref[bb, 0:1, :])
        gens.append(_rw_direction(xf_ref[bb], prev_f, next_f, 0, prm, hf_ref.at[bb]))
        gens.append(_rw_direction(xb_ref[bb], prev_b, next_b, 1, prm, hb_ref.at[bb]))
    ys = _run_interleaved(gens)
    for bb in range(RW_BATCH_PER_STEP):
        yf_ref[bb] = ys[2 * bb]
        yb_ref[bb] = ys[2 * bb + 1]


def _rw_call(p_rw, prm, batch, seq):
    nc = seq // CHUNK
    c8 = CHUNK // 8
    last8 = seq // 8 - 1
    bps = RW_BATCH_PER_STEP

    def main_f(b, c):
        return (b, c, 0)

    def prev_f(b, c):
        return (b, jnp.maximum(c * c8 - 1, 0), 0)

    def next_f(b, c):
        return (b, jnp.minimum((c + 1) * c8, last8), 0)

    def main_b(b, c):
        return (b, nc - 1 - c, 0)

    def prev_b(b, c):
        return (b, jnp.maximum((nc - 1 - c) * c8 - 1, 0), 0)

    def next_b(b, c):
        return (b, jnp.minimum((nc - c) * c8, last8), 0)

    def full(shape):
        return pl.BlockSpec(shape, lambda b, c: (0,) * len(shape))

    mu, w0, w2, a0, a2, kk, ka = prm
    return pl.pallas_call(
        functools.partial(_rw_body, n_chunks=nc),
        grid=(batch // bps, nc),
        in_specs=[pl.BlockSpec((bps, CHUNK, RW_PAD), main_f),
                  pl.BlockSpec((bps, 8, RW_PAD), prev_f),
                  pl.BlockSpec((bps, 8, RW_PAD), next_f),
                  pl.BlockSpec((bps, CHUNK, RW_PAD), main_b),
                  pl.BlockSpec((bps, 8, RW_PAD), prev_b),
                  pl.BlockSpec((bps, 8, RW_PAD), next_b),
                  full(mu.shape), full(w0.shape), full(w2.shape), full(a0.shape),
                  full(a2.shape), full(kk.shape), full(ka.shape)],
        out_specs=[pl.BlockSpec((bps, CHUNK, RW_WIDTH), main_f),
                   pl.BlockSpec((bps, CHUNK, RW_WIDTH), main_b)],
        out_shape=[jax.ShapeDtypeStruct((batch, seq, RW_WIDTH), F32),
                   jax.ShapeDtypeStruct((batch, seq, RW_WIDTH), F32)],
        scratch_shapes=[pltpu.VMEM((bps, RW_WIDTH, RW_WIDTH), F32),
                        pltpu.VMEM((bps, RW_WIDTH, RW_WIDTH), F32)],
        compiler_params=_cparams(2),
        name="rwkv_scan",
    )(p_rw, p_rw, p_rw, p_rw, p_rw, p_rw, mu, w0, w2, a0, a2, kk, ka)


def _rwout_body(x_ref, p_ref, n_ref, yf_ref, yb_ref, mu_ref, a0_ref, a2_ref, ka_ref, rk_ref,
                gg_ref, gb_ref, g2_ref, o_ref, *, tiles_per_seq):
    i = pl.program_id(0)
    w = RW_WIDTH
    zero = jnp.zeros((1, RW_PAD), F32)
    prev_row = jnp.where(i % tiles_per_seq == 0, zero, p_ref[7:8, :])
    next_row = jnp.where(i % tiles_per_seq == tiles_per_seq - 1, zero, n_ref[0:1, :])
    xs = _token_shift(x_ref[...], prev_row, next_row, mu_ref[0:1, :], mu_ref[1:2, :])
    r = xs[:, 0:w]
    k = xs[:, w:2 * w]
    v = xs[:, 2 * w:3 * w]
    lora = xs[:, 3 * w:3 * w + LANES]
    gblk = xs[:, 3 * w + LANES:3 * w + 2 * LANES]
    iclr0 = _sigmoid(a0_ref[0:1, :] + _dot(lora.astype(BF16), a2_ref[0]))
    k_dir0 = k * (1.0 + (iclr0 - 1.0) * ka_ref[...])
    gate = _dot(_sigmoid(gblk).astype(BF16), g2_ref[...])
    hsum = _head_sum_matrix(w, HEAD_DIM)
    y = yf_ref[...] + yb_ref[...]
    mean = _dot_exact_rhs(y, hsum) * (1.0 / HEAD_DIM)
    yc = y - mean
    var = _dot_exact_rhs(yc * yc, hsum) * (1.0 / HEAD_DIM)
    yn = yc * lax.rsqrt(var + RW_GN_EPS) * gg_ref[...] + gb_ref[...]
    bonus = _dot_exact_rhs(r * k_dir0 * rk_ref[...], hsum) * v
    o_ref[...] = ((yn + bonus) * gate).astype(BF16)


def _rwout_call(p_rw, yf, yb, prm, seq):
    n = p_rw.shape[0]
    t8 = ROW_TILE // 8
    last8 = n // 8 - 1
    mu, a0, a2, ka, rk, gg, gb, g2 = prm

    def full(shape):
        return pl.BlockSpec(shape, lambda i: (0,) * len(shape))

    return pl.pallas_call(
        functools.partial(_rwout_body, tiles_per_seq=seq // ROW_TILE),
        grid=(n // ROW_TILE,),
        in_specs=[pl.BlockSpec((ROW_TILE, RW_PAD), lambda i: (i, 0)),
                  pl.BlockSpec((8, RW_PAD), lambda i: (jnp.maximum(i * t8 - 1, 0), 0)),
                  pl.BlockSpec((8, RW_PAD), lambda i: (jnp.minimum((i + 1) * t8, last8), 0)),
                  pl.BlockSpec((ROW_TILE, RW_WIDTH), lambda i: (i, 0)),
                  pl.BlockSpec((ROW_TILE, RW_WIDTH), lambda i: (i, 0)),
                  full(mu.shape), full(a0.shape), full(a2.shape), full(ka.shape), full(rk.shape),
                  full(gg.shape), full(gb.shape), full(g2.shape)],
        out_specs=pl.BlockSpec((ROW_TILE, RW_WIDTH), lambda i: (i, 0)),
        out_shape=jax.ShapeDtypeStruct((n, RW_WIDTH), BF16),
        compiler_params=_cparams(1),
        name="rwkv_out",
    )(p_rw, p_rw, p_rw, yf, yb, mu, a0, a2, ka, rk, gg, gb, g2)


def _na_body(q_ref, k_ref, v_ref, bias_ref, o_ref, *, rows):
    g = pl.program_id(1)
    w = NA_WIDTH
    kr = NA_WIN_ROWS
    nk = kr * GRID_W
    nq = NA_HEADS * GRID_W
    qc = lax.broadcasted_iota(jnp.int32, (nq, nk), 0) % GRID_W
    kc = lax.broadcasted_iota(jnp.int32, (nq, nk), 1) % GRID_W
    cs = jnp.clip(qc - NA_WIN_COLS // 2, 0, GRID_W - NA_WIN_COLS)
    col_ok = (kc >= cs) & (kc < cs + NA_WIN_COLS)
    lane_head = lax.broadcasted_iota(jnp.int32, (1, w), 1) // HEAD_DIM
    head_masks = [lane_head == h for h in range(NA_HEADS)]
    own_head = (lax.broadcasted_iota(jnp.int32, (nq, w), 0) // GRID_W ==
                lax.broadcasted_iota(jnp.int32, (nq, w), 1) // HEAD_DIM)
    for j in range(NA_ROWS_PER_STEP):
        i = g * NA_ROWS_PER_STEP + j
        row_start = jnp.clip(i - kr // 2, 0, rows - kr)
        delta = i - row_start
        start = pl.multiple_of(row_start * GRID_W, GRID_W)
        q = q_ref[j * GRID_W:(j + 1) * GRID_W, :] * (HEAD_DIM ** -0.5)
        q_st = jnp.concatenate([jnp.where(m, q, jnp.zeros_like(q)) for m in head_masks], axis=0)
        kwin = k_ref[pl.ds(start, nk), :]
        vwin = v_ref[pl.ds(start, nk), :]
        s = jnp.where(col_ok, _dot(q_st, kwin, _NT) + bias_ref[delta], NEG_INF)
        m = jnp.max(s, axis=-1, keepdims=True)
        e = jnp.exp(s - m)
        prob = e / jnp.sum(e, axis=-1, keepdims=True)
        o_st = jnp.where(own_head, _dot(prob.astype(BF16), vwin), 0.0)
        o_ref[j * GRID_W:(j + 1) * GRID_W, :] = _unstack_heads(o_st).astype(BF16)


def _na_call(p_na, bias, batch, seq):
    n = p_na.shape[0]
    rows = seq // GRID_W
    steps = rows // NA_ROWS_PER_STEP
    blk = NA_ROWS_PER_STEP * GRID_W
    return pl.pallas_call(
        functools.partial(_na_body, rows=rows),
        grid=(batch, steps),
        in_specs=[pl.BlockSpec((blk, NA_WIDTH), lambda b, i: (b * steps + i, 0)),
                  pl.BlockSpec((seq, NA_WIDTH), lambda b, i: (b, 1)),
                  pl.BlockSpec((seq, NA_WIDTH), lambda b, i: (b, 2)),
                  pl.BlockSpec(bias.shape, lambda b, i: (0, 0, 0))],
        out_specs=pl.BlockSpec((blk, NA_WIDTH), lambda b, i: (b * steps + i, 0)),
        out_shape=jax.ShapeDtypeStruct((n, NA_WIDTH), BF16),
        compiler_params=_cparams(2),
        name="na_attn",
    )(p_na, p_na, p_na, bias)


def _mlaproj_body(p_ref, qg_ref, kvg_ref, wqa_ref, wqb_ref, wk_ref, wv_ref,
                  cq_ref, sq_ref, ck_ref, sk_ref, q_ref, k_ref, v_ref):
    p = p_ref[...]
    cq = p[:, 0:MLA_Q_RANK]
    ckv = p[:, MLA_Q_RANK:MLA_Q_RANK + MLA_KV_RANK]
    kpe = p[:, MLA_Q_RANK + MLA_KV_RANK:MLA_Q_RANK + MLA_KV_RANK + LANES]
    kpe_rot = p[:, MLA_Q_RANK + MLA_KV_RANK + LANES:MLA_PAD]
    xq = cq * lax.rsqrt(jnp.mean(cq * cq, axis=-1, keepdims=True) + RMS_EPS) * qg_ref[...]
    xq = xq.astype(BF16)
    xkv = ckv * lax.rsqrt(jnp.mean(ckv * ckv, axis=-1, keepdims=True) + RMS_EPS) * kvg_ref[...]
    xkv = xkv.astype(BF16)
    qa = _dot(xq, wqa_ref[...])
    qb = _dot(xq, wqb_ref[...])
    scale = (MLA_NOPE_DIM + MLA_ROPE_DIM) ** -0.5 * LOG2_E
    kn = _dot(xkv, wk_ref[...])
    kpe_r = kpe * ck_ref[...] + kpe_rot * sk_ref[...]
    cq_t = cq_ref[...]
    sq_t = sq_ref[...]
    for h in range(MLA_HEADS):
        sl = slice(h * MLA_HEAD_PAD, (h + 1) * MLA_HEAD_PAD)
        q_ref[:, sl] = ((qa[:, sl] * cq_t + qb[:, sl] * sq_t) * scale).astype(BF16)
        k_ref[:, sl] = (kn[:, sl] + kpe_r).astype(BF16)
    lane = lax.broadcasted_iota(jnp.int32, (1, MLA_HEADS * MLA_HEAD_PAD), 1) % MLA_HEAD_PAD
    ones_cols = jnp.where(lane >= MLA_V_DIM, 1.0, 0.0).astype(F32)
    v_ref[...] = (_dot(xkv, wv_ref[...]) + ones_cols).astype(BF16)


def _mlaproj_call(p_mla, prm, tables, seq):
    n = p_mla.shape[0]
    qg, kvg, wqa, wqb, wk, wv = prm
    tps = seq // ROW_TILE
    hw = MLA_HEADS * MLA_HEAD_PAD

    def full(shape):
        return pl.BlockSpec(shape, lambda i: (0,) * len(shape))

    tab = pl.BlockSpec((ROW_TILE, LANES), lambda i: (i % tps, 0))
    return pl.pallas_call(
        _mlaproj_body,
        grid=(n // ROW_TILE,),
        in_specs=[pl.BlockSpec((ROW_TILE, MLA_PAD), lambda i: (i, 0)),
                  full(qg.shape), full(kvg.shape), full(wqa.shape), full(wqb.shape),
                  full(wk.shape), full(wv.shape), tab, tab, tab, tab],
        out_specs=[pl.BlockSpec((ROW_TILE, hw), lambda i: (i, 0)),
                   pl.BlockSpec((ROW_TILE, hw), lambda i: (i, 0)),
                   pl.BlockSpec((ROW_TILE, hw), lambda i: (i, 0))],
        out_shape=[jax.ShapeDtypeStruct((n, hw), BF16),
                   jax.ShapeDtypeStruct((n, hw), BF16),
                   jax.ShapeDtypeStruct((n, hw), BF16)],
        compiler_params=_cparams(1),
        name="mla_proj",
    )(p_mla, qg, kvg, wqa, wqb, wk, wv, *tables)


def _mlaattn_body(q_ref, k_ref, v_ref, o_ref):
    ov = []
    for h in range(2):
        sl = slice(h * MLA_HEAD_PAD, (h + 1) * MLA_HEAD_PAD)
        s = _dot(q_ref[:, sl], k_ref[:, sl], _NT)
        m = jnp.max(s, axis=-1, keepdims=True)
        e = jnp.exp2(s - m).astype(BF16)
        ov.append(_dot(e, v_ref[:, sl]))
    a = ov[0]
    a_sw = pltpu.roll(a, MLA_V_DIM, 1)
    b = ov[1]
    b_sw = pltpu.roll(b, MLA_V_DIM, 1)
    lane = lax.broadcasted_iota(jnp.int32, (Q_TILE, 2 * MLA_V_DIM), 1)
    first = lane < MLA_V_DIM
    num = jnp.where(first, a, b_sw)
    den = jnp.where(first, a_sw, b)
    o_ref[...] = (num / den).astype(BF16)


def _mlaattn_call(q, k, v, batch, seq):
    n = q.shape[0]
    qt = seq // Q_TILE
    pairs = MLA_HEADS // 2
    return pl.pallas_call(
        _mlaattn_body,
        grid=(batch, pairs, qt),
        in_specs=[pl.BlockSpec((Q_TILE, 2 * MLA_HEAD_PAD), lambda b, hp, i: (b * qt + i, hp)),
                  pl.BlockSpec((seq, 2 * MLA_HEAD_PAD), lambda b, hp, i: (b, hp)),
                  pl.BlockSpec((seq, 2 * MLA_HEAD_PAD), lambda b, hp, i: (b, hp))],
        out_specs=pl.BlockSpec((Q_TILE, 2 * MLA_V_DIM), lambda b, hp, i: (b * qt + i, hp)),
        out_shape=jax.ShapeDtypeStruct((n, MLA_WIDTH), BF16),
        compiler_params=_cparams(3),
        name="mla_attn",
    )(q, k, v)


def _outproj_body(rw_ref, na_ref, mla_ref, h_ref, w_ref, g_ref, b_ref, o_ref, *, alpha):
    mix = _dot(rw_ref[...], w_ref[0:RW_WIDTH, :])
    mix = mix + _dot(na_ref[...], w_ref[RW_WIDTH:RW_WIDTH + NA_WIDTH, :])
    mix = mix + _dot(mla_ref[...], w_ref[RW_WIDTH + NA_WIDTH:, :])
    o_ref[...] = _layer_norm(alpha * h_ref[...] + mix, g_ref[...], b_ref[...])


def _outproj_call(o_rw, o_na, o_mla, h, w, g, b, alpha):
    n = h.shape[0]
    row = lambda width: pl.BlockSpec((ROW_TILE, width), lambda i: (i, 0))
    return pl.pallas_call(
        functools.partial(_outproj_body, alpha=alpha),
        grid=(n // ROW_TILE,),
        in_specs=[row(RW_WIDTH), row(NA_WIDTH), row(MLA_WIDTH), row(D_MODEL),
                  pl.BlockSpec((D_MODEL, D_MODEL), lambda i: (0, 0)),
                  pl.BlockSpec((1, D_MODEL), lambda i: (0, 0)),
                  pl.BlockSpec((1, D_MODEL), lambda i: (0, 0))],
        out_specs=row(D_MODEL),
        out_shape=jax.ShapeDtypeStruct((n, D_MODEL), F32),
        compiler_params=_cparams(1),
        name="out_proj",
    )(o_rw, o_na, o_mla, h, w, g, b)


def _ffn_body(h_ref, wg_ref, wu_ref, wd_ref, g_ref, b_ref, o_ref, acc_ref, *, alpha, steps):
    j = pl.program_id(1)
    hb = h_ref[...].astype(BF16)
    gt = _dot(hb, wg_ref[...])
    up = _dot(hb, wu_ref[...])
    act = (gt * _sigmoid(gt) * up).astype(BF16)
    part = _dot(act, wd_ref[...])

    @pl.when(j == 0)
    def _():
        acc_ref[...] = part

    @pl.when(j > 0)
    def _():
        acc_ref[...] += part

    @pl.when(j == steps - 1)
    def _():
        o_ref[...] = _layer_norm(alpha * h_ref[...] + acc_ref[...], g_ref[...], b_ref[...])


def _ffn_call(h, wg, wu, wd, g, b, alpha):
    n = h.shape[0]
    steps = D_FF // FFN_TILE
    return pl.pallas_call(
        functools.partial(_ffn_body, alpha=alpha, steps=steps),
        grid=(n // ROW_TILE, steps),
        in_specs=[pl.BlockSpec((ROW_TILE, D_MODEL), lambda i, j: (i, 0)),
                  pl.BlockSpec((D_MODEL, FFN_TILE), lambda i, j: (0, j)),
                  pl.BlockSpec((D_MODEL, FFN_TILE), lambda i, j: (0, j)),
                  pl.BlockSpec((FFN_TILE, D_MODEL), lambda i, j: (j, 0)),
                  pl.BlockSpec((1, D_MODEL), lambda i, j: (0, 0)),
                  pl.BlockSpec((1, D_MODEL), lambda i, j: (0, 0))],
        out_specs=pl.BlockSpec((ROW_TILE, D_MODEL), lambda i, j: (i, 0)),
        out_shape=jax.ShapeDtypeStruct((n, D_MODEL), F32),
        scratch_shapes=[pltpu.VMEM((ROW_TILE, D_MODEL), F32)],
        compiler_params=_cparams(2),
        name="ffn",
    )(h, wg, wu, wd, g, b)


def _rope_rotate_cols(w_pe):
    half = MLA_ROPE_DIM // 2
    return jnp.concatenate([-w_pe[..., half:], w_pe[..., :half]], axis=-1)


def _prep_in_proj(w_in):
    d = w_in.shape[0]
    rw = w_in[:, :RW_IN]
    na = w_in[:, RW_IN:RW_IN + NA_IN]
    mla = w_in[:, RW_IN + NA_IN:]
    cq_ckv = mla[:, :MLA_Q_RANK + MLA_KV_RANK]
    kpe = mla[:, MLA_Q_RANK + MLA_KV_RANK:]
    z = lambda c: jnp.zeros((d, c), w_in.dtype)
    tail = LANES - MLA_NOPE_DIM - MLA_ROPE_DIM
    w = jnp.concatenate([rw, z(RW_PAD - RW_IN), na, cq_ckv,
                         z(MLA_NOPE_DIM), kpe, z(tail),
                         z(MLA_NOPE_DIM), _rope_rotate_cols(kpe), z(tail)], axis=1)
    return w.astype(BF16)


def _prep_lora(w2, row_offset):
    out = jnp.zeros((2, LANES, w2.shape[-1]), w2.dtype)
    rank = w2.shape[1]
    for d in range(2):
        out = out.at[d, row_offset + d * rank:row_offset + (d + 1) * rank].set(w2[d])
    return out.astype(BF16)


def _prep_mla_weights(w_q_b, w_kv_b):
    qd = MLA_NOPE_DIM + MLA_ROPE_DIM
    wq = w_q_b.reshape(MLA_Q_RANK, MLA_HEADS, qd)
    tail = MLA_HEAD_PAD - qd
    zq = lambda c: jnp.zeros((MLA_Q_RANK, MLA_HEADS, c), w_q_b.dtype)
    wqa = jnp.concatenate([wq, zq(tail)], axis=-1).reshape(MLA_Q_RANK, -1)
    wqb = jnp.concatenate([zq(MLA_NOPE_DIM), _rope_rotate_cols(wq[..., MLA_NOPE_DIM:]), zq(tail)],
                          axis=-1).reshape(MLA_Q_RANK, -1)
    wkv = w_kv_b.reshape(MLA_KV_RANK, MLA_HEADS, MLA_NOPE_DIM + MLA_V_DIM)
    zk = jnp.zeros((MLA_KV_RANK, MLA_HEADS, MLA_HEAD_PAD - MLA_NOPE_DIM), w_kv_b.dtype)
    wk = jnp.concatenate([wkv[..., :MLA_NOPE_DIM], zk], axis=-1).reshape(MLA_KV_RANK, -1)
    zv = jnp.zeros((MLA_KV_RANK, MLA_HEADS, MLA_HEAD_PAD - MLA_V_DIM), w_kv_b.dtype)
    wv = jnp.concatenate([wkv[..., MLA_NOPE_DIM:], zv], axis=-1).reshape(MLA_KV_RANK, -1)
    return wqa.astype(BF16), wqb.astype(BF16), wk.astype(BF16), wv.astype(BF16)


def _rope_tables(seq):
    inv_freq = jnp.power(ROPE_THETA, -jnp.arange(0, MLA_ROPE_DIM, 2, dtype=F32) / MLA_ROPE_DIM)
    ang = jnp.arange(seq, dtype=F32)[:, None] * inv_freq[None, :]
    cos2 = jnp.concatenate([jnp.cos(ang), jnp.cos(ang)], axis=-1)
    sin2 = jnp.concatenate([jnp.sin(ang), jnp.sin(ang)], axis=-1)
    tail = LANES - MLA_NOPE_DIM - MLA_ROPE_DIM
    one = jnp.ones((seq, MLA_NOPE_DIM), F32)
    z = lambda c: jnp.zeros((seq, c), F32)
    cq = jnp.concatenate([one, cos2, z(tail)], axis=-1)
    sq = jnp.concatenate([z(MLA_NOPE_DIM), sin2, z(tail)], axis=-1)
    ck = jnp.concatenate([z(MLA_NOPE_DIM), cos2, z(tail)], axis=-1)
    return cq, sq, ck, sq


def _na_bias_table(rpb, rows):
    kr = min(NA_WIN_ROWS, rows)
    heads, n_dr, n_dc = rpb.shape
    edge = GRID_W - 1 - (NA_WIN_COLS - 1)
    ext = jnp.concatenate([jnp.broadcast_to(rpb[..., :1], (heads, n_dr, edge)), rpb,
                           jnp.broadcast_to(rpb[..., -1:], (heads, n_dr, edge))], axis=-1)
    toep = jnp.stack([ext[..., GRID_W - 1 - q:2 * GRID_W - 1 - q] for q in range(GRID_W)],
                     axis=-2)
    slabs = []
    for delta in range(kr):
        lo = NA_WIN_ROWS - 1 - delta
        s = toep[:, lo:lo + kr]
        slabs.append(jnp.transpose(s, (0, 2, 1, 3)).reshape(heads * GRID_W, kr * GRID_W))
    return jnp.stack(slabs, axis=0).astype(F32)


def kernel(x, ln_in_g, ln_in_b, w_in, rw_mu, rw_w0, rw_w2, rw_a0, rw_a2, rw_g2, rw_k_k, rw_k_a,
           rw_r_k, rw_gn_g, rw_gn_b, na_rpb, mla_q_norm_g, mla_w_q_b, mla_kv_norm_g, mla_w_kv_b,
           w_out, ln1_g, ln1_b, w_ffn_gate, w_ffn_up, w_ffn_down, ln2_g, ln2_b):
    batch, seq, d = x.shape
    depth = w_in.shape[0]
    assert d == D_MODEL and seq % ROW_TILE == 0 and seq % CHUNK == 0 and seq % GRID_W == 0
    assert seq // GRID_W >= NA_WIN_ROWS and D_FF % FFN_TILE == 0
    assert (seq // GRID_W) % NA_ROWS_PER_STEP == 0 and MLA_V_DIM * 2 == MLA_HEAD_PAD
    assert batch % RW_BATCH_PER_STEP == 0
    n = batch * seq
    alpha = float((2 * depth) ** 0.25)
    row = lambda a: a.reshape(1, -1)
    tables = _rope_tables(seq)

    h = _ln_call(x.reshape(n, d), row(ln_in_g), row(ln_in_b))
    for l in range(depth):
        p_rw, p_na, p_mla = _inproj_call(h, _prep_in_proj(w_in[l]))

        mu = jnp.pad(rw_mu[l], ((0, 0), (0, RW_PAD - RW_IN)))
        a2 = _prep_lora(rw_a2[l], 2 * RW_DECAY_RANK)
        scan_prm = (mu, rw_w0[l], _prep_lora(rw_w2[l], 0), rw_a0[l], a2,
                    row(rw_k_k[l]), row(rw_k_a[l]))
        y_f, y_b = _rw_call(p_rw.reshape(batch, seq, RW_PAD), scan_prm, batch, seq)
        y_f = y_f.reshape(n, RW_WIDTH)
        y_b = y_b.reshape(n, RW_WIDTH)
        g2 = jnp.pad(rw_g2[l], ((0, LANES - RW_GATE_RANK), (0, 0))).astype(BF16)
        out_prm = (mu, rw_a0[l], a2, row(rw_k_a[l]), row(rw_r_k[l]),
                   row(rw_gn_g[l]), row(rw_gn_b[l]), g2)
        o_rw = _rwout_call(p_rw, y_f, y_b, out_prm, seq)

        o_na = _na_call(p_na, _na_bias_table(na_rpb[l], seq // GRID_W), batch, seq)

        mla_prm = (row(mla_q_norm_g[l]), row(mla_kv_norm_g[l]),
                   *_prep_mla_weights(mla_w_q_b[l], mla_w_kv_b[l]))
        q, k, v = _mlaproj_call(p_mla, mla_prm, tables, seq)
        o_mla = _mlaattn_call(q, k, v, batch, seq)

        h = _outproj_call(o_rw, o_na, o_mla, h, w_out[l].astype(BF16),
                          row(ln1_g[l]), row(ln1_b[l]), alpha)
        h = _ffn_call(h, w_ffn_gate[l].astype(BF16), w_ffn_up[l].astype(BF16),
                      w_ffn_down[l].astype(BF16), row(ln2_g[l]), row(ln2_b[l]), alpha)
    return h.reshape(batch, seq, d)
```

```python
import functools

import jax
import jax.numpy as jnp
import numpy as np
from jax import lax
from jax.experimental import pallas as pl
from jax.experimental.pallas import tpu as pltpu

F32 = jnp.float32
BF16 = jnp.bfloat16

D_MODEL = 1024
HEAD_DIM = 64
GRID_W = 64
RW_HEADS = 4
RW_WIDTH = 256
RW_DECAY_RANK = 32
RW_ICLR_RANK = 32
RW_GATE_RANK = 64
RW_GN_EPS = 64e-5
NA_HEADS = 4
NA_WIDTH = 256
NA_WIN_ROWS = 8
NA_WIN_COLS = 16
MLA_HEADS = 8
MLA_Q_RANK = 256
MLA_KV_RANK = 128
MLA_NOPE_DIM = 64
MLA_ROPE_DIM = 32
MLA_V_DIM = 64
MLA_WIDTH = 512
ROPE_THETA = 10000.0
RW_IN = 960
NA_IN = 768
MLA_IN = 416
D_FF = 2816
LN_EPS = 1e-5
RMS_EPS = 1e-6
NEG_INF = -1e30
LOG2_E = 1.4426950408889634

LANES = 128
RW_PAD = 1024
MLA_PAD = 640
IN_PAD = RW_PAD + NA_IN + MLA_PAD
MLA_HEAD_PAD = 128
CHUNK = 64
RW_BATCH_PER_STEP = 4
ROW_TILE = 512
MM_ROW_TILE = 1024
FFN_TILE = 256
Q_TILE = 256
MLA_HEADS_PER_STEP = 4
NA_ROWS_PER_STEP = 4
VMEM_LIMIT = 56 * 1024 * 1024


def _cparams(n_axes):
    return pltpu.CompilerParams(dimension_semantics=("arbitrary",) * n_axes,
                                vmem_limit_bytes=VMEM_LIMIT)


def _split_bf16(x, parts):
    out = []
    rem = x
    for i in range(parts):
        p = rem.astype(BF16)
        out.append(p)
        if i + 1 < parts:
            rem = rem - p.astype(F32)
    return out


_NN = (((1,), (0,)), ((), ()))
_NT = (((1,), (1,)), ((), ()))


def _dot(a, b, dims=_NN):
    return lax.dot_general(a, b, dims, preferred_element_type=F32)


def _dot1(a, b, dims=_NN):
    return _dot(a.astype(BF16), b.astype(BF16), dims)


def _dot_exact_rhs(a, b_bf16, dims=_NN):
    a0, a1, a2 = _split_bf16(a, 3)
    return _dot(a0, b_bf16, dims) + (_dot(a1, b_bf16, dims) + _dot(a2, b_bf16, dims))


def _sigmoid(x):
    return 1.0 / (1.0 + jnp.exp(-x))


def _softplus(x):
    return jnp.maximum(x, 0.0) + jnp.log(1.0 + jnp.exp(-jnp.abs(x)))


def _layer_norm(x, g, b):
    mu = jnp.mean(x, axis=-1, keepdims=True)
    xc = x - mu
    var = jnp.mean(xc * xc, axis=-1, keepdims=True)
    return xc * lax.rsqrt(var + LN_EPS) * g + b


def _head_sum_matrix(width, head):
    r = lax.broadcasted_iota(jnp.int32, (width, width), 0) // head
    c = lax.broadcasted_iota(jnp.int32, (width, width), 1) // head
    return jnp.where(r == c, 1.0, 0.0).astype(BF16)


def _token_shift(x, prev_row, next_row, mu0, mu1):
    rows = x.shape[0]
    ridx = lax.broadcasted_iota(jnp.int32, x.shape, 0)
    xp = jnp.where(ridx == 0, prev_row, pltpu.roll(x, 1, 0))
    xn = jnp.where(ridx == rows - 1, next_row, pltpu.roll(x, rows - 1, 0))
    return x + mu0 * (xp - x) + mu1 * (xn - x)


def _ln_body(x_ref, g_ref, b_ref, o_ref):
    o_ref[...] = _layer_norm(x_ref[...], g_ref[...], b_ref[...])


def _ln_call(x, g, b):
    n = x.shape[0]
    return pl.pallas_call(
        _ln_body,
        grid=(n // ROW_TILE,),
        in_specs=[pl.BlockSpec((ROW_TILE, D_MODEL), lambda i: (i, 0)),
                  pl.BlockSpec((1, D_MODEL), lambda i: (0, 0)),
                  pl.BlockSpec((1, D_MODEL), lambda i: (0, 0))],
        out_specs=pl.BlockSpec((ROW_TILE, D_MODEL), lambda i: (i, 0)),
        out_shape=jax.ShapeDtypeStruct((n, D_MODEL), F32),
        compiler_params=_cparams(1),
        name="ln_in",
    )(x, g, b)


def _inproj_body(h_ref, w_ref, rw_ref, na_ref, mla_ref):
    hb = h_ref[...].astype(BF16)
    rw_ref[...] = _dot(hb, w_ref[:, 0:RW_PAD])
    na_ref[...] = _dot(hb, w_ref[:, RW_PAD:RW_PAD + NA_IN]).astype(BF16)
    mla_ref[...] = _dot(hb, w_ref[:, RW_PAD + NA_IN:IN_PAD])


def _inproj_call(h, w):
    n = h.shape[0]
    return pl.pallas_call(
        _inproj_body,
        grid=(n // MM_ROW_TILE,),
        in_specs=[pl.BlockSpec((MM_ROW_TILE, D_MODEL), lambda i: (i, 0)),
                  pl.BlockSpec((D_MODEL, IN_PAD), lambda i: (0, 0))],
        out_specs=[pl.BlockSpec((MM_ROW_TILE, RW_PAD), lambda i: (i, 0)),
                   pl.BlockSpec((MM_ROW_TILE, NA_IN), lambda i: (i, 0)),
                   pl.BlockSpec((MM_ROW_TILE, MLA_PAD), lambda i: (i, 0))],
        out_shape=[jax.ShapeDtypeStruct((n, RW_PAD), F32),
                   jax.ShapeDtypeStruct((n, NA_IN), BF16),
                   jax.ShapeDtypeStruct((n, MLA_PAD), F32)],
        compiler_params=_cparams(1),
        name="in_proj",
    )(h, w)


def _stack_heads(x, head_masks):
    xb = x.astype(BF16)
    return jnp.concatenate([jnp.where(m, xb, jnp.zeros_like(xb)) for m in head_masks], axis=0)


def _unstack_heads(x):
    c = x.shape[0] // 4
    return (x[0:c] + x[c:2 * c]) + (x[2 * c:3 * c] + x[3 * c:4 * c])


def _neumann_inverse(l):
    n = l.shape[0]
    eye = jnp.where(lax.broadcasted_iota(jnp.int32, (n, n), 0) ==
                    lax.broadcasted_iota(jnp.int32, (n, n), 1), 1.0, 0.0).astype(F32)
    t = eye + l
    p = l
    steps = int(np.log2(CHUNK)) - 1
    for _ in range(steps):
        pb = p.astype(BF16)
        p = _dot(pb, pb)
        t = t + _dot1(t, p)
        yield None
    return t


def _rw_direction(x, prev_row, next_row, d, prm, h_ref):
    mu_ref, w0_ref, w2_ref, a0_ref, a2_ref, kk_ref, ka_ref = prm
    c = CHUNK
    w = RW_WIDTH
    xs = _token_shift(x, prev_row, next_row, mu_ref[0:1, :], mu_ref[1:2, :])
    r = xs[:, 0:w]
    k = xs[:, w:2 * w]
    v = xs[:, 2 * w:3 * w]
    lora = xs[:, 3 * w:3 * w + LANES]
    lw = _dot(jnp.tanh(lora).astype(BF16), w2_ref[d])
    la = _dot(lora.astype(BF16), a2_ref[d])
    log_w = -_softplus(-(w0_ref[d:d + 1, :] + lw)) - 0.5
    ldec = -jnp.exp(log_w)
    iclr = _sigmoid(a0_ref[d:d + 1, :] + la)
    kk = k * kk_ref[...]
    ss = _dot_exact_rhs(kk * kk, _head_sum_matrix(w, HEAD_DIM))
    kk = kk / jnp.maximum(jnp.sqrt(ss), 1e-12)
    k_dir = k * (1.0 + (iclr - 1.0) * ka_ref[...])
    a = -kk
    b = kk * iclr

    ti = lax.broadcasted_iota(jnp.int32, (c, c), 0)
    si = lax.broadcasted_iota(jnp.int32, (c, c), 1)
    cum = jnp.where(si <= ti, 1.0, 0.0).astype(BF16)
    l0, l1, l2 = _split_bf16(ldec, 3)
    cl_incl = _dot(cum, l0) + (_dot(cum, l1) + _dot(cum, l2))
    cl_excl = cl_incl - ldec
    tot = cl_incl[c - 1:c, :]
    if d == 0:
        e_a = jnp.exp(cl_excl)
        e_d = jnp.exp(-cl_incl)
        e_r = jnp.exp(cl_incl)
        e_h = jnp.exp(tot - cl_incl)
    else:
        e_a = jnp.exp(tot - cl_incl)
        e_d = jnp.exp(cl_excl - tot)
        e_r = e_a
        e_h = jnp.exp(cl_excl)
    at = a * e_a
    bt = b * e_d
    kt = k_dir * e_d
    rt = r * e_r
    bh = b * e_h
    kh = k_dir * e_h

    lane_head = lax.broadcasted_iota(jnp.int32, (1, w), 1) // HEAD_DIM
    head_masks = [lane_head == h for h in range(RW_HEADS)]
    xa = _stack_heads(at, head_masks)
    xr = _stack_heads(rt, head_masks)
    xb = _stack_heads(bt, head_masks)
    bk = jnp.concatenate([bt, kt], axis=0).astype(BF16)
    v_b = v.astype(BF16)

    n = RW_HEADS * c
    ri = lax.broadcasted_iota(jnp.int32, (n, n), 0)
    ci = lax.broadcasted_iota(jnp.int32, (n, n), 1)
    same = (ri // c) == (ci // c)
    tc = lax.broadcasted_iota(jnp.int32, (n, 2 * c), 0) % c
    sc = lax.broadcasted_iota(jnp.int32, (n, 2 * c), 1)
    second = sc >= c
    sc = sc % c
    if d == 0:
        strict = same & ((ci % c) < (ri % c))
        strict_c = sc < tc
        incl_c = sc <= tc
    else:
        strict = same & ((ci % c) > (ri % c))
        strict_c = sc > tc
        incl_c = strict_c
    yield None
    lab = jnp.where(strict, _dot(xa, xb, _NT), 0.0)
    pa = _dot(xa, bk, _NT)
    pr = _dot(xr, bk, _NT)
    lak_c = jnp.where(strict_c & second, pa, 0.0).astype(BF16)
    mr_c = jnp.where(incl_c, pr, 0.0).astype(BF16)
    yield None

    t = (yield from _neumann_inverse(lab)).astype(BF16)
    own = (lax.broadcasted_iota(jnp.int32, (n, w), 0) // c ==
           lax.broadcasted_iota(jnp.int32, (n, w), 1) // HEAD_DIM)
    vv = jnp.concatenate([v_b, v_b], axis=0)
    q = jnp.where(own, _dot(lak_c, vv), 0.0)
    ta = _dot(t, xa)
    yield None
    u0 = _dot(t, q.astype(BF16))
    ta_c = _unstack_heads(ta).astype(BF16)
    u0_c = _unstack_heads(u0)
    bk_t = jnp.concatenate([bh, kh], axis=0).T.astype(BF16)
    w_col = jnp.exp(jnp.sum(ldec.T, axis=1, keepdims=True))
    yield None

    h = h_ref[...]
    h_hi, h_lo = _split_bf16(h, 2)
    u_c = (_dot(ta_c, h_hi) + _dot(ta_c, h_lo)) + u0_c
    uv = jnp.concatenate([u_c.astype(BF16), v_b], axis=0)
    rt_b = rt.astype(BF16)
    y = (_dot(rt_b, h_hi) + _dot(rt_b, h_lo)) + _unstack_heads(jnp.where(own, _dot(mr_c, uv), 0.0))
    lane_k = lax.broadcasted_iota(jnp.int32, (w, w), 1)
    row_k = lax.broadcasted_iota(jnp.int32, (w, w), 0)
    bdm = (lane_k // HEAD_DIM) == (row_k // HEAD_DIM)
    h_ref[...] = w_col * h + jnp.where(bdm, _dot(bk_t, uv), 0.0)
    return y


def _run_interleaved(gens):
    results = [None] * len(gens)
    live = list(range(len(gens)))
    while live:
        for i in list(live):
            try:
                next(gens[i])
            except StopIteration as stop:
                results[i] = stop.value
                live.remove(i)
    return results


def _rw_body(xf_ref, pf_ref, nf_ref, xb_ref, pb_ref, nb_ref,
             mu_ref, w0_ref, w2_ref, a0_ref, a2_ref, kk_ref, ka_ref,
             yf_ref, yb_ref, hf_ref, hb_ref, *, n_chunks):
    ch = pl.program_id(1)

    @pl.when(ch == 0)
    def _():
        hf_ref[...] = jnp.zeros_like(hf_ref)
        hb_ref[...] = jnp.zeros_like(hb_ref)

    prm = (mu_ref, w0_ref, w2_ref, a0_ref, a2_ref, kk_ref, ka_ref)
    zero = jnp.zeros((1, RW_PAD), F32)
    gens = []
    for bb in range(RW_BATCH_PER_STEP):
        prev_f = jnp.where(ch == 0, zero, pf_ref[bb, 7:8, :])
        next_f = jnp.where(ch == n_chunks - 1, zero, nf_ref[bb, 0:1, :])
        prev_b = jnp.where(ch == n_chunks - 1, zero, pb_ref[bb, 7:8, :])
        next_b = jnp.where(ch == 0, zero, nb_ref[bb, 0:1, :])
        gens.append(_rw_direction(xf_ref[bb], prev_f, next_f, 0, prm, hf_ref.at[bb]))
        gens.append(_rw_direction(xb_ref[bb], prev_b, next_b, 1, prm, hb_ref.at[bb]))
    ys = _run_interleaved(gens)
    for bb in range(RW_BATCH_PER_STEP):
        yf_ref[bb] = ys[2 * bb]
        yb_ref[bb] = ys[2 * bb + 1]


def _rw_call(p_rw, prm, batch, seq):
    nc = seq // CHUNK
    c8 = CHUNK // 8
    last8 = seq // 8 - 1
    bps = RW_BATCH_PER_STEP

    def main_f(b, c):
        return (b, c, 0)

    def prev_f(b, c):
        return (b, jnp.maximum(c * c8 - 1, 0), 0)

    def next_f(b, c):
        return (b, jnp.minimum((c + 1) * c8, last8), 0)

    def main_b(b, c):
        return (b, nc - 1 - c, 0)

    def prev_b(b, c):
        return (b, jnp.maximum((nc - 1 - c) * c8 - 1, 0), 0)

    def next_b(b, c):
        return (b, jnp.minimum((nc - c) * c8, last8), 0)

    def full(shape):
        return pl.BlockSpec(shape, lambda b, c: (0,) * len(shape))

    mu, w0, w2, a0, a2, kk, ka = prm
    return pl.pallas_call(
        functools.partial(_rw_body, n_chunks=nc),
        grid=(batch // bps, nc),
        in_specs=[pl.BlockSpec((bps, CHUNK, RW_PAD), main_f),
                  pl.BlockSpec((bps, 8, RW_PAD), prev_f),
                  pl.BlockSpec((bps, 8, RW_PAD), next_f),
                  pl.BlockSpec((bps, CHUNK, RW_PAD), main_b),
                  pl.BlockSpec((bps, 8, RW_PAD), prev_b),
                  pl.BlockSpec((bps, 8, RW_PAD), next_b),
                  full(mu.shape), full(w0.shape), full(w2.shape), full(a0.shape),
                  full(a2.shape), full(kk.shape), full(ka.shape)],
        out_specs=[pl.BlockSpec((bps, CHUNK, RW_WIDTH), main_f),
                   pl.BlockSpec((bps, CHUNK, RW_WIDTH), main_b)],
        out_shape=[jax.ShapeDtypeStruct((batch, seq, RW_WIDTH), F32),
                   jax.ShapeDtypeStruct((batch, seq, RW_WIDTH), F32)],
        scratch_shapes=[pltpu.VMEM((bps, RW_WIDTH, RW_WIDTH), F32),
                        pltpu.VMEM((bps, RW_WIDTH, RW_WIDTH), F32)],
        compiler_params=_cparams(2),
        name="rwkv_scan",
    )(p_rw, p_rw, p_rw, p_rw, p_rw, p_rw, mu, w0, w2, a0, a2, kk, ka)


def _rwout_body(x_ref, p_ref, n_ref, yf_ref, yb_ref, mu_ref, a0_ref, a2_ref, ka_ref, rk_ref,
                gg_ref, gb_ref, g2_ref, o_ref, *, tiles_per_seq):
    i = pl.program_id(0)
    w = RW_WIDTH
    zero = jnp.zeros((1, RW_PAD), F32)
    prev_row = jnp.where(i % tiles_per_seq == 0, zero, p_ref[7:8, :])
    next_row = jnp.where(i % tiles_per_seq == tiles_per_seq - 1, zero, n_ref[0:1, :])
    xs = _token_shift(x_ref[...], prev_row, next_row, mu_ref[0:1, :], mu_ref[1:2, :])
    r = xs[:, 0:w]
    k = xs[:, w:2 * w]
    v = xs[:, 2 * w:3 * w]
    lora = xs[:, 3 * w:3 * w + LANES]
    gblk = xs[:, 3 * w + LANES:3 * w + 2 * LANES]
    iclr0 = _sigmoid(a0_ref[0:1, :] + _dot(lora.astype(BF16), a2_ref[0]))
    k_dir0 = k * (1.0 + (iclr0 - 1.0) * ka_ref[...])
    gate = _dot(_sigmoid(gblk).astype(BF16), g2_ref[...])
    hsum = _head_sum_matrix(w, HEAD_DIM)
    y = yf_ref[...] + yb_ref[...]
    mean = _dot_exact_rhs(y, hsum) * (1.0 / HEAD_DIM)
    yc = y - mean
    var = _dot_exact_rhs(yc * yc, hsum) * (1.0 / HEAD_DIM)
    yn = yc * lax.rsqrt(var + RW_GN_EPS) * gg_ref[...] + gb_ref[...]
    bonus = _dot_exact_rhs(r * k_dir0 * rk_ref[...], hsum) * v
    o_ref[...] = ((yn + bonus) * gate).astype(BF16)


def _rwout_call(p_rw, yf, yb, prm, seq):
    n = p_rw.shape[0]
    t8 = ROW_TILE // 8
    last8 = n // 8 - 1
    mu, a0, a2, ka, rk, gg, gb, g2 = prm

    def full(shape):
        return pl.BlockSpec(shape, lambda i: (0,) * len(shape))

    return pl.pallas_call(
        functools.partial(_rwout_body, tiles_per_seq=seq // ROW_TILE),
        grid=(n // ROW_TILE,),
        in_specs=[pl.BlockSpec((ROW_TILE, RW_PAD), lambda i: (i, 0)),
                  pl.BlockSpec((8, RW_PAD), lambda i: (jnp.maximum(i * t8 - 1, 0), 0)),
                  pl.BlockSpec((8, RW_PAD), lambda i: (jnp.minimum((i + 1) * t8, last8), 0)),
                  pl.BlockSpec((ROW_TILE, RW_WIDTH), lambda i: (i, 0)),
                  pl.BlockSpec((ROW_TILE, RW_WIDTH), lambda i: (i, 0)),
                  full(mu.shape), full(a0.shape), full(a2.shape), full(ka.shape), full(rk.shape),
                  full(gg.shape), full(gb.shape), full(g2.shape)],
        out_specs=pl.BlockSpec((ROW_TILE, RW_WIDTH), lambda i: (i, 0)),
        out_shape=jax.ShapeDtypeStruct((n, RW_WIDTH), BF16),
        compiler_params=_cparams(1),
        name="rwkv_out",
    )(p_rw, p_rw, p_rw, yf, yb, mu, a0, a2, ka, rk, gg, gb, g2)


def _na_body(q_ref, k_ref, v_ref, bias_ref, o_ref, *, rows):
    g = pl.program_id(1)
    w = NA_WIDTH
    kr = NA_WIN_ROWS
    nk = kr * GRID_W
    nq = NA_HEADS * GRID_W
    qc = lax.broadcasted_iota(jnp.int32, (nq, nk), 0) % GRID_W
    kc = lax.broadcasted_iota(jnp.int32, (nq, nk), 1) % GRID_W
    cs = jnp.clip(qc - NA_WIN_COLS // 2, 0, GRID_W - NA_WIN_COLS)
    col_ok = (kc >= cs) & (kc < cs + NA_WIN_COLS)
    lane_head = lax.broadcasted_iota(jnp.int32, (1, w), 1) // HEAD_DIM
    head_masks = [lane_head == h for h in range(NA_HEADS)]
    own_head = (lax.broadcasted_iota(jnp.int32, (nq, w), 0) // GRID_W ==
                lax.broadcasted_iota(jnp.int32, (nq, w), 1) // HEAD_DIM)
    for j in range(NA_ROWS_PER_STEP):
        i = g * NA_ROWS_PER_STEP + j
        row_start = jnp.clip(i - kr // 2, 0, rows - kr)
        delta = i - row_start
        start = pl.multiple_of(row_start * GRID_W, GRID_W)
        q = q_ref[j * GRID_W:(j + 1) * GRID_W, :] * (HEAD_DIM ** -0.5)
        q_st = jnp.concatenate([jnp.where(m, q, jnp.zeros_like(q)) for m in head_masks], axis=0)
        kwin = k_ref[pl.ds(start, nk), :]
        vwin = v_ref[pl.ds(start, nk), :]
        s = jnp.where(col_ok, _dot(q_st, kwin, _NT) + bias_ref[delta], NEG_INF)
        m = jnp.max(s, axis=-1, keepdims=True)
        e = jnp.exp(s - m)
        prob = e / jnp.sum(e, axis=-1, keepdims=True)
        o_st = jnp.where(own_head, _dot(prob.astype(BF16), vwin), 0.0)
        o_ref[j * GRID_W:(j + 1) * GRID_W, :] = _unstack_heads(o_st).astype(BF16)


def _na_call(p_na, bias, batch, seq):
    n = p_na.shape[0]
    rows = seq // GRID_W
    steps = rows // NA_ROWS_PER_STEP
    blk = NA_ROWS_PER_STEP * GRID_W
    return pl.pallas_call(
        functools.partial(_na_body, rows=rows),
        grid=(batch, steps),
        in_specs=[pl.BlockSpec((blk, NA_WIDTH), lambda b, i: (b * steps + i, 0)),
                  pl.BlockSpec((seq, NA_WIDTH), lambda b, i: (b, 1)),
                  pl.BlockSpec((seq, NA_WIDTH), lambda b, i: (b, 2)),
                  pl.BlockSpec(bias.shape, lambda b, i: (0, 0, 0))],
        out_specs=pl.BlockSpec((blk, NA_WIDTH), lambda b, i: (b * steps + i, 0)),
        out_shape=jax.ShapeDtypeStruct((n, NA_WIDTH), BF16),
        compiler_params=_cparams(2),
        name="na_attn",
    )(p_na, p_na, p_na, bias)


def _mlaproj_body(p_ref, qg_ref, kvg_ref, wqa_ref, wqb_ref, wk_ref, wv_ref,
                  cq_ref, sq_ref, ck_ref, sk_ref, q_ref, k_ref, v_ref):
    p = p_ref[...]
    cq = p[:, 0:MLA_Q_RANK]
    ckv = p[:, MLA_Q_RANK:MLA_Q_RANK + MLA_KV_RANK]
    kpe = p[:, MLA_Q_RANK + MLA_KV_RANK:MLA_Q_RANK + MLA_KV_RANK + LANES]
    kpe_rot = p[:, MLA_Q_RANK + MLA_KV_RANK + LANES:MLA_PAD]
    xq = cq * lax.rsqrt(jnp.mean(cq * cq, axis=-1, keepdims=True) + RMS_EPS) * qg_ref[...]
    xq = xq.astype(BF16)
    xkv = ckv * lax.rsqrt(jnp.mean(ckv * ckv, axis=-1, keepdims=True) + RMS_EPS) * kvg_ref[...]
    xkv = xkv.astype(BF16)
    qa = _dot(xq, wqa_ref[...])
    qb = _dot(xq, wqb_ref[...])
    scale = (MLA_NOPE_DIM + MLA_ROPE_DIM) ** -0.5 * LOG2_E
    kn = _dot(xkv, wk_ref[...])
    kpe_r = kpe * ck_ref[...] + kpe_rot * sk_ref[...]
    cq_t = cq_ref[...]
    sq_t = sq_ref[...]
    for h in range(MLA_HEADS):
        sl = slice(h * MLA_HEAD_PAD, (h + 1) * MLA_HEAD_PAD)
        q_ref[:, sl] = ((qa[:, sl] * cq_t + qb[:, sl] * sq_t) * scale).astype(BF16)
        k_ref[:, sl] = (kn[:, sl] + kpe_r).astype(BF16)
    lane = lax.broadcasted_iota(jnp.int32, (1, MLA_HEADS * MLA_HEAD_PAD), 1) % MLA_HEAD_PAD
    ones_cols = jnp.where(lane >= MLA_V_DIM, 1.0, 0.0).astype(F32)
    v_ref[...] = (_dot(xkv, wv_ref[...]) + ones_cols).astype(BF16)


def _mlaproj_call(p_mla, prm, tables, seq):
    n = p_mla.shape[0]
    qg, kvg, wqa, wqb, wk, wv = prm
    tps = seq // ROW_TILE
    hw = MLA_HEADS * MLA_HEAD_PAD

    def full(shape):
        return pl.BlockSpec(shape, lambda i: (0,) * len(shape))

    tab = pl.BlockSpec((ROW_TILE, LANES), lambda i: (i % tps, 0))
    return pl.pallas_call(
        _mlaproj_body,
        grid=(n // ROW_TILE,),
        in_specs=[pl.BlockSpec((ROW_TILE, MLA_PAD), lambda i: (i, 0)),
                  full(qg.shape), full(kvg.shape), full(wqa.shape), full(wqb.shape),
                  full(wk.shape), full(wv.shape), tab, tab, tab, tab],
        out_specs=[pl.BlockSpec((ROW_TILE, hw), lambda i: (i, 0)),
                   pl.BlockSpec((ROW_TILE, hw), lambda i: (i, 0)),
                   pl.BlockSpec((ROW_TILE, hw), lambda i: (i, 0))],
        out_shape=[jax.ShapeDtypeStruct((n, hw), BF16),
                   jax.ShapeDtypeStruct((n, hw), BF16),
                   jax.ShapeDtypeStruct((n, hw), BF16)],
        compiler_params=_cparams(1),
        name="mla_proj",
    )(p_mla, qg, kvg, wqa, wqb, wk, wv, *tables)


def _mlaattn_body(q_ref, k_ref, v_ref, o_ref):
    lane = lax.broadcasted_iota(jnp.int32, (Q_TILE, 2 * MLA_V_DIM), 1)
    first = lane < MLA_V_DIM
    for pair in range(MLA_HEADS_PER_STEP // 2):
        ov = []
        for h in (2 * pair, 2 * pair + 1):
            sl = slice(h * MLA_HEAD_PAD, (h + 1) * MLA_HEAD_PAD)
            s = _dot(q_ref[:, sl], k_ref[:, sl], _NT)
            m = jnp.max(s, axis=-1, keepdims=True)
            e = jnp.exp2(s - m).astype(BF16)
            ov.append(_dot(e, v_ref[:, sl]))
        a, b = ov
        num = jnp.where(first, a, pltpu.roll(b, MLA_V_DIM, 1))
        den = jnp.where(first, pltpu.roll(a, MLA_V_DIM, 1), b)
        o_ref[:, pair * 2 * MLA_V_DIM:(pair + 1) * 2 * MLA_V_DIM] = (num / den).astype(BF16)


def _mlaattn_call(q, k, v, batch, seq):
    n = q.shape[0]
    qt = seq // Q_TILE
    groups = MLA_HEADS // MLA_HEADS_PER_STEP
    wide = MLA_HEADS_PER_STEP * MLA_HEAD_PAD
    return pl.pallas_call(
        _mlaattn_body,
        grid=(batch, groups, qt),
        in_specs=[pl.BlockSpec((Q_TILE, wide), lambda b, hg, i: (b * qt + i, hg)),
                  pl.BlockSpec((seq, wide), lambda b, hg, i: (b, hg)),
                  pl.BlockSpec((seq, wide), lambda b, hg, i: (b, hg))],
        out_specs=pl.BlockSpec((Q_TILE, MLA_HEADS_PER_STEP * MLA_V_DIM),
                               lambda b, hg, i: (b * qt + i, hg)),
        out_shape=jax.ShapeDtypeStruct((n, MLA_WIDTH), BF16),
        compiler_params=_cparams(3),
        name="mla_attn",
    )(q, k, v)


def _outproj_body(rw_ref, na_ref, mla_ref, h_ref, w_ref, g_ref, b_ref, o_ref, *, alpha):
    mix = _dot(rw_ref[...], w_ref[0:RW_WIDTH, :])
    mix = mix + _dot(na_ref[...], w_ref[RW_WIDTH:RW_WIDTH + NA_WIDTH, :])
    mix = mix + _dot(mla_ref[...], w_ref[RW_WIDTH + NA_WIDTH:, :])
    o_ref[...] = _layer_norm(alpha * h_ref[...] + mix, g_ref[...], b_ref[...])


def _outproj_call(o_rw, o_na, o_mla, h, w, g, b, alpha):
    n = h.shape[0]
    row = lambda width: pl.BlockSpec((MM_ROW_TILE, width), lambda i: (i, 0))
    return pl.pallas_call(
        functools.partial(_outproj_body, alpha=alpha),
        grid=(n // MM_ROW_TILE,),
        in_specs=[row(RW_WIDTH), row(NA_WIDTH), row(MLA_WIDTH), row(D_MODEL),
                  pl.BlockSpec((D_MODEL, D_MODEL), lambda i: (0, 0)),
                  pl.BlockSpec((1, D_MODEL), lambda i: (0, 0)),
                  pl.BlockSpec((1, D_MODEL), lambda i: (0, 0))],
        out_specs=row(D_MODEL),
        out_shape=jax.ShapeDtypeStruct((n, D_MODEL), F32),
        compiler_params=_cparams(1),
        name="out_proj",
    )(o_rw, o_na, o_mla, h, w, g, b)


def _ffn_body(h_ref, wg_ref, wu_ref, wd_ref, g_ref, b_ref, o_ref, *, alpha):
    h = h_ref[...]
    hb = h.astype(BF16)
    acc = alpha * h
    for j in range(D_FF // FFN_TILE):
        cols = slice(j * FFN_TILE, (j + 1) * FFN_TILE)
        gt = _dot(hb, wg_ref[:, cols])
        up = _dot(hb, wu_ref[:, cols])
        act = (gt * _sigmoid(gt) * up).astype(BF16)
        acc = acc + _dot(act, wd_ref[cols, :])
    o_ref[...] = _layer_norm(acc, g_ref[...], b_ref[...])


def _ffn_call(h, wg, wu, wd, g, b, alpha):
    n = h.shape[0]
    resident = lambda shape: pl.BlockSpec(shape, lambda i: (0, 0), pipeline_mode=pl.Buffered(1))
    return pl.pallas_call(
        functools.partial(_ffn_body, alpha=alpha),
        grid=(n // MM_ROW_TILE,),
        in_specs=[pl.BlockSpec((MM_ROW_TILE, D_MODEL), lambda i: (i, 0)),
                  resident((D_MODEL, D_FF)), resident((D_MODEL, D_FF)), resident((D_FF, D_MODEL)),
                  pl.BlockSpec((1, D_MODEL), lambda i: (0, 0)),
                  pl.BlockSpec((1, D_MODEL), lambda i: (0, 0))],
        out_specs=pl.BlockSpec((MM_ROW_TILE, D_MODEL), lambda i: (i, 0)),
        out_shape=jax.ShapeDtypeStruct((n, D_MODEL), F32),
        compiler_params=_cparams(1),
        name="ffn",
    )(h, wg, wu, wd, g, b)


def _rope_rotate_cols(w_pe):
    half = MLA_ROPE_DIM // 2
    return jnp.concatenate([-w_pe[..., half:], w_pe[..., :half]], axis=-1)


def _prep_in_proj(w_in):
    d = w_in.shape[0]
    rw = w_in[:, :RW_IN]
    na = w_in[:, RW_IN:RW_IN + NA_IN]
    mla = w_in[:, RW_IN + NA_IN:]
    cq_ckv = mla[:, :MLA_Q_RANK + MLA_KV_RANK]
    kpe = mla[:, MLA_Q_RANK + MLA_KV_RANK:]
    z = lambda c: jnp.zeros((d, c), w_in.dtype)
    tail = LANES - MLA_NOPE_DIM - MLA_ROPE_DIM
    w = jnp.concatenate([rw, z(RW_PAD - RW_IN), na, cq_ckv,
                         z(MLA_NOPE_DIM), kpe, z(tail),
                         z(MLA_NOPE_DIM), _rope_rotate_cols(kpe), z(tail)], axis=1)
    return w.astype(BF16)


def _prep_lora(w2, row_offset):
    out = jnp.zeros((2, LANES, w2.shape[-1]), w2.dtype)
    rank = w2.shape[1]
    for d in range(2):
        out = out.at[d, row_offset + d * rank:row_offset + (d + 1) * rank].set(w2[d])
    return out.astype(BF16)


def _prep_mla_weights(w_q_b, w_kv_b):
    qd = MLA_NOPE_DIM + MLA_ROPE_DIM
    wq = w_q_b.reshape(MLA_Q_RANK, MLA_HEADS, qd)
    tail = MLA_HEAD_PAD - qd
    zq = lambda c: jnp.zeros((MLA_Q_RANK, MLA_HEADS, c), w_q_b.dtype)
    wqa = jnp.concatenate([wq, zq(tail)], axis=-1).reshape(MLA_Q_RANK, -1)
    wqb = jnp.concatenate([zq(MLA_NOPE_DIM), _rope_rotate_cols(wq[..., MLA_NOPE_DIM:]), zq(tail)],
                          axis=-1).reshape(MLA_Q_RANK, -1)
    wkv = w_kv_b.reshape(MLA_KV_RANK, MLA_HEADS, MLA_NOPE_DIM + MLA_V_DIM)
    zk = jnp.zeros((MLA_KV_RANK, MLA_HEADS, MLA_HEAD_PAD - MLA_NOPE_DIM), w_kv_b.dtype)
    wk = jnp.concatenate([wkv[..., :MLA_NOPE_DIM], zk], axis=-1).reshape(MLA_KV_RANK, -1)
    zv = jnp.zeros((MLA_KV_RANK, MLA_HEADS, MLA_HEAD_PAD - MLA_V_DIM), w_kv_b.dtype)
    wv = jnp.concatenate([wkv[..., MLA_NOPE_DIM:], zv], axis=-1).reshape(MLA_KV_RANK, -1)
    return wqa.astype(BF16), wqb.astype(BF16), wk.astype(BF16), wv.astype(BF16)


def _rope_tables(seq):
    inv_freq = jnp.power(ROPE_THETA, -jnp.arange(0, MLA_ROPE_DIM, 2, dtype=F32) / MLA_ROPE_DIM)
    ang = jnp.arange(seq, dtype=F32)[:, None] * inv_freq[None, :]
    cos2 = jnp.concatenate([jnp.cos(ang), jnp.cos(ang)], axis=-1)
    sin2 = jnp.concatenate([jnp.sin(ang), jnp.sin(ang)], axis=-1)
    tail = LANES - MLA_NOPE_DIM - MLA_ROPE_DIM
    one = jnp.ones((seq, MLA_NOPE_DIM), F32)
    z = lambda c: jnp.zeros((seq, c), F32)
    cq = jnp.concatenate([one, cos2, z(tail)], axis=-1)
    sq = jnp.concatenate([z(MLA_NOPE_DIM), sin2, z(tail)], axis=-1)
    ck = jnp.concatenate([z(MLA_NOPE_DIM), cos2, z(tail)], axis=-1)
    return cq, sq, ck, sq


def _na_bias_table(rpb, rows):
    kr = min(NA_WIN_ROWS, rows)
    heads, n_dr, n_dc = rpb.shape
    edge = GRID_W - 1 - (NA_WIN_COLS - 1)
    ext = jnp.concatenate([jnp.broadcast_to(rpb[..., :1], (heads, n_dr, edge)), rpb,
                           jnp.broadcast_to(rpb[..., -1:], (heads, n_dr, edge))], axis=-1)
    toep = jnp.stack([ext[..., GRID_W - 1 - q:2 * GRID_W - 1 - q] for q in range(GRID_W)],
                     axis=-2)
    slabs = []
    for delta in range(kr):
        lo = NA_WIN_ROWS - 1 - delta
        s = toep[:, lo:lo + kr]
        slabs.append(jnp.transpose(s, (0, 2, 1, 3)).reshape(heads * GRID_W, kr * GRID_W))
    return jnp.stack(slabs, axis=0).astype(F32)


def kernel(x, ln_in_g, ln_in_b, w_in, rw_mu, rw_w0, rw_w2, rw_a0, rw_a2, rw_g2, rw_k_k, rw_k_a,
           rw_r_k, rw_gn_g, rw_gn_b, na_rpb, mla_q_norm_g, mla_w_q_b, mla_kv_norm_g, mla_w_kv_b,
           w_out, ln1_g, ln1_b, w_ffn_gate, w_ffn_up, w_ffn_down, ln2_g, ln2_b):
    batch, seq, d = x.shape
    depth = w_in.shape[0]
    assert d == D_MODEL and seq % ROW_TILE == 0 and seq % CHUNK == 0 and seq % GRID_W == 0
    assert seq // GRID_W >= NA_WIN_ROWS and D_FF % FFN_TILE == 0 and seq % Q_TILE == 0
    assert (batch * seq) % MM_ROW_TILE == 0
    assert (seq // GRID_W) % NA_ROWS_PER_STEP == 0 and MLA_V_DIM * 2 == MLA_HEAD_PAD
    assert batch % RW_BATCH_PER_STEP == 0
    n = batch * seq
    alpha = float((2 * depth) ** 0.25)
    row = lambda a: a.reshape(1, -1)
    tables = _rope_tables(seq)

    h = _ln_call(x.reshape(n, d), row(ln_in_g), row(ln_in_b))
    for l in range(depth):
        p_rw, p_na, p_mla = _inproj_call(h, _prep_in_proj(w_in[l]))

        mu = jnp.pad(rw_mu[l], ((0, 0), (0, RW_PAD - RW_IN)))
        a2 = _prep_lora(rw_a2[l], 2 * RW_DECAY_RANK)
        scan_prm = (mu, rw_w0[l], _prep_lora(rw_w2[l], 0), rw_a0[l], a2,
                    row(rw_k_k[l]), row(rw_k_a[l]))
        y_f, y_b = _rw_call(p_rw.reshape(batch, seq, RW_PAD), scan_prm, batch, seq)
        y_f = y_f.reshape(n, RW_WIDTH)
        y_b = y_b.reshape(n, RW_WIDTH)
        g2 = jnp.pad(rw_g2[l], ((0, LANES - RW_GATE_RANK), (0, 0))).astype(BF16)
        out_prm = (mu, rw_a0[l], a2, row(rw_k_a[l]), row(rw_r_k[l]),
                   row(rw_gn_g[l]), row(rw_gn_b[l]), g2)
        o_rw = _rwout_call(p_rw, y_f, y_b, out_prm, seq)

        o_na = _na_call(p_na, _na_bias_table(na_rpb[l], seq // GRID_W), batch, seq)

        mla_prm = (row(mla_q_norm_g[l]), row(mla_kv_norm_g[l]),
                   *_prep_mla_weights(mla_w_q_b[l], mla_w_kv_b[l]))
        q, k, v = _mlaproj_call(p_mla, mla_prm, tables, seq)
        o_mla = _mlaattn_call(q, k, v, batch, seq)

        h = _outproj_call(o_rw, o_na, o_mla, h, w_out[l].astype(BF16),
                          row(ln1_g[l]), row(ln1_b[l]), alpha)
        h = _ffn_call(h, w_ffn_gate[l].astype(BF16), w_ffn_up[l].astype(BF16),
                      w_ffn_down[l].astype(BF16), row(ln2_g[l]), row(ln2_b[l]), alpha)
    return h.reshape(batch, seq, d)
```

```python
import functools

import jax
import jax.numpy as jnp
import numpy as np
from jax import lax
from jax.experimental import pallas as pl
from jax.experimental.pallas import tpu as pltpu

F32 = jnp.float32
BF16 = jnp.bfloat16

D_MODEL = 1024
HEAD_DIM = 64
GRID_W = 64
RW_HEADS = 4
RW_WIDTH = 256
RW_DECAY_RANK = 32
RW_ICLR_RANK = 32
RW_GATE_RANK = 64
RW_GN_EPS = 64e-5
NA_HEADS = 4
NA_WIDTH = 256
NA_WIN_ROWS = 8
NA_WIN_COLS = 16
MLA_HEADS = 8
MLA_Q_RANK = 256
MLA_KV_RANK = 128
MLA_NOPE_DIM = 64
MLA_ROPE_DIM = 32
MLA_V_DIM = 64
MLA_WIDTH = 512
ROPE_THETA = 10000.0
RW_IN = 960
NA_IN = 768
MLA_IN = 416
D_FF = 2816
LN_EPS = 1e-5
RMS_EPS = 1e-6
NEG_INF = -1e30
LOG2_E = 1.4426950408889634

LANES = 128
RW_PAD = 1024
MLA_PAD = 640
IN_PAD = RW_PAD + NA_IN + MLA_PAD
MLA_HEAD_PAD = 128
CHUNK = 64
RW_BATCH_PER_STEP = 4
RW_STAGGER_ROUNDS = 2
ROW_TILE = 512
MM_ROW_TILE = 1024
FFN_TILE = 256
Q_TILE = 256
MLA_HEADS_PER_STEP = 4
NA_ROWS_PER_STEP = 8
VMEM_LIMIT = 56 * 1024 * 1024


def _cparams(n_axes):
    return pltpu.CompilerParams(dimension_semantics=("arbitrary",) * n_axes,
                                vmem_limit_bytes=VMEM_LIMIT)


def _split_bf16(x, parts):
    out = []
    rem = x
    for i in range(parts):
        p = rem.astype(BF16)
        out.append(p)
        if i + 1 < parts:
            rem = rem - p.astype(F32)
    return out


_NN = (((1,), (0,)), ((), ()))
_NT = (((1,), (1,)), ((), ()))


def _dot(a, b, dims=_NN):
    return lax.dot_general(a, b, dims, preferred_element_type=F32)


def _dot1(a, b, dims=_NN):
    return _dot(a.astype(BF16), b.astype(BF16), dims)


def _dot_exact_rhs(a, b_bf16, dims=_NN):
    a0, a1, a2 = _split_bf16(a, 3)
    return _dot(a0, b_bf16, dims) + (_dot(a1, b_bf16, dims) + _dot(a2, b_bf16, dims))


def _sigmoid(x):
    return 1.0 / (1.0 + jnp.exp(-x))


def _softplus(x):
    return jnp.maximum(x, 0.0) + jnp.log(1.0 + jnp.exp(-jnp.abs(x)))


def _layer_norm(x, g, b):
    mu = jnp.mean(x, axis=-1, keepdims=True)
    xc = x - mu
    var = jnp.mean(xc * xc, axis=-1, keepdims=True)
    return xc * lax.rsqrt(var + LN_EPS) * g + b


def _head_sum_matrix(width, head):
    r = lax.broadcasted_iota(jnp.int32, (width, width), 0) // head
    c = lax.broadcasted_iota(jnp.int32, (width, width), 1) // head
    return jnp.where(r == c, 1.0, 0.0).astype(BF16)


def _token_shift(x, prev_row, next_row, mu0, mu1):
    rows = x.shape[0]
    ridx = lax.broadcasted_iota(jnp.int32, x.shape, 0)
    xp = jnp.where(ridx == 0, prev_row, pltpu.roll(x, 1, 0))
    xn = jnp.where(ridx == rows - 1, next_row, pltpu.roll(x, rows - 1, 0))
    return x + mu0 * (xp - x) + mu1 * (xn - x)


def _inproj_body(*refs, tiles_per_seq, apply_ln):
    if apply_ln:
        (x_ref, xp_ref, xn_ref, g_ref, b_ref, w_ref, mu_ref,
         h_ref, rw_ref, na_ref, mla_ref) = refs
        norm = lambda t: _layer_norm(t, g_ref[...], b_ref[...])
    else:
        x_ref, xp_ref, xn_ref, w_ref, mu_ref, rw_ref, na_ref, mla_ref = refs
        norm = lambda t: t
    i = pl.program_id(0)
    h = norm(x_ref[...])
    if apply_ln:
        h_ref[...] = h
    hb = h.astype(BF16)
    w_rw = w_ref[:, 0:RW_PAD]
    p_rw = _dot(hb, w_rw)
    zero = jnp.zeros((1, RW_PAD), F32)
    p_prev = _dot(norm(xp_ref[...]).astype(BF16), w_rw)[7:8, :]
    p_next = _dot(norm(xn_ref[...]).astype(BF16), w_rw)[0:1, :]
    p_prev = jnp.where(i % tiles_per_seq == 0, zero, p_prev)
    p_next = jnp.where(i % tiles_per_seq == tiles_per_seq - 1, zero, p_next)
    rw_ref[...] = _token_shift(p_rw, p_prev, p_next, mu_ref[0:1, :], mu_ref[1:2, :])
    na_ref[...] = _dot(hb, w_ref[:, RW_PAD:RW_PAD + NA_IN]).astype(BF16)
    mla_ref[...] = _dot(hb, w_ref[:, RW_PAD + NA_IN:IN_PAD])


def _inproj_call(x, w, mu, seq, ln=None):
    n = x.shape[0]
    t8 = MM_ROW_TILE // 8
    last8 = n // 8 - 1
    full = lambda shape: pl.BlockSpec(shape, lambda i: (0,) * len(shape))
    row = lambda width: pl.BlockSpec((MM_ROW_TILE, width), lambda i: (i, 0))
    in_specs = [row(D_MODEL),
                pl.BlockSpec((8, D_MODEL), lambda i: (jnp.maximum(i * t8 - 1, 0), 0)),
                pl.BlockSpec((8, D_MODEL), lambda i: (jnp.minimum((i + 1) * t8, last8), 0))]
    args = [x, x, x]
    out_specs = [row(RW_PAD), row(NA_IN), row(MLA_PAD)]
    out_shape = [jax.ShapeDtypeStruct((n, RW_PAD), F32),
                 jax.ShapeDtypeStruct((n, NA_IN), BF16),
                 jax.ShapeDtypeStruct((n, MLA_PAD), F32)]
    if ln is not None:
        in_specs += [full(ln[0].shape), full(ln[1].shape)]
        args += list(ln)
        out_specs = [row(D_MODEL)] + out_specs
        out_shape = [jax.ShapeDtypeStruct((n, D_MODEL), F32)] + out_shape
    in_specs += [full(w.shape), full(mu.shape)]
    args += [w, mu]
    return pl.pallas_call(
        functools.partial(_inproj_body, tiles_per_seq=seq // MM_ROW_TILE, apply_ln=ln is not None),
        grid=(n // MM_ROW_TILE,),
        in_specs=in_specs,
        out_specs=out_specs,
        out_shape=out_shape,
        compiler_params=_cparams(1),
        name="in_proj",
    )(*args)


def _stack_heads(x, head_masks):
    xb = x.astype(BF16)
    return jnp.concatenate([jnp.where(m, xb, jnp.zeros_like(xb)) for m in head_masks], axis=0)


def _unstack_heads(x):
    c = x.shape[0] // 4
    return (x[0:c] + x[c:2 * c]) + (x[2 * c:3 * c] + x[3 * c:4 * c])


def _neumann_inverse(l):
    n = l.shape[0]
    eye = jnp.where(lax.broadcasted_iota(jnp.int32, (n, n), 0) ==
                    lax.broadcasted_iota(jnp.int32, (n, n), 1), 1.0, 0.0).astype(F32)
    t = eye + l
    p = l
    steps = int(np.log2(CHUNK)) - 1
    for _ in range(steps):
        pb = p.astype(BF16)
        p = _dot(pb, pb)
        t = t + _dot1(t, p)
        yield None
    return t


def _rw_direction(xs, d, prm, h_ref):
    w0_ref, w2_ref, a0_ref, a2_ref, kk_ref, ka_ref = prm
    c = CHUNK
    w = RW_WIDTH
    r = xs[:, 0:w]
    k = xs[:, w:2 * w]
    v = xs[:, 2 * w:3 * w]
    lora = xs[:, 3 * w:3 * w + LANES]
    yield None
    lw = _dot(jnp.tanh(lora).astype(BF16), w2_ref[d])
    la = _dot(lora.astype(BF16), a2_ref[d])
    log_w = -_softplus(-(w0_ref[d:d + 1, :] + lw)) - 0.5
    ldec = -jnp.exp(log_w)
    iclr = _sigmoid(a0_ref[d:d + 1, :] + la)
    yield None
    kk = k * kk_ref[...]
    ss = _dot_exact_rhs(kk * kk, _head_sum_matrix(w, HEAD_DIM))
    kk = kk / jnp.maximum(jnp.sqrt(ss), 1e-12)
    k_dir = k * (1.0 + (iclr - 1.0) * ka_ref[...])
    a = -kk
    b = kk * iclr
    yield None

    ti = lax.broadcasted_iota(jnp.int32, (c, c), 0)
    si = lax.broadcasted_iota(jnp.int32, (c, c), 1)
    cum = jnp.where(si <= ti, 1.0, 0.0).astype(BF16)
    l0, l1, l2 = _split_bf16(ldec, 3)
    cl_incl = _dot(cum, l0) + (_dot(cum, l1) + _dot(cum, l2))
    cl_excl = cl_incl - ldec
    tot = cl_incl[c - 1:c, :]
    yield None
    if d == 0:
        e_a = jnp.exp(cl_excl)
        e_d = jnp.exp(-cl_incl)
        e_r = jnp.exp(cl_incl)
        e_h = jnp.exp(tot - cl_incl)
    else:
        e_a = jnp.exp(tot - cl_incl)
        e_d = jnp.exp(cl_excl - tot)
        e_r = e_a
        e_h = jnp.exp(cl_excl)
    at = a * e_a
    bt = b * e_d
    kt = k_dir * e_d
    rt = r * e_r
    bh = b * e_h
    kh = k_dir * e_h
    yield None

    lane_head = lax.broadcasted_iota(jnp.int32, (1, w), 1) // HEAD_DIM
    head_masks = [lane_head == h for h in range(RW_HEADS)]
    xa = _stack_heads(at, head_masks)
    xr = _stack_heads(rt, head_masks)
    xb = _stack_heads(bt, head_masks)
    bk = jnp.concatenate([bt, kt], axis=0).astype(BF16)
    v_b = v.astype(BF16)

    n = RW_HEADS * c
    ri = lax.broadcasted_iota(jnp.int32, (n, n), 0)
    ci = lax.broadcasted_iota(jnp.int32, (n, n), 1)
    same = (ri // c) == (ci // c)
    tc = lax.broadcasted_iota(jnp.int32, (n, 2 * c), 0) % c
    sc = lax.broadcasted_iota(jnp.int32, (n, 2 * c), 1)
    second = sc >= c
    sc = sc % c
    if d == 0:
        strict = same & ((ci % c) < (ri % c))
        strict_c = sc < tc
        incl_c = sc <= tc
    else:
        strict = same & ((ci % c) > (ri % c))
        strict_c = sc > tc
        incl_c = strict_c
    yield None
    lab = jnp.where(strict, _dot(xa, xb, _NT), 0.0)
    pa = _dot(xa, bk, _NT)
    pr = _dot(xr, bk, _NT)
    lak_c = jnp.where(strict_c & second, pa, 0.0).astype(BF16)
    mr_c = jnp.where(incl_c, pr, 0.0).astype(BF16)
    yield None

    t = (yield from _neumann_inverse(lab)).astype(BF16)
    own = (lax.broadcasted_iota(jnp.int32, (n, w), 0) // c ==
           lax.broadcasted_iota(jnp.int32, (n, w), 1) // HEAD_DIM)
    vv = jnp.concatenate([v_b, v_b], axis=0)
    q = jnp.where(own, _dot(lak_c, vv), 0.0)
    ta = _dot(t, xa)
    yield None
    u0 = _dot(t, q.astype(BF16))
    ta_c = _unstack_heads(ta).astype(BF16)
    u0_c = _unstack_heads(u0)
    bk_t = jnp.concatenate([bh, kh], axis=0).T.astype(BF16)
    w_col = jnp.exp(jnp.sum(ldec.T, axis=1, keepdims=True))
    yield None

    h = h_ref[...]
    h_hi, h_lo = _split_bf16(h, 2)
    u_c = (_dot(ta_c, h_hi) + _dot(ta_c, h_lo)) + u0_c
    uv = jnp.concatenate([u_c.astype(BF16), v_b], axis=0)
    rt_b = rt.astype(BF16)
    y = (_dot(rt_b, h_hi) + _dot(rt_b, h_lo)) + _unstack_heads(jnp.where(own, _dot(mr_c, uv), 0.0))
    lane_k = lax.broadcasted_iota(jnp.int32, (w, w), 1)
    row_k = lax.broadcasted_iota(jnp.int32, (w, w), 0)
    bdm = (lane_k // HEAD_DIM) == (row_k // HEAD_DIM)
    h_ref[...] = w_col * h + jnp.where(bdm, _dot(bk_t, uv), 0.0)
    return y


def _run_interleaved(gens, delays):
    results = [None] * len(gens)
    live = list(range(len(gens)))
    rnd = 0
    while live:
        for i in list(live):
            if rnd < delays[i]:
                continue
            try:
                next(gens[i])
            except StopIteration as stop:
                results[i] = stop.value
                live.remove(i)
        rnd += 1
    return results


def _rw_body(xf_ref, xb_ref, w0_ref, w2_ref, a0_ref, a2_ref, kk_ref, ka_ref,
             yf_ref, yb_ref, hf_ref, hb_ref):
    ch = pl.program_id(1)

    @pl.when(ch == 0)
    def _():
        hf_ref[...] = jnp.zeros_like(hf_ref)
        hb_ref[...] = jnp.zeros_like(hb_ref)

    prm = (w0_ref, w2_ref, a0_ref, a2_ref, kk_ref, ka_ref)
    gens = []
    for bb in range(RW_BATCH_PER_STEP):
        gens.append(_rw_direction(xf_ref[bb], 0, prm, hf_ref.at[bb]))
        gens.append(_rw_direction(xb_ref[bb], 1, prm, hb_ref.at[bb]))
    delays = [RW_STAGGER_ROUNDS * (i // 2) for i in range(len(gens))]
    ys = _run_interleaved(gens, delays)
    for bb in range(RW_BATCH_PER_STEP):
        yf_ref[bb] = ys[2 * bb]
        yb_ref[bb] = ys[2 * bb + 1]


def _rw_call(xs, prm, batch, seq):
    nc = seq // CHUNK
    bps = RW_BATCH_PER_STEP

    def main_f(b, c):
        return (b, c, 0)

    def main_b(b, c):
        return (b, nc - 1 - c, 0)

    def full(shape):
        return pl.BlockSpec(shape, lambda b, c: (0,) * len(shape))

    w0, w2, a0, a2, kk, ka = prm
    return pl.pallas_call(
        _rw_body,
        grid=(batch // bps, nc),
        in_specs=[pl.BlockSpec((bps, CHUNK, RW_PAD), main_f),
                  pl.BlockSpec((bps, CHUNK, RW_PAD), main_b),
                  full(w0.shape), full(w2.shape), full(a0.shape),
                  full(a2.shape), full(kk.shape), full(ka.shape)],
        out_specs=[pl.BlockSpec((bps, CHUNK, RW_WIDTH), main_f),
                   pl.BlockSpec((bps, CHUNK, RW_WIDTH), main_b)],
        out_shape=[jax.ShapeDtypeStruct((batch, seq, RW_WIDTH), F32),
                   jax.ShapeDtypeStruct((batch, seq, RW_WIDTH), F32)],
        scratch_shapes=[pltpu.VMEM((bps, RW_WIDTH, RW_WIDTH), F32),
                        pltpu.VMEM((bps, RW_WIDTH, RW_WIDTH), F32)],
        compiler_params=_cparams(2),
        name="rwkv_scan",
    )(xs, xs, w0, w2, a0, a2, kk, ka)


def _rwout_body(x_ref, yf_ref, yb_ref, a0_ref, a2_ref, ka_ref, rk_ref,
                gg_ref, gb_ref, g2_ref, o_ref):
    w = RW_WIDTH
    xs = x_ref[...]
    r = xs[:, 0:w]
    k = xs[:, w:2 * w]
    v = xs[:, 2 * w:3 * w]
    lora = xs[:, 3 * w:3 * w + LANES]
    gblk = xs[:, 3 * w + LANES:3 * w + 2 * LANES]
    iclr0 = _sigmoid(a0_ref[0:1, :] + _dot(lora.astype(BF16), a2_ref[0]))
    k_dir0 = k * (1.0 + (iclr0 - 1.0) * ka_ref[...])
    gate = _dot(_sigmoid(gblk).astype(BF16), g2_ref[...])
    hsum = _head_sum_matrix(w, HEAD_DIM)
    y = yf_ref[...] + yb_ref[...]
    mean = _dot_exact_rhs(y, hsum) * (1.0 / HEAD_DIM)
    yc = y - mean
    var = _dot_exact_rhs(yc * yc, hsum) * (1.0 / HEAD_DIM)
    yn = yc * lax.rsqrt(var + RW_GN_EPS) * gg_ref[...] + gb_ref[...]
    bonus = _dot_exact_rhs(r * k_dir0 * rk_ref[...], hsum) * v
    o_ref[...] = ((yn + bonus) * gate).astype(BF16)


def _rwout_call(xs, yf, yb, prm):
    n = xs.shape[0]
    a0, a2, ka, rk, gg, gb, g2 = prm

    def full(shape):
        return pl.BlockSpec(shape, lambda i: (0,) * len(shape))

    return pl.pallas_call(
        _rwout_body,
        grid=(n // ROW_TILE,),
        in_specs=[pl.BlockSpec((ROW_TILE, RW_PAD), lambda i: (i, 0)),
                  pl.BlockSpec((ROW_TILE, RW_WIDTH), lambda i: (i, 0)),
                  pl.BlockSpec((ROW_TILE, RW_WIDTH), lambda i: (i, 0)),
                  full(a0.shape), full(a2.shape), full(ka.shape), full(rk.shape),
                  full(gg.shape), full(gb.shape), full(g2.shape)],
        out_specs=pl.BlockSpec((ROW_TILE, RW_WIDTH), lambda i: (i, 0)),
        out_shape=jax.ShapeDtypeStruct((n, RW_WIDTH), BF16),
        compiler_params=_cparams(1),
        name="rwkv_out",
    )(xs, yf, yb, a0, a2, ka, rk, gg, gb, g2)


def _na_body(q_ref, k_ref, v_ref, bias_ref, o_ref, *, rows):
    g = pl.program_id(1)
    w = NA_WIDTH
    kr = NA_WIN_ROWS
    nk = kr * GRID_W
    nq = NA_HEADS * GRID_W
    qc = lax.broadcasted_iota(jnp.int32, (nq, nk), 0) % GRID_W
    kc = lax.broadcasted_iota(jnp.int32, (nq, nk), 1) % GRID_W
    cs = jnp.clip(qc - NA_WIN_COLS // 2, 0, GRID_W - NA_WIN_COLS)
    col_ok = (kc >= cs) & (kc < cs + NA_WIN_COLS)
    lane_head = lax.broadcasted_iota(jnp.int32, (1, w), 1) // HEAD_DIM
    head_masks = [lane_head == h for h in range(NA_HEADS)]
    own_head = (lax.broadcasted_iota(jnp.int32, (nq, w), 0) // GRID_W ==
                lax.broadcasted_iota(jnp.int32, (nq, w), 1) // HEAD_DIM)
    for j in range(NA_ROWS_PER_STEP):
        i = g * NA_ROWS_PER_STEP + j
        row_start = jnp.clip(i - kr // 2, 0, rows - kr)
        delta = i - row_start
        start = pl.multiple_of(row_start * GRID_W, GRID_W)
        q = q_ref[j * GRID_W:(j + 1) * GRID_W, :] * (HEAD_DIM ** -0.5)
        q_st = jnp.concatenate([jnp.where(m, q, jnp.zeros_like(q)) for m in head_masks], axis=0)
        kwin = k_ref[pl.ds(start, nk), :]
        vwin = v_ref[pl.ds(start, nk), :]
        s = jnp.where(col_ok, _dot(q_st, kwin, _NT) + bias_ref[delta], NEG_INF)
        m = jnp.max(s, axis=-1, keepdims=True)
        e = jnp.exp(s - m)
        inv = 1.0 / jnp.sum(e, axis=-1, keepdims=True)
        o_st = jnp.where(own_head, _dot(e.astype(BF16), vwin) * inv, 0.0)
        o_ref[j * GRID_W:(j + 1) * GRID_W, :] = _unstack_heads(o_st).astype(BF16)


def _na_call(p_na, bias, batch, seq):
    n = p_na.shape[0]
    rows = seq // GRID_W
    steps = rows // NA_ROWS_PER_STEP
    blk = NA_ROWS_PER_STEP * GRID_W
    return pl.pallas_call(
        functools.partial(_na_body, rows=rows),
        grid=(batch, steps),
        in_specs=[pl.BlockSpec((blk, NA_WIDTH), lambda b, i: (b * steps + i, 0)),
                  pl.BlockSpec((seq, NA_WIDTH), lambda b, i: (b, 1)),
                  pl.BlockSpec((seq, NA_WIDTH), lambda b, i: (b, 2)),
                  pl.BlockSpec(bias.shape, lambda b, i: (0, 0, 0))],
        out_specs=pl.BlockSpec((blk, NA_WIDTH), lambda b, i: (b * steps + i, 0)),
        out_shape=jax.ShapeDtypeStruct((n, NA_WIDTH), BF16),
        compiler_params=_cparams(2),
        name="na_attn",
    )(p_na, p_na, p_na, bias)


def _mlaproj_body(p_ref, qg_ref, kvg_ref, wqa_ref, wqb_ref, wk_ref, wv_ref,
                  cq_ref, sq_ref, ck_ref, sk_ref, q_ref, k_ref, v_ref):
    p = p_ref[...]
    cq = p[:, 0:MLA_Q_RANK]
    ckv = p[:, MLA_Q_RANK:MLA_Q_RANK + MLA_KV_RANK]
    kpe = p[:, MLA_Q_RANK + MLA_KV_RANK:MLA_Q_RANK + MLA_KV_RANK + LANES]
    kpe_rot = p[:, MLA_Q_RANK + MLA_KV_RANK + LANES:MLA_PAD]
    xq = cq * lax.rsqrt(jnp.mean(cq * cq, axis=-1, keepdims=True) + RMS_EPS) * qg_ref[...]
    xq = xq.astype(BF16)
    xkv = ckv * lax.rsqrt(jnp.mean(ckv * ckv, axis=-1, keepdims=True) + RMS_EPS) * kvg_ref[...]
    xkv = xkv.astype(BF16)
    qa = _dot(xq, wqa_ref[...])
    qb = _dot(xq, wqb_ref[...])
    scale = (MLA_NOPE_DIM + MLA_ROPE_DIM) ** -0.5 * LOG2_E
    kn = _dot(xkv, wk_ref[...])
    kpe_r = kpe * ck_ref[...] + kpe_rot * sk_ref[...]
    cq_t = cq_ref[...]
    sq_t = sq_ref[...]
    for h in range(MLA_HEADS):
        sl = slice(h * MLA_HEAD_PAD, (h + 1) * MLA_HEAD_PAD)
        q_ref[:, sl] = ((qa[:, sl] * cq_t + qb[:, sl] * sq_t) * scale).astype(BF16)
        k_ref[:, sl] = (kn[:, sl] + kpe_r).astype(BF16)
    lane = lax.broadcasted_iota(jnp.int32, (1, MLA_HEADS * MLA_HEAD_PAD), 1) % MLA_HEAD_PAD
    ones_cols = jnp.where(lane >= MLA_V_DIM, 1.0, 0.0).astype(F32)
    v_ref[...] = (_dot(xkv, wv_ref[...]) + ones_cols).astype(BF16)


def _mlaproj_call(p_mla, prm, tables, seq):
    n = p_mla.shape[0]
    qg, kvg, wqa, wqb, wk, wv = prm
    tps = seq // ROW_TILE
    hw = MLA_HEADS * MLA_HEAD_PAD

    def full(shape):
        return pl.BlockSpec(shape, lambda i: (0,) * len(shape))

    tab = pl.BlockSpec((ROW_TILE, LANES), lambda i: (i % tps, 0))
    return pl.pallas_call(
        _mlaproj_body,
        grid=(n // ROW_TILE,),
        in_specs=[pl.BlockSpec((ROW_TILE, MLA_PAD), lambda i: (i, 0)),
                  full(qg.shape), full(kvg.shape), full(wqa.shape), full(wqb.shape),
                  full(wk.shape), full(wv.shape), tab, tab, tab, tab],
        out_specs=[pl.BlockSpec((ROW_TILE, hw), lambda i: (i, 0)),
                   pl.BlockSpec((ROW_TILE, hw), lambda i: (i, 0)),
                   pl.BlockSpec((ROW_TILE, hw), lambda i: (i, 0))],
        out_shape=[jax.ShapeDtypeStruct((n, hw), BF16),
                   jax.ShapeDtypeStruct((n, hw), BF16),
                   jax.ShapeDtypeStruct((n, hw), BF16)],
        compiler_params=_cparams(1),
        name="mla_proj",
    )(p_mla, qg, kvg, wqa, wqb, wk, wv, *tables)


def _mlaattn_body(q_ref, k_ref, v_ref, o_ref):
    lane = lax.broadcasted_iota(jnp.int32, (Q_TILE, 2 * MLA_V_DIM), 1)
    first = lane < MLA_V_DIM
    for pair in range(MLA_HEADS_PER_STEP // 2):
        ov = []
        for h in (2 * pair, 2 * pair + 1):
            sl = slice(h * MLA_HEAD_PAD, (h + 1) * MLA_HEAD_PAD)
            s = _dot(q_ref[:, sl], k_ref[:, sl], _NT)
            m = jnp.max(s, axis=-1, keepdims=True)
            e = jnp.exp2(s - m).astype(BF16)
            ov.append(_dot(e, v_ref[:, sl]))
        a, b = ov
        num = jnp.where(first, a, pltpu.roll(b, MLA_V_DIM, 1))
        den = jnp.where(first, pltpu.roll(a, MLA_V_DIM, 1), b)
        o_ref[:, pair * 2 * MLA_V_DIM:(pair + 1) * 2 * MLA_V_DIM] = (num / den).astype(BF16)


def _mlaattn_call(q, k, v, batch, seq):
    n = q.shape[0]
    qt = seq // Q_TILE
    groups = MLA_HEADS // MLA_HEADS_PER_STEP
    wide = MLA_HEADS_PER_STEP * MLA_HEAD_PAD
    return pl.pallas_call(
        _mlaattn_body,
        grid=(batch, groups, qt),
        in_specs=[pl.BlockSpec((Q_TILE, wide), lambda b, hg, i: (b * qt + i, hg)),
                  pl.BlockSpec((seq, wide), lambda b, hg, i: (b, hg)),
                  pl.BlockSpec((seq, wide), lambda b, hg, i: (b, hg))],
        out_specs=pl.BlockSpec((Q_TILE, MLA_HEADS_PER_STEP * MLA_V_DIM),
                               lambda b, hg, i: (b * qt + i, hg)),
        out_shape=jax.ShapeDtypeStruct((n, MLA_WIDTH), BF16),
        compiler_params=_cparams(3),
        name="mla_attn",
    )(q, k, v)


def _outproj_body(rw_ref, na_ref, mla_ref, h_ref, w_ref, g_ref, b_ref, o_ref, *, alpha):
    mix = _dot(rw_ref[...], w_ref[0:RW_WIDTH, :])
    mix = mix + _dot(na_ref[...], w_ref[RW_WIDTH:RW_WIDTH + NA_WIDTH, :])
    mix = mix + _dot(mla_ref[...], w_ref[RW_WIDTH + NA_WIDTH:, :])
    o_ref[...] = _layer_norm(alpha * h_ref[...] + mix, g_ref[...], b_ref[...])


def _outproj_call(o_rw, o_na, o_mla, h, w, g, b, alpha):
    n = h.shape[0]
    row = lambda width: pl.BlockSpec((MM_ROW_TILE, width), lambda i: (i, 0))
    return pl.pallas_call(
        functools.partial(_outproj_body, alpha=alpha),
        grid=(n // MM_ROW_TILE,),
        in_specs=[row(RW_WIDTH), row(NA_WIDTH), row(MLA_WIDTH), row(D_MODEL),
                  pl.BlockSpec((D_MODEL, D_MODEL), lambda i: (0, 0)),
                  pl.BlockSpec((1, D_MODEL), lambda i: (0, 0)),
                  pl.BlockSpec((1, D_MODEL), lambda i: (0, 0))],
        out_specs=row(D_MODEL),
        out_shape=jax.ShapeDtypeStruct((n, D_MODEL), F32),
        compiler_params=_cparams(1),
        name="out_proj",
    )(o_rw, o_na, o_mla, h, w, g, b)


def _ffn_body(h_ref, wg_ref, wu_ref, wd_ref, g_ref, b_ref, o_ref, *, alpha):
    h = h_ref[...]
    hb = h.astype(BF16)
    acc = alpha * h
    for j in range(D_FF // FFN_TILE):
        cols = slice(j * FFN_TILE, (j + 1) * FFN_TILE)
        gt = _dot(hb, wg_ref[:, cols])
        up = _dot(hb, wu_ref[:, cols])
        act = (gt * _sigmoid(gt) * up).astype(BF16)
        acc = acc + _dot(act, wd_ref[cols, :])
    o_ref[...] = _layer_norm(acc, g_ref[...], b_ref[...])


def _ffn_call(h, wg, wu, wd, g, b, alpha):
    n = h.shape[0]
    resident = lambda shape: pl.BlockSpec(shape, lambda i: (0, 0), pipeline_mode=pl.Buffered(1))
    return pl.pallas_call(
        functools.partial(_ffn_body, alpha=alpha),
        grid=(n // MM_ROW_TILE,),
        in_specs=[pl.BlockSpec((MM_ROW_TILE, D_MODEL), lambda i: (i, 0)),
                  resident((D_MODEL, D_FF)), resident((D_MODEL, D_FF)), resident((D_FF, D_MODEL)),
                  pl.BlockSpec((1, D_MODEL), lambda i: (0, 0)),
                  pl.BlockSpec((1, D_MODEL), lambda i: (0, 0))],
        out_specs=pl.BlockSpec((MM_ROW_TILE, D_MODEL), lambda i: (i, 0)),
        out_shape=jax.ShapeDtypeStruct((n, D_MODEL), F32),
        compiler_params=_cparams(1),
        name="ffn",
    )(h, wg, wu, wd, g, b)


def _rope_rotate_cols(w_pe):
    half = MLA_ROPE_DIM // 2
    return jnp.concatenate([-w_pe[..., half:], w_pe[..., :half]], axis=-1)


def _prep_in_proj(w_in):
    d = w_in.shape[0]
    rw = w_in[:, :RW_IN]
    na = w_in[:, RW_IN:RW_IN + NA_IN]
    mla = w_in[:, RW_IN + NA_IN:]
    cq_ckv = mla[:, :MLA_Q_RANK + MLA_KV_RANK]
    kpe = mla[:, MLA_Q_RANK + MLA_KV_RANK:]
    z = lambda c: jnp.zeros((d, c), w_in.dtype)
    tail = LANES - MLA_NOPE_DIM - MLA_ROPE_DIM
    w = jnp.concatenate([rw, z(RW_PAD - RW_IN), na, cq_ckv,
                         z(MLA_NOPE_DIM), kpe, z(tail),
                         z(MLA_NOPE_DIM), _rope_rotate_cols(kpe), z(tail)], axis=1)
    return w.astype(BF16)


def _prep_lora(w2, row_offset):
    out = jnp.zeros((2, LANES, w2.shape[-1]), w2.dtype)
    rank = w2.shape[1]
    for d in range(2):
        out = out.at[d, row_offset + d * rank:row_offset + (d + 1) * rank].set(w2[d])
    return out.astype(BF16)


def _prep_mla_weights(w_q_b, w_kv_b):
    qd = MLA_NOPE_DIM + MLA_ROPE_DIM
    wq = w_q_b.reshape(MLA_Q_RANK, MLA_HEADS, qd)
    tail = MLA_HEAD_PAD - qd
    zq = lambda c: jnp.zeros((MLA_Q_RANK, MLA_HEADS, c), w_q_b.dtype)
    wqa = jnp.concatenate([wq, zq(tail)], axis=-1).reshape(MLA_Q_RANK, -1)
    wqb = jnp.concatenate([zq(MLA_NOPE_DIM), _rope_rotate_cols(wq[..., MLA_NOPE_DIM:]), zq(tail)],
                          axis=-1).reshape(MLA_Q_RANK, -1)
    wkv = w_kv_b.reshape(MLA_KV_RANK, MLA_HEADS, MLA_NOPE_DIM + MLA_V_DIM)
    zk = jnp.zeros((MLA_KV_RANK, MLA_HEADS, MLA_HEAD_PAD - MLA_NOPE_DIM), w_kv_b.dtype)
    wk = jnp.concatenate([wkv[..., :MLA_NOPE_DIM], zk], axis=-1).reshape(MLA_KV_RANK, -1)
    zv = jnp.zeros((MLA_KV_RANK, MLA_HEADS, MLA_HEAD_PAD - MLA_V_DIM), w_kv_b.dtype)
    wv = jnp.concatenate([wkv[..., MLA_NOPE_DIM:], zv], axis=-1).reshape(MLA_KV_RANK, -1)
    return wqa.astype(BF16), wqb.astype(BF16), wk.astype(BF16), wv.astype(BF16)


def _rope_tables(seq):
    inv_freq = jnp.power(ROPE_THETA, -jnp.arange(0, MLA_ROPE_DIM, 2, dtype=F32) / MLA_ROPE_DIM)
    ang = jnp.arange(seq, dtype=F32)[:, None] * inv_freq[None, :]
    cos2 = jnp.concatenate([jnp.cos(ang), jnp.cos(ang)], axis=-1)
    sin2 = jnp.concatenate([jnp.sin(ang), jnp.sin(ang)], axis=-1)
    tail = LANES - MLA_NOPE_DIM - MLA_ROPE_DIM
    one = jnp.ones((seq, MLA_NOPE_DIM), F32)
    z = lambda c: jnp.zeros((seq, c), F32)
    cq = jnp.concatenate([one, cos2, z(tail)], axis=-1)
    sq = jnp.concatenate([z(MLA_NOPE_DIM), sin2, z(tail)], axis=-1)
    ck = jnp.concatenate([z(MLA_NOPE_DIM), cos2, z(tail)], axis=-1)
    return cq, sq, ck, sq


def _na_bias_table(rpb, rows):
    kr = min(NA_WIN_ROWS, rows)
    heads, n_dr, n_dc = rpb.shape
    edge = GRID_W - 1 - (NA_WIN_COLS - 1)
    ext = jnp.concatenate([jnp.broadcast_to(rpb[..., :1], (heads, n_dr, edge)), rpb,
                           jnp.broadcast_to(rpb[..., -1:], (heads, n_dr, edge))], axis=-1)
    toep = jnp.stack([ext[..., GRID_W - 1 - q:2 * GRID_W - 1 - q] for q in range(GRID_W)],
                     axis=-2)
    slabs = []
    for delta in range(kr):
        lo = NA_WIN_ROWS - 1 - delta
        s = toep[:, lo:lo + kr]
        slabs.append(jnp.transpose(s, (0, 2, 1, 3)).reshape(heads * GRID_W, kr * GRID_W))
    return jnp.stack(slabs, axis=0).astype(F32)


def kernel(x, ln_in_g, ln_in_b, w_in, rw_mu, rw_w0, rw_w2, rw_a0, rw_a2, rw_g2, rw_k_k, rw_k_a,
           rw_r_k, rw_gn_g, rw_gn_b, na_rpb, mla_q_norm_g, mla_w_q_b, mla_kv_norm_g, mla_w_kv_b,
           w_out, ln1_g, ln1_b, w_ffn_gate, w_ffn_up, w_ffn_down, ln2_g, ln2_b):
    batch, seq, d = x.shape
    depth = w_in.shape[0]
    assert d == D_MODEL and seq % ROW_TILE == 0 and seq % CHUNK == 0 and seq % GRID_W == 0
    assert seq // GRID_W >= NA_WIN_ROWS and D_FF % FFN_TILE == 0 and seq % Q_TILE == 0
    assert seq % MM_ROW_TILE == 0
    assert (seq // GRID_W) % NA_ROWS_PER_STEP == 0 and MLA_V_DIM * 2 == MLA_HEAD_PAD
    assert batch % RW_BATCH_PER_STEP == 0
    n = batch * seq
    alpha = float((2 * depth) ** 0.25)
    row = lambda a: a.reshape(1, -1)
    tables = _rope_tables(seq)

    h = x.reshape(n, d)
    for l in range(depth):
        mu = jnp.pad(rw_mu[l], ((0, 0), (0, RW_PAD - RW_IN)))
        w_l = _prep_in_proj(w_in[l])
        if l == 0:
            h, xs, p_na, p_mla = _inproj_call(h, w_l, mu, seq, ln=(row(ln_in_g), row(ln_in_b)))
        else:
            xs, p_na, p_mla = _inproj_call(h, w_l, mu, seq)

        a2 = _prep_lora(rw_a2[l], 2 * RW_DECAY_RANK)
        scan_prm = (rw_w0[l], _prep_lora(rw_w2[l], 0), rw_a0[l], a2,
                    row(rw_k_k[l]), row(rw_k_a[l]))
        y_f, y_b = _rw_call(xs.reshape(batch, seq, RW_PAD), scan_prm, batch, seq)
        y_f = y_f.reshape(n, RW_WIDTH)
        y_b = y_b.reshape(n, RW_WIDTH)
        g2 = jnp.pad(rw_g2[l], ((0, LANES - RW_GATE_RANK), (0, 0))).astype(BF16)
        out_prm = (rw_a0[l], a2, row(rw_k_a[l]), row(rw_r_k[l]),
                   row(rw_gn_g[l]), row(rw_gn_b[l]), g2)
        o_rw = _rwout_call(xs, y_f, y_b, out_prm)

        o_na = _na_call(p_na, _na_bias_table(na_rpb[l], seq // GRID_W), batch, seq)

        mla_prm = (row(mla_q_norm_g[l]), row(mla_kv_norm_g[l]),
                   *_prep_mla_weights(mla_w_q_b[l], mla_w_kv_b[l]))
        q, k, v = _mlaproj_call(p_mla, mla_prm, tables, seq)
        o_mla = _mlaattn_call(q, k, v, batch, seq)

        h = _outproj_call(o_rw, o_na, o_mla, h, w_out[l].astype(BF16),
                          row(ln1_g[l]), row(ln1_b[l]), alpha)
        h = _ffn_call(h, w_ffn_gate[l].astype(BF16), w_ffn_up[l].astype(BF16),
                      w_ffn_down[l].astype(BF16), row(ln2_g[l]), row(ln2_b[l]), alpha)
    return h.reshape(batch, seq, d)
```

```python
import functools

import jax
import jax.numpy as jnp
import numpy as np
from jax import lax
from jax.experimental import pallas as pl
from jax.experimental.pallas import tpu as pltpu

F32 = jnp.float32
BF16 = jnp.bfloat16

D_MODEL = 1024
HEAD_DIM = 64
GRID_W = 64
RW_HEADS = 4
RW_WIDTH = 256
RW_DECAY_RANK = 32
RW_ICLR_RANK = 32
RW_GATE_RANK = 64
RW_GN_EPS = 64e-5
NA_HEADS = 4
NA_WIDTH = 256
NA_WIN_ROWS = 8
NA_WIN_COLS = 16
MLA_HEADS = 8
MLA_Q_RANK = 256
MLA_KV_RANK = 128
MLA_NOPE_DIM = 64
MLA_ROPE_DIM = 32
MLA_V_DIM = 64
MLA_WIDTH = 512
ROPE_THETA = 10000.0
RW_IN = 960
NA_IN = 768
MLA_IN = 416
D_FF = 2816
LN_EPS = 1e-5
RMS_EPS = 1e-6
NEG_INF = -1e30
LOG2_E = 1.4426950408889634

LANES = 128
RW_PAD = 1024
MLA_PAD = 640
IN_PAD = RW_PAD + NA_IN + MLA_PAD
MLA_HEAD_PAD = 128
CHUNK = 64
RW_BATCH_PER_STEP = 4
RW_STAGGER_ROUNDS = 2
ROW_TILE = 512
MM_ROW_TILE = 1024
FFN_TILE = 256
Q_TILE = 256
MLA_HEADS_PER_STEP = 8
MLA_KEY_CHUNK = 1024
NA_ROWS_PER_STEP = 8
VMEM_LIMIT = 56 * 1024 * 1024


def _cparams(n_axes):
    return pltpu.CompilerParams(dimension_semantics=("arbitrary",) * n_axes,
                                vmem_limit_bytes=VMEM_LIMIT)


def _split_bf16(x, parts):
    out = []
    rem = x
    for i in range(parts):
        p = rem.astype(BF16)
        out.append(p)
        if i + 1 < parts:
            rem = rem - p.astype(F32)
    return out


_NN = (((1,), (0,)), ((), ()))
_NT = (((1,), (1,)), ((), ()))


def _dot(a, b, dims=_NN):
    return lax.dot_general(a, b, dims, preferred_element_type=F32)


def _dot1(a, b, dims=_NN):
    return _dot(a.astype(BF16), b.astype(BF16), dims)


def _dot_exact_rhs(a, b_bf16, dims=_NN):
    a0, a1, a2 = _split_bf16(a, 3)
    return _dot(a0, b_bf16, dims) + (_dot(a1, b_bf16, dims) + _dot(a2, b_bf16, dims))


def _sigmoid(x):
    return 1.0 / (1.0 + jnp.exp(-x))


def _softplus(x):
    return jnp.maximum(x, 0.0) + jnp.log(1.0 + jnp.exp(-jnp.abs(x)))


def _layer_norm(x, g, b):
    mu = jnp.mean(x, axis=-1, keepdims=True)
    xc = x - mu
    var = jnp.mean(xc * xc, axis=-1, keepdims=True)
    return xc * lax.rsqrt(var + LN_EPS) * g + b


def _head_sum_matrix(width, head):
    r = lax.broadcasted_iota(jnp.int32, (width, width), 0) // head
    c = lax.broadcasted_iota(jnp.int32, (width, width), 1) // head
    return jnp.where(r == c, 1.0, 0.0).astype(BF16)


def _token_shift(x, prev_row, next_row, mu0, mu1):
    rows = x.shape[0]
    ridx = lax.broadcasted_iota(jnp.int32, x.shape, 0)
    xp = jnp.where(ridx == 0, prev_row, pltpu.roll(x, 1, 0))
    xn = jnp.where(ridx == rows - 1, next_row, pltpu.roll(x, rows - 1, 0))
    return x + mu0 * (xp - x) + mu1 * (xn - x)


def _inproj_body(*refs, tiles_per_seq, apply_ln):
    if apply_ln:
        (x_ref, xp_ref, xn_ref, g_ref, b_ref, w_ref, mu_ref,
         h_ref, rw_ref, na_ref, mla_ref) = refs
        norm = lambda t: _layer_norm(t, g_ref[...], b_ref[...])
    else:
        x_ref, xp_ref, xn_ref, w_ref, mu_ref, rw_ref, na_ref, mla_ref = refs
        norm = lambda t: t
    i = pl.program_id(0)
    h = norm(x_ref[...])
    if apply_ln:
        h_ref[...] = h
    hb = h.astype(BF16)
    w_rw = w_ref[:, 0:RW_PAD]
    p_rw = _dot(hb, w_rw)
    zero = jnp.zeros((1, RW_PAD), F32)
    p_prev = _dot(norm(xp_ref[...]).astype(BF16), w_rw)[7:8, :]
    p_next = _dot(norm(xn_ref[...]).astype(BF16), w_rw)[0:1, :]
    p_prev = jnp.where(i % tiles_per_seq == 0, zero, p_prev)
    p_next = jnp.where(i % tiles_per_seq == tiles_per_seq - 1, zero, p_next)
    rw_ref[...] = _token_shift(p_rw, p_prev, p_next, mu_ref[0:1, :], mu_ref[1:2, :])
    na_ref[...] = _dot(hb, w_ref[:, RW_PAD:RW_PAD + NA_IN]).astype(BF16)
    mla_ref[...] = _dot(hb, w_ref[:, RW_PAD + NA_IN:IN_PAD])


def _inproj_call(x, w, mu, seq, ln=None):
    n = x.shape[0]
    t8 = MM_ROW_TILE // 8
    last8 = n // 8 - 1
    full = lambda shape: pl.BlockSpec(shape, lambda i: (0,) * len(shape))
    row = lambda width: pl.BlockSpec((MM_ROW_TILE, width), lambda i: (i, 0))
    in_specs = [row(D_MODEL),
                pl.BlockSpec((8, D_MODEL), lambda i: (jnp.maximum(i * t8 - 1, 0), 0)),
                pl.BlockSpec((8, D_MODEL), lambda i: (jnp.minimum((i + 1) * t8, last8), 0))]
    args = [x, x, x]
    out_specs = [row(RW_PAD), row(NA_IN), row(MLA_PAD)]
    out_shape = [jax.ShapeDtypeStruct((n, RW_PAD), F32),
                 jax.ShapeDtypeStruct((n, NA_IN), BF16),
                 jax.ShapeDtypeStruct((n, MLA_PAD), F32)]
    if ln is not None:
        in_specs += [full(ln[0].shape), full(ln[1].shape)]
        args += list(ln)
        out_specs = [row(D_MODEL)] + out_specs
        out_shape = [jax.ShapeDtypeStruct((n, D_MODEL), F32)] + out_shape
    in_specs += [full(w.shape), full(mu.shape)]
    args += [w, mu]
    return pl.pallas_call(
        functools.partial(_inproj_body, tiles_per_seq=seq // MM_ROW_TILE, apply_ln=ln is not None),
        grid=(n // MM_ROW_TILE,),
        in_specs=in_specs,
        out_specs=out_specs,
        out_shape=out_shape,
        compiler_params=_cparams(1),
        name="in_proj",
    )(*args)


def _stack_heads(x, head_masks):
    xb = x.astype(BF16)
    return jnp.concatenate([jnp.where(m, xb, jnp.zeros_like(xb)) for m in head_masks], axis=0)


def _unstack_heads(x):
    c = x.shape[0] // 4
    return (x[0:c] + x[c:2 * c]) + (x[2 * c:3 * c] + x[3 * c:4 * c])


def _neumann_inverse(l):
    n = l.shape[0]
    eye = jnp.where(lax.broadcasted_iota(jnp.int32, (n, n), 0) ==
                    lax.broadcasted_iota(jnp.int32, (n, n), 1), 1.0, 0.0).astype(F32)
    t = eye + l
    p = l
    steps = int(np.log2(CHUNK)) - 1
    for _ in range(steps):
        pb = p.astype(BF16)
        p = _dot(pb, pb)
        t = t + _dot1(t, p)
        yield None
    return t


def _rw_direction(xs, d, prm, h_ref):
    w0_ref, w2_ref, a0_ref, a2_ref, kk_ref, ka_ref = prm
    c = CHUNK
    w = RW_WIDTH
    r = xs[:, 0:w]
    k = xs[:, w:2 * w]
    v = xs[:, 2 * w:3 * w]
    lora = xs[:, 3 * w:3 * w + LANES]
    yield None
    lw = _dot(jnp.tanh(lora).astype(BF16), w2_ref[d])
    la = _dot(lora.astype(BF16), a2_ref[d])
    log_w = -_softplus(-(w0_ref[d:d + 1, :] + lw)) - 0.5
    ldec = -jnp.exp(log_w)
    iclr = _sigmoid(a0_ref[d:d + 1, :] + la)
    yield None
    kk = k * kk_ref[...]
    ss = _dot_exact_rhs(kk * kk, _head_sum_matrix(w, HEAD_DIM))
    kk = kk / jnp.maximum(jnp.sqrt(ss), 1e-12)
    k_dir = k * (1.0 + (iclr - 1.0) * ka_ref[...])
    a = -kk
    b = kk * iclr
    yield None

    ti = lax.broadcasted_iota(jnp.int32, (c, c), 0)
    si = lax.broadcasted_iota(jnp.int32, (c, c), 1)
    cum = jnp.where(si <= ti, 1.0, 0.0).astype(BF16)
    l0, l1, l2 = _split_bf16(ldec, 3)
    cl_incl = _dot(cum, l0) + (_dot(cum, l1) + _dot(cum, l2))
    cl_excl = cl_incl - ldec
    tot = cl_incl[c - 1:c, :]
    yield None
    if d == 0:
        e_a = jnp.exp(cl_excl)
        e_d = jnp.exp(-cl_incl)
        e_r = jnp.exp(cl_incl)
        e_h = jnp.exp(tot - cl_incl)
    else:
        e_a = jnp.exp(tot - cl_incl)
        e_d = jnp.exp(cl_excl - tot)
        e_r = e_a
        e_h = jnp.exp(cl_excl)
    at = a * e_a
    bt = b * e_d
    kt = k_dir * e_d
    rt = r * e_r
    bh = b * e_h
    kh = k_dir * e_h
    yield None

    lane_head = lax.broadcasted_iota(jnp.int32, (1, w), 1) // HEAD_DIM
    head_masks = [lane_head == h for h in range(RW_HEADS)]
    xa = _stack_heads(at, head_masks)
    xr = _stack_heads(rt, head_masks)
    xb = _stack_heads(bt, head_masks)
    bk = jnp.concatenate([bt, kt], axis=0).astype(BF16)
    v_b = v.astype(BF16)

    n = RW_HEADS * c
    ri = lax.broadcasted_iota(jnp.int32, (n, n), 0)
    ci = lax.broadcasted_iota(jnp.int32, (n, n), 1)
    same = (ri // c) == (ci // c)
    tc = lax.broadcasted_iota(jnp.int32, (n, 2 * c), 0) % c
    sc = lax.broadcasted_iota(jnp.int32, (n, 2 * c), 1)
    second = sc >= c
    sc = sc % c
    if d == 0:
        strict = same & ((ci % c) < (ri % c))
        strict_c = sc < tc
        incl_c = sc <= tc
    else:
        strict = same & ((ci % c) > (ri % c))
        strict_c = sc > tc
        incl_c = strict_c
    yield None
    lab = jnp.where(strict, _dot(xa, xb, _NT), 0.0)
    pa = _dot(xa, bk, _NT)
    pr = _dot(xr, bk, _NT)
    lak_c = jnp.where(strict_c & second, pa, 0.0).astype(BF16)
    mr_c = jnp.where(incl_c, pr, 0.0).astype(BF16)
    yield None

    t = (yield from _neumann_inverse(lab)).astype(BF16)
    own = (lax.broadcasted_iota(jnp.int32, (n, w), 0) // c ==
           lax.broadcasted_iota(jnp.int32, (n, w), 1) // HEAD_DIM)
    vv = jnp.concatenate([v_b, v_b], axis=0)
    q = jnp.where(own, _dot(lak_c, vv), 0.0)
    ta = _dot(t, xa)
    yield None
    u0 = _dot(t, q.astype(BF16))
    ta_c = _unstack_heads(ta).astype(BF16)
    u0_c = _unstack_heads(u0)
    bk_t = jnp.concatenate([bh, kh], axis=0).T.astype(BF16)
    w_col = jnp.exp(jnp.sum(ldec.T, axis=1, keepdims=True))
    yield None

    h = h_ref[...]
    h_hi, h_lo = _split_bf16(h, 2)
    u_c = (_dot(ta_c, h_hi) + _dot(ta_c, h_lo)) + u0_c
    uv = jnp.concatenate([u_c.astype(BF16), v_b], axis=0)
    rt_b = rt.astype(BF16)
    y = (_dot(rt_b, h_hi) + _dot(rt_b, h_lo)) + _unstack_heads(jnp.where(own, _dot(mr_c, uv), 0.0))
    lane_k = lax.broadcasted_iota(jnp.int32, (w, w), 1)
    row_k = lax.broadcasted_iota(jnp.int32, (w, w), 0)
    bdm = (lane_k // HEAD_DIM) == (row_k // HEAD_DIM)
    h_ref[...] = w_col * h + jnp.where(bdm, _dot(bk_t, uv), 0.0)
    return y


def _run_interleaved(gens, delays):
    results = [None] * len(gens)
    live = list(range(len(gens)))
    rnd = 0
    while live:
        for i in list(live):
            if rnd < delays[i]:
                continue
            try:
                next(gens[i])
            except StopIteration as stop:
                results[i] = stop.value
                live.remove(i)
        rnd += 1
    return results


def _rw_body(xf_ref, xb_ref, w0_ref, w2_ref, a0_ref, a2_ref, kk_ref, ka_ref,
             yf_ref, yb_ref, hf_ref, hb_ref):
    ch = pl.program_id(1)

    @pl.when(ch == 0)
    def _():
        hf_ref[...] = jnp.zeros_like(hf_ref)
        hb_ref[...] = jnp.zeros_like(hb_ref)

    prm = (w0_ref, w2_ref, a0_ref, a2_ref, kk_ref, ka_ref)
    gens = []
    for bb in range(RW_BATCH_PER_STEP):
        gens.append(_rw_direction(xf_ref[bb], 0, prm, hf_ref.at[bb]))
        gens.append(_rw_direction(xb_ref[bb], 1, prm, hb_ref.at[bb]))
    delays = [RW_STAGGER_ROUNDS * (i // 2) for i in range(len(gens))]
    ys = _run_interleaved(gens, delays)
    for bb in range(RW_BATCH_PER_STEP):
        yf_ref[bb] = ys[2 * bb]
        yb_ref[bb] = ys[2 * bb + 1]


def _rw_call(xs, prm, batch, seq):
    nc = seq // CHUNK
    bps = RW_BATCH_PER_STEP

    def main_f(b, c):
        return (b, c, 0)

    def main_b(b, c):
        return (b, nc - 1 - c, 0)

    def full(shape):
        return pl.BlockSpec(shape, lambda b, c: (0,) * len(shape))

    w0, w2, a0, a2, kk, ka = prm
    return pl.pallas_call(
        _rw_body,
        grid=(batch // bps, nc),
        in_specs=[pl.BlockSpec((bps, CHUNK, RW_PAD), main_f),
                  pl.BlockSpec((bps, CHUNK, RW_PAD), main_b),
                  full(w0.shape), full(w2.shape), full(a0.shape),
                  full(a2.shape), full(kk.shape), full(ka.shape)],
        out_specs=[pl.BlockSpec((bps, CHUNK, RW_WIDTH), main_f),
                   pl.BlockSpec((bps, CHUNK, RW_WIDTH), main_b)],
        out_shape=[jax.ShapeDtypeStruct((batch, seq, RW_WIDTH), F32),
                   jax.ShapeDtypeStruct((batch, seq, RW_WIDTH), F32)],
        scratch_shapes=[pltpu.VMEM((bps, RW_WIDTH, RW_WIDTH), F32),
                        pltpu.VMEM((bps, RW_WIDTH, RW_WIDTH), F32)],
        compiler_params=_cparams(2),
        name="rwkv_scan",
    )(xs, xs, w0, w2, a0, a2, kk, ka)


def _rwout_body(x_ref, yf_ref, yb_ref, a0_ref, a2_ref, ka_ref, rk_ref,
                gg_ref, gb_ref, g2_ref, o_ref):
    w = RW_WIDTH
    xs = x_ref[...]
    r = xs[:, 0:w]
    k = xs[:, w:2 * w]
    v = xs[:, 2 * w:3 * w]
    lora = xs[:, 3 * w:3 * w + LANES]
    gblk = xs[:, 3 * w + LANES:3 * w + 2 * LANES]
    iclr0 = _sigmoid(a0_ref[0:1, :] + _dot(lora.astype(BF16), a2_ref[0]))
    k_dir0 = k * (1.0 + (iclr0 - 1.0) * ka_ref[...])
    gate = _dot(_sigmoid(gblk).astype(BF16), g2_ref[...])
    hsum = _head_sum_matrix(w, HEAD_DIM)
    y = yf_ref[...] + yb_ref[...]
    mean = _dot_exact_rhs(y, hsum) * (1.0 / HEAD_DIM)
    yc = y - mean
    var = _dot_exact_rhs(yc * yc, hsum) * (1.0 / HEAD_DIM)
    yn = yc * lax.rsqrt(var + RW_GN_EPS) * gg_ref[...] + gb_ref[...]
    bonus = _dot_exact_rhs(r * k_dir0 * rk_ref[...], hsum) * v
    o_ref[...] = ((yn + bonus) * gate).astype(BF16)


def _rwout_call(xs, yf, yb, prm):
    n = xs.shape[0]
    a0, a2, ka, rk, gg, gb, g2 = prm

    def full(shape):
        return pl.BlockSpec(shape, lambda i: (0,) * len(shape))

    return pl.pallas_call(
        _rwout_body,
        grid=(n // ROW_TILE,),
        in_specs=[pl.BlockSpec((ROW_TILE, RW_PAD), lambda i: (i, 0)),
                  pl.BlockSpec((ROW_TILE, RW_WIDTH), lambda i: (i, 0)),
                  pl.BlockSpec((ROW_TILE, RW_WIDTH), lambda i: (i, 0)),
                  full(a0.shape), full(a2.shape), full(ka.shape), full(rk.shape),
                  full(gg.shape), full(gb.shape), full(g2.shape)],
        out_specs=pl.BlockSpec((ROW_TILE, RW_WIDTH), lambda i: (i, 0)),
        out_shape=jax.ShapeDtypeStruct((n, RW_WIDTH), BF16),
        compiler_params=_cparams(1),
        name="rwkv_out",
    )(xs, yf, yb, a0, a2, ka, rk, gg, gb, g2)


def _na_body(q_ref, k_ref, v_ref, bias_ref, o_ref, *, rows):
    g = pl.program_id(1)
    w = NA_WIDTH
    kr = NA_WIN_ROWS
    nk = kr * GRID_W
    nq = NA_HEADS * GRID_W
    qc = lax.broadcasted_iota(jnp.int32, (nq, nk), 0) % GRID_W
    kc = lax.broadcasted_iota(jnp.int32, (nq, nk), 1) % GRID_W
    cs = jnp.clip(qc - NA_WIN_COLS // 2, 0, GRID_W - NA_WIN_COLS)
    col_ok = (kc >= cs) & (kc < cs + NA_WIN_COLS)
    lane_head = lax.broadcasted_iota(jnp.int32, (1, w), 1) // HEAD_DIM
    head_masks = [lane_head == h for h in range(NA_HEADS)]
    own_head = (lax.broadcasted_iota(jnp.int32, (nq, w), 0) // GRID_W ==
                lax.broadcasted_iota(jnp.int32, (nq, w), 1) // HEAD_DIM)
    for j in range(NA_ROWS_PER_STEP):
        i = g * NA_ROWS_PER_STEP + j
        row_start = jnp.clip(i - kr // 2, 0, rows - kr)
        delta = i - row_start
        start = pl.multiple_of(row_start * GRID_W, GRID_W)
        q = q_ref[j * GRID_W:(j + 1) * GRID_W, :] * (HEAD_DIM ** -0.5)
        q_st = jnp.concatenate([jnp.where(m, q, jnp.zeros_like(q)) for m in head_masks], axis=0)
        kwin = k_ref[pl.ds(start, nk), :]
        vwin = v_ref[pl.ds(start, nk), :]
        s = jnp.where(col_ok, _dot(q_st, kwin, _NT) + bias_ref[delta], NEG_INF)
        m = jnp.max(s, axis=-1, keepdims=True)
        e = jnp.exp(s - m)
        inv = 1.0 / jnp.sum(e, axis=-1, keepdims=True)
        o_st = jnp.where(own_head, _dot(e.astype(BF16), vwin) * inv, 0.0)
        o_ref[j * GRID_W:(j + 1) * GRID_W, :] = _unstack_heads(o_st).astype(BF16)


def _na_call(p_na, bias, batch, seq):
    n = p_na.shape[0]
    rows = seq // GRID_W
    steps = rows // NA_ROWS_PER_STEP
    blk = NA_ROWS_PER_STEP * GRID_W
    return pl.pallas_call(
        functools.partial(_na_body, rows=rows),
        grid=(batch, steps),
        in_specs=[pl.BlockSpec((blk, NA_WIDTH), lambda b, i: (b * steps + i, 0)),
                  pl.BlockSpec((seq, NA_WIDTH), lambda b, i: (b, 1)),
                  pl.BlockSpec((seq, NA_WIDTH), lambda b, i: (b, 2)),
                  pl.BlockSpec(bias.shape, lambda b, i: (0, 0, 0))],
        out_specs=pl.BlockSpec((blk, NA_WIDTH), lambda b, i: (b * steps + i, 0)),
        out_shape=jax.ShapeDtypeStruct((n, NA_WIDTH), BF16),
        compiler_params=_cparams(2),
        name="na_attn",
    )(p_na, p_na, p_na, bias)


def _mlaproj_body(p_ref, qg_ref, kvg_ref, wqa_ref, wqb_ref, wk_ref, wvt_ref,
                  cq_ref, sq_ref, ck_ref, sk_ref, q_ref, k_ref, vt_ref):
    p = p_ref[...]
    cq = p[:, 0:MLA_Q_RANK]
    ckv = p[:, MLA_Q_RANK:MLA_Q_RANK + MLA_KV_RANK]
    kpe = p[:, MLA_Q_RANK + MLA_KV_RANK:MLA_Q_RANK + MLA_KV_RANK + LANES]
    kpe_rot = p[:, MLA_Q_RANK + MLA_KV_RANK + LANES:MLA_PAD]
    xq = cq * lax.rsqrt(jnp.mean(cq * cq, axis=-1, keepdims=True) + RMS_EPS) * qg_ref[...]
    xq = xq.astype(BF16)
    xkv = ckv * lax.rsqrt(jnp.mean(ckv * ckv, axis=-1, keepdims=True) + RMS_EPS) * kvg_ref[...]
    xkv = xkv.astype(BF16)
    qa = _dot(xq, wqa_ref[...])
    qb = _dot(xq, wqb_ref[...])
    scale = (MLA_NOPE_DIM + MLA_ROPE_DIM) ** -0.5 * LOG2_E
    kn = _dot(xkv, wk_ref[...])
    kpe_r = kpe * ck_ref[...] + kpe_rot * sk_ref[...]
    cq_t = cq_ref[...]
    sq_t = sq_ref[...]
    for h in range(MLA_HEADS):
        sl = slice(h * MLA_HEAD_PAD, (h + 1) * MLA_HEAD_PAD)
        q_ref[:, sl] = ((qa[:, sl] * cq_t + qb[:, sl] * sq_t) * scale).astype(BF16)
        k_ref[:, sl] = (kn[:, sl] + kpe_r).astype(BF16)
    rowi = lax.broadcasted_iota(jnp.int32, (MLA_HEADS * MLA_HEAD_PAD, 1), 0) % MLA_HEAD_PAD
    ones_rows = jnp.where(rowi >= MLA_V_DIM, 1.0, 0.0).astype(F32)
    vt_ref[...] = (_dot(wvt_ref[...], xkv, _NT) + ones_rows).astype(BF16)


def _mlaproj_call(p_mla, prm, tables, seq):
    n = p_mla.shape[0]
    qg, kvg, wqa, wqb, wk, wv = prm
    tps = seq // ROW_TILE
    hw = MLA_HEADS * MLA_HEAD_PAD

    def full(shape):
        return pl.BlockSpec(shape, lambda i: (0,) * len(shape))

    tab = pl.BlockSpec((ROW_TILE, LANES), lambda i: (i % tps, 0))
    return pl.pallas_call(
        _mlaproj_body,
        grid=(n // ROW_TILE,),
        in_specs=[pl.BlockSpec((ROW_TILE, MLA_PAD), lambda i: (i, 0)),
                  full(qg.shape), full(kvg.shape), full(wqa.shape), full(wqb.shape),
                  full(wk.shape), full(wv.shape), tab, tab, tab, tab],
        out_specs=[pl.BlockSpec((ROW_TILE, hw), lambda i: (i, 0)),
                   pl.BlockSpec((ROW_TILE, hw), lambda i: (i, 0)),
                   pl.BlockSpec((hw, ROW_TILE), lambda i: (0, i))],
        out_shape=[jax.ShapeDtypeStruct((n, hw), BF16),
                   jax.ShapeDtypeStruct((n, hw), BF16),
                   jax.ShapeDtypeStruct((hw, n), BF16)],
        compiler_params=_cparams(1),
        name="mla_proj",
    )(p_mla, qg, kvg, wqa, wqb, wk, wv, *tables)


def _column_max(x):
    while x.shape[0] > 8:
        rows = x.shape[0]
        fold = 8 if rows % 64 == 0 else rows // 8
        x = jnp.max(x.reshape(fold, rows // fold, x.shape[1]), axis=0)
    return jnp.max(x, axis=0, keepdims=True)


def _mlaattn_body(q_ref, k_ref, vt_ref, o_ref):
    seq = k_ref.shape[0]
    n_chunks = seq // MLA_KEY_CHUNK
    head = lambda h: slice(h * MLA_HEAD_PAD, (h + 1) * MLA_HEAD_PAD)
    keys = lambda c: slice(c * MLA_KEY_CHUNK, (c + 1) * MLA_KEY_CHUNK)

    def scores(h, c):
        return _dot(k_ref[keys(c), head(h)], q_ref[:, head(h)], _NT)

    nh = MLA_HEADS_PER_STEP
    outs = []
    st = {0: [scores(0, c) for c in range(n_chunks)]}
    if nh > 1:
        st[1] = [scores(1, c) for c in range(n_chunks)]
    col_max = {0: _column_max(functools.reduce(jnp.maximum, st[0]))}
    for h in range(nh):
        if h + 2 < nh:
            st[h + 2] = []
        ot = None
        for c in range(n_chunks):
            if h + 2 < nh:
                st[h + 2].append(scores(h + 2, c))
            et = jnp.exp2(st[h][c] - col_max[h]).astype(BF16)
            part = _dot(vt_ref[head(h), keys(c)], et)
            ot = part if ot is None else ot + part
        if h + 1 < nh:
            col_max[h + 1] = _column_max(functools.reduce(jnp.maximum, st[h + 1]))
        outs.append(ot[0:MLA_V_DIM, :] / ot[MLA_V_DIM:MLA_V_DIM + 1, :])
    o_ref[...] = jnp.concatenate(outs, axis=0).T.astype(BF16)


def _mlaattn_call(q, k, v, batch, seq):
    n = q.shape[0]
    qt = seq // Q_TILE
    groups = MLA_HEADS // MLA_HEADS_PER_STEP
    wide = MLA_HEADS_PER_STEP * MLA_HEAD_PAD
    return pl.pallas_call(
        _mlaattn_body,
        grid=(batch, groups, qt),
        in_specs=[pl.BlockSpec((Q_TILE, wide), lambda b, hg, i: (b * qt + i, hg)),
                  pl.BlockSpec((seq, wide), lambda b, hg, i: (b, hg)),
                  pl.BlockSpec((wide, seq), lambda b, hg, i: (hg, b))],
        out_specs=pl.BlockSpec((Q_TILE, MLA_HEADS_PER_STEP * MLA_V_DIM),
                               lambda b, hg, i: (b * qt + i, hg)),
        out_shape=jax.ShapeDtypeStruct((n, MLA_WIDTH), BF16),
        compiler_params=_cparams(3),
        name="mla_attn",
    )(q, k, v)


def _outproj_body(rw_ref, na_ref, mla_ref, h_ref, w_ref, g_ref, b_ref, o_ref, *, alpha):
    mix = _dot(rw_ref[...], w_ref[0:RW_WIDTH, :])
    mix = mix + _dot(na_ref[...], w_ref[RW_WIDTH:RW_WIDTH + NA_WIDTH, :])
    mix = mix + _dot(mla_ref[...], w_ref[RW_WIDTH + NA_WIDTH:, :])
    o_ref[...] = _layer_norm(alpha * h_ref[...] + mix, g_ref[...], b_ref[...])


def _outproj_call(o_rw, o_na, o_mla, h, w, g, b, alpha):
    n = h.shape[0]
    row = lambda width: pl.BlockSpec((MM_ROW_TILE, width), lambda i: (i, 0))
    return pl.pallas_call(
        functools.partial(_outproj_body, alpha=alpha),
        grid=(n // MM_ROW_TILE,),
        in_specs=[row(RW_WIDTH), row(NA_WIDTH), row(MLA_WIDTH), row(D_MODEL),
                  pl.BlockSpec((D_MODEL, D_MODEL), lambda i: (0, 0)),
                  pl.BlockSpec((1, D_MODEL), lambda i: (0, 0)),
                  pl.BlockSpec((1, D_MODEL), lambda i: (0, 0))],
        out_specs=row(D_MODEL),
        out_shape=jax.ShapeDtypeStruct((n, D_MODEL), F32),
        compiler_params=_cparams(1),
        name="out_proj",
    )(o_rw, o_na, o_mla, h, w, g, b)


def _ffn_body(h_ref, wg_ref, wu_ref, wd_ref, g_ref, b_ref, o_ref, *, alpha):
    h = h_ref[...]
    hb = h.astype(BF16)
    acc = alpha * h
    for j in range(D_FF // FFN_TILE):
        cols = slice(j * FFN_TILE, (j + 1) * FFN_TILE)
        gt = _dot(hb, wg_ref[:, cols])
        up = _dot(hb, wu_ref[:, cols])
        act = (gt * _sigmoid(gt) * up).astype(BF16)
        acc = acc + _dot(act, wd_ref[cols, :])
    o_ref[...] = _layer_norm(acc, g_ref[...], b_ref[...])


def _ffn_call(h, wg, wu, wd, g, b, alpha):
    n = h.shape[0]
    resident = lambda shape: pl.BlockSpec(shape, lambda i: (0, 0), pipeline_mode=pl.Buffered(1))
    return pl.pallas_call(
        functools.partial(_ffn_body, alpha=alpha),
        grid=(n // MM_ROW_TILE,),
        in_specs=[pl.BlockSpec((MM_ROW_TILE, D_MODEL), lambda i: (i, 0)),
                  resident((D_MODEL, D_FF)), resident((D_MODEL, D_FF)), resident((D_FF, D_MODEL)),
                  pl.BlockSpec((1, D_MODEL), lambda i: (0, 0)),
                  pl.BlockSpec((1, D_MODEL), lambda i: (0, 0))],
        out_specs=pl.BlockSpec((MM_ROW_TILE, D_MODEL), lambda i: (i, 0)),
        out_shape=jax.ShapeDtypeStruct((n, D_MODEL), F32),
        compiler_params=_cparams(1),
        name="ffn",
    )(h, wg, wu, wd, g, b)


def _rope_rotate_cols(w_pe):
    half = MLA_ROPE_DIM // 2
    return jnp.concatenate([-w_pe[..., half:], w_pe[..., :half]], axis=-1)


def _prep_in_proj(w_in):
    d = w_in.shape[0]
    rw = w_in[:, :RW_IN]
    na = w_in[:, RW_IN:RW_IN + NA_IN]
    mla = w_in[:, RW_IN + NA_IN:]
    cq_ckv = mla[:, :MLA_Q_RANK + MLA_KV_RANK]
    kpe = mla[:, MLA_Q_RANK + MLA_KV_RANK:]
    z = lambda c: jnp.zeros((d, c), w_in.dtype)
    tail = LANES - MLA_NOPE_DIM - MLA_ROPE_DIM
    w = jnp.concatenate([rw, z(RW_PAD - RW_IN), na, cq_ckv,
                         z(MLA_NOPE_DIM), kpe, z(tail),
                         z(MLA_NOPE_DIM), _rope_rotate_cols(kpe), z(tail)], axis=1)
    return w.astype(BF16)


def _prep_lora(w2, row_offset):
    out = jnp.zeros((2, LANES, w2.shape[-1]), w2.dtype)
    rank = w2.shape[1]
    for d in range(2):
        out = out.at[d, row_offset + d * rank:row_offset + (d + 1) * rank].set(w2[d])
    return out.astype(BF16)


def _prep_mla_weights(w_q_b, w_kv_b):
    qd = MLA_NOPE_DIM + MLA_ROPE_DIM
    wq = w_q_b.reshape(MLA_Q_RANK, MLA_HEADS, qd)
    tail = MLA_HEAD_PAD - qd
    zq = lambda c: jnp.zeros((MLA_Q_RANK, MLA_HEADS, c), w_q_b.dtype)
    wqa = jnp.concatenate([wq, zq(tail)], axis=-1).reshape(MLA_Q_RANK, -1)
    wqb = jnp.concatenate([zq(MLA_NOPE_DIM), _rope_rotate_cols(wq[..., MLA_NOPE_DIM:]), zq(tail)],
                          axis=-1).reshape(MLA_Q_RANK, -1)
    wkv = w_kv_b.reshape(MLA_KV_RANK, MLA_HEADS, MLA_NOPE_DIM + MLA_V_DIM)
    zk = jnp.zeros((MLA_KV_RANK, MLA_HEADS, MLA_HEAD_PAD - MLA_NOPE_DIM), w_kv_b.dtype)
    wk = jnp.concatenate([wkv[..., :MLA_NOPE_DIM], zk], axis=-1).reshape(MLA_KV_RANK, -1)
    zv = jnp.zeros((MLA_KV_RANK, MLA_HEADS, MLA_HEAD_PAD - MLA_V_DIM), w_kv_b.dtype)
    wv = jnp.concatenate([wkv[..., MLA_NOPE_DIM:], zv], axis=-1).reshape(MLA_KV_RANK, -1)
    return wqa.astype(BF16), wqb.astype(BF16), wk.astype(BF16), wv.T.astype(BF16)


def _rope_tables(seq):
    inv_freq = jnp.power(ROPE_THETA, -jnp.arange(0, MLA_ROPE_DIM, 2, dtype=F32) / MLA_ROPE_DIM)
    ang = jnp.arange(seq, dtype=F32)[:, None] * inv_freq[None, :]
    cos2 = jnp.concatenate([jnp.cos(ang), jnp.cos(ang)], axis=-1)
    sin2 = jnp.concatenate([jnp.sin(ang), jnp.sin(ang)], axis=-1)
    tail = LANES - MLA_NOPE_DIM - MLA_ROPE_DIM
    one = jnp.ones((seq, MLA_NOPE_DIM), F32)
    z = lambda c: jnp.zeros((seq, c), F32)
    cq = jnp.concatenate([one, cos2, z(tail)], axis=-1)
    sq = jnp.concatenate([z(MLA_NOPE_DIM), sin2, z(tail)], axis=-1)
    ck = jnp.concatenate([z(MLA_NOPE_DIM), cos2, z(tail)], axis=-1)
    return cq, sq, ck, sq


def _na_bias_table(rpb, rows):
    kr = min(NA_WIN_ROWS, rows)
    heads, n_dr, n_dc = rpb.shape
    edge = GRID_W - 1 - (NA_WIN_COLS - 1)
    ext = jnp.concatenate([jnp.broadcast_to(rpb[..., :1], (heads, n_dr, edge)), rpb,
                           jnp.broadcast_to(rpb[..., -1:], (heads, n_dr, edge))], axis=-1)
    toep = jnp.stack([ext[..., GRID_W - 1 - q:2 * GRID_W - 1 - q] for q in range(GRID_W)],
                     axis=-2)
    slabs = []
    for delta in range(kr):
        lo = NA_WIN_ROWS - 1 - delta
        s = toep[:, lo:lo + kr]
        slabs.append(jnp.transpose(s, (0, 2, 1, 3)).reshape(heads * GRID_W, kr * GRID_W))
    return jnp.stack(slabs, axis=0).astype(F32)


def kernel(x, ln_in_g, ln_in_b, w_in, rw_mu, rw_w0, rw_w2, rw_a0, rw_a2, rw_g2, rw_k_k, rw_k_a,
           rw_r_k, rw_gn_g, rw_gn_b, na_rpb, mla_q_norm_g, mla_w_q_b, mla_kv_norm_g, mla_w_kv_b,
           w_out, ln1_g, ln1_b, w_ffn_gate, w_ffn_up, w_ffn_down, ln2_g, ln2_b):
    batch, seq, d = x.shape
    depth = w_in.shape[0]
    assert d == D_MODEL and seq % ROW_TILE == 0 and seq % CHUNK == 0 and seq % GRID_W == 0
    assert seq // GRID_W >= NA_WIN_ROWS and D_FF % FFN_TILE == 0 and seq % Q_TILE == 0
    assert seq % MM_ROW_TILE == 0
    assert (seq // GRID_W) % NA_ROWS_PER_STEP == 0 and MLA_V_DIM * 2 == MLA_HEAD_PAD
    assert batch % RW_BATCH_PER_STEP == 0
    n = batch * seq
    alpha = float((2 * depth) ** 0.25)
    row = lambda a: a.reshape(1, -1)
    tables = _rope_tables(seq)

    h = x.reshape(n, d)
    for l in range(depth):
        mu = jnp.pad(rw_mu[l], ((0, 0), (0, RW_PAD - RW_IN)))
        w_l = _prep_in_proj(w_in[l])
        if l == 0:
            h, xs, p_na, p_mla = _inproj_call(h, w_l, mu, seq, ln=(row(ln_in_g), row(ln_in_b)))
        else:
            xs, p_na, p_mla = _inproj_call(h, w_l, mu, seq)

        a2 = _prep_lora(rw_a2[l], 2 * RW_DECAY_RANK)
        scan_prm = (rw_w0[l], _prep_lora(rw_w2[l], 0), rw_a0[l], a2,
                    row(rw_k_k[l]), row(rw_k_a[l]))
        y_f, y_b = _rw_call(xs.reshape(batch, seq, RW_PAD), scan_prm, batch, seq)
        y_f = y_f.reshape(n, RW_WIDTH)
        y_b = y_b.reshape(n, RW_WIDTH)
        g2 = jnp.pad(rw_g2[l], ((0, LANES - RW_GATE_RANK), (0, 0))).astype(BF16)
        out_prm = (rw_a0[l], a2, row(rw_k_a[l]), row(rw_r_k[l]),
                   row(rw_gn_g[l]), row(rw_gn_b[l]), g2)
        o_rw = _rwout_call(xs, y_f, y_b, out_prm)

        o_na = _na_call(p_na, _na_bias_table(na_rpb[l], seq // GRID_W), batch, seq)

        mla_prm = (row(mla_q_norm_g[l]), row(mla_kv_norm_g[l]),
                   *_prep_mla_weights(mla_w_q_b[l], mla_w_kv_b[l]))
        q, k, v = _mlaproj_call(p_mla, mla_prm, tables, seq)
        o_mla = _mlaattn_call(q, k, v, batch, seq)

        h = _outproj_call(o_rw, o_na, o_mla, h, w_out[l].astype(BF16),
                          row(ln1_g[l]), row(ln1_b[l]), alpha)
        h = _ffn_call(h, w_ffn_gate[l].astype(BF16), w_ffn_up[l].astype(BF16),
                      w_ffn_down[l].astype(BF16), row(ln2_g[l]), row(ln2_b[l]), alpha)
    return h.reshape(batch, seq, d)
```

```python
import functools

import jax
import jax.numpy as jnp
import numpy as np
from jax import lax
from jax.experimental import pallas as pl
from jax.experimental.pallas import tpu as pltpu

F32 = jnp.float32
BF16 = jnp.bfloat16

D_MODEL = 1024
HEAD_DIM = 64
GRID_W = 64
RW_HEADS = 4
RW_WIDTH = 256
RW_DECAY_RANK = 32
RW_ICLR_RANK = 32
RW_GATE_RANK = 64
RW_GN_EPS = 64e-5
NA_HEADS = 4
NA_WIDTH = 256
NA_WIN_ROWS = 8
NA_WIN_COLS = 16
MLA_HEADS = 8
MLA_Q_RANK = 256
MLA_KV_RANK = 128
MLA_NOPE_DIM = 64
MLA_ROPE_DIM = 32
MLA_V_DIM = 64
MLA_WIDTH = 512
ROPE_THETA = 10000.0
RW_IN = 960
NA_IN = 768
MLA_IN = 416
D_FF = 2816
LN_EPS = 1e-5
RMS_EPS = 1e-6
NEG_INF = -1e30
LOG2_E = 1.4426950408889634

LANES = 128
RW_PAD = 1024
MLA_PAD = 640
IN_PAD = RW_PAD + NA_IN + MLA_PAD
MLA_HEAD_PAD = 128
CHUNK = 64
RW_BATCH_PER_STEP = 4
RW_STAGGER_ROUNDS = 2
ROW_TILE = 512
MM_ROW_TILE = 1024
FFN_TILE = 256
Q_TILE = 256
MLA_HEADS_PER_STEP = 8
MLA_Q_PER_STEP = 2
MLA_KEY_CHUNK = 1024
NA_ROWS_PER_STEP = 8
VMEM_LIMIT = 56 * 1024 * 1024


def _cparams(n_axes):
    return pltpu.CompilerParams(dimension_semantics=("arbitrary",) * n_axes,
                                vmem_limit_bytes=VMEM_LIMIT)


def _split_bf16(x, parts):
    out = []
    rem = x
    for i in range(parts):
        p = rem.astype(BF16)
        out.append(p)
        if i + 1 < parts:
            rem = rem - p.astype(F32)
    return out


_NN = (((1,), (0,)), ((), ()))
_NT = (((1,), (1,)), ((), ()))


def _dot(a, b, dims=_NN):
    return lax.dot_general(a, b, dims, preferred_element_type=F32)


def _dot1(a, b, dims=_NN):
    return _dot(a.astype(BF16), b.astype(BF16), dims)


def _dot_exact_rhs(a, b_bf16, dims=_NN):
    a0, a1, a2 = _split_bf16(a, 3)
    return _dot(a0, b_bf16, dims) + (_dot(a1, b_bf16, dims) + _dot(a2, b_bf16, dims))


def _sigmoid(x):
    return 1.0 / (1.0 + jnp.exp(-x))


def _softplus(x):
    return jnp.maximum(x, 0.0) + jnp.log(1.0 + jnp.exp(-jnp.abs(x)))


def _layer_norm(x, g, b):
    mu = jnp.mean(x, axis=-1, keepdims=True)
    xc = x - mu
    var = jnp.mean(xc * xc, axis=-1, keepdims=True)
    return xc * lax.rsqrt(var + LN_EPS) * g + b


def _head_sum_matrix(width, head):
    r = lax.broadcasted_iota(jnp.int32, (width, width), 0) // head
    c = lax.broadcasted_iota(jnp.int32, (width, width), 1) // head
    return jnp.where(r == c, 1.0, 0.0).astype(BF16)


def _token_shift(x, prev_row, next_row, mu0, mu1):
    rows = x.shape[0]
    ridx = lax.broadcasted_iota(jnp.int32, x.shape, 0)
    xp = jnp.where(ridx == 0, prev_row, pltpu.roll(x, 1, 0))
    xn = jnp.where(ridx == rows - 1, next_row, pltpu.roll(x, rows - 1, 0))
    return x + mu0 * (xp - x) + mu1 * (xn - x)


def _inproj_body(*refs, tiles_per_seq, apply_ln):
    if apply_ln:
        (x_ref, xp_ref, xn_ref, g_ref, b_ref, w_ref, mu_ref,
         h_ref, rw_ref, na_ref, mla_ref) = refs
        norm = lambda t: _layer_norm(t, g_ref[...], b_ref[...])
    else:
        x_ref, xp_ref, xn_ref, w_ref, mu_ref, rw_ref, na_ref, mla_ref = refs
        norm = lambda t: t
    i = pl.program_id(0)
    h = norm(x_ref[...])
    if apply_ln:
        h_ref[...] = h
    hb = h.astype(BF16)
    w_rw = w_ref[:, 0:RW_PAD]
    p_rw = _dot(hb, w_rw)
    zero = jnp.zeros((1, RW_PAD), F32)
    p_prev = _dot(norm(xp_ref[...]).astype(BF16), w_rw)[7:8, :]
    p_next = _dot(norm(xn_ref[...]).astype(BF16), w_rw)[0:1, :]
    p_prev = jnp.where(i % tiles_per_seq == 0, zero, p_prev)
    p_next = jnp.where(i % tiles_per_seq == tiles_per_seq - 1, zero, p_next)
    rw_ref[...] = _token_shift(p_rw, p_prev, p_next, mu_ref[0:1, :], mu_ref[1:2, :])
    na_ref[...] = _dot(hb, w_ref[:, RW_PAD:RW_PAD + NA_IN]).astype(BF16)
    mla_ref[...] = _dot(hb, w_ref[:, RW_PAD + NA_IN:IN_PAD])


def _inproj_call(x, w, mu, seq, ln=None):
    n = x.shape[0]
    t8 = MM_ROW_TILE // 8
    last8 = n // 8 - 1
    full = lambda shape: pl.BlockSpec(shape, lambda i: (0,) * len(shape))
    row = lambda width: pl.BlockSpec((MM_ROW_TILE, width), lambda i: (i, 0))
    in_specs = [row(D_MODEL),
                pl.BlockSpec((8, D_MODEL), lambda i: (jnp.maximum(i * t8 - 1, 0), 0)),
                pl.BlockSpec((8, D_MODEL), lambda i: (jnp.minimum((i + 1) * t8, last8), 0))]
    args = [x, x, x]
    out_specs = [row(RW_PAD), row(NA_IN), row(MLA_PAD)]
    out_shape = [jax.ShapeDtypeStruct((n, RW_PAD), F32),
                 jax.ShapeDtypeStruct((n, NA_IN), BF16),
                 jax.ShapeDtypeStruct((n, MLA_PAD), F32)]
    if ln is not None:
        in_specs += [full(ln[0].shape), full(ln[1].shape)]
        args += list(ln)
        out_specs = [row(D_MODEL)] + out_specs
        out_shape = [jax.ShapeDtypeStruct((n, D_MODEL), F32)] + out_shape
    in_specs += [full(w.shape), full(mu.shape)]
    args += [w, mu]
    return pl.pallas_call(
        functools.partial(_inproj_body, tiles_per_seq=seq // MM_ROW_TILE, apply_ln=ln is not None),
        grid=(n // MM_ROW_TILE,),
        in_specs=in_specs,
        out_specs=out_specs,
        out_shape=out_shape,
        compiler_params=_cparams(1),
        name="in_proj",
    )(*args)


def _stack_heads(x, head_masks):
    xb = x.astype(BF16)
    return jnp.concatenate([jnp.where(m, xb, jnp.zeros_like(xb)) for m in head_masks], axis=0)


def _unstack_heads(x):
    c = x.shape[0] // 4
    return (x[0:c] + x[c:2 * c]) + (x[2 * c:3 * c] + x[3 * c:4 * c])


def _neumann_inverse(l):
    n = l.shape[0]
    eye = jnp.where(lax.broadcasted_iota(jnp.int32, (n, n), 0) ==
                    lax.broadcasted_iota(jnp.int32, (n, n), 1), 1.0, 0.0).astype(F32)
    t = eye + l
    lb = l.astype(BF16)
    p = _dot(lb, lb)
    yield None
    steps = int(np.log2(CHUNK)) - 1
    for i in range(steps):
        pb = p.astype(BF16)
        if i + 1 < steps:
            prod = _dot(jnp.concatenate([t.astype(BF16), pb], axis=0), pb)
            t = t + prod[0:n]
            p = prod[n:2 * n]
        else:
            t = t + _dot(t.astype(BF16), pb)
        yield None
    return t


def _rw_direction(xs, d, prm, h_ref):
    w0_ref, w2_ref, a0_ref, a2_ref, kk_ref, ka_ref = prm
    c = CHUNK
    w = RW_WIDTH
    r = xs[:, 0:w]
    k = xs[:, w:2 * w]
    v = xs[:, 2 * w:3 * w]
    lora = xs[:, 3 * w:3 * w + LANES]
    yield None
    lw = _dot(jnp.tanh(lora).astype(BF16), w2_ref[d])
    la = _dot(lora.astype(BF16), a2_ref[d])
    log_w = -_softplus(-(w0_ref[d:d + 1, :] + lw)) - 0.5
    ldec = -jnp.exp(log_w)
    iclr = _sigmoid(a0_ref[d:d + 1, :] + la)
    yield None
    kk = k * kk_ref[...]
    ss = _dot_exact_rhs(kk * kk, _head_sum_matrix(w, HEAD_DIM))
    kk = kk / jnp.maximum(jnp.sqrt(ss), 1e-12)
    k_dir = k * (1.0 + (iclr - 1.0) * ka_ref[...])
    a = -kk
    b = kk * iclr
    yield None

    ti = lax.broadcasted_iota(jnp.int32, (c, c), 0)
    si = lax.broadcasted_iota(jnp.int32, (c, c), 1)
    cum = jnp.where(si <= ti, 1.0, 0.0).astype(BF16)
    l0, l1, l2 = _split_bf16(ldec, 3)
    cl_incl = _dot(cum, l0) + (_dot(cum, l1) + _dot(cum, l2))
    cl_excl = cl_incl - ldec
    tot = cl_incl[c - 1:c, :]
    yield None
    if d == 0:
        e_a = jnp.exp(cl_excl)
        e_d = jnp.exp(-cl_incl)
        e_r = jnp.exp(cl_incl)
        e_h = jnp.exp(tot - cl_incl)
    else:
        e_a = jnp.exp(tot - cl_incl)
        e_d = jnp.exp(cl_excl - tot)
        e_r = e_a
        e_h = jnp.exp(cl_excl)
    at = a * e_a
    bt = b * e_d
    kt = k_dir * e_d
    rt = r * e_r
    bh = b * e_h
    kh = k_dir * e_h
    yield None

    lane_head = lax.broadcasted_iota(jnp.int32, (1, w), 1) // HEAD_DIM
    head_masks = [lane_head == h for h in range(RW_HEADS)]
    xa = _stack_heads(at, head_masks)
    xr = _stack_heads(rt, head_masks)
    xb = _stack_heads(bt, head_masks)
    bk = jnp.concatenate([bt, kt], axis=0).astype(BF16)
    v_b = v.astype(BF16)

    n = RW_HEADS * c
    ri = lax.broadcasted_iota(jnp.int32, (n, n), 0)
    ci = lax.broadcasted_iota(jnp.int32, (n, n), 1)
    same = (ri // c) == (ci // c)
    tc = lax.broadcasted_iota(jnp.int32, (n, 2 * c), 0) % c
    sc = lax.broadcasted_iota(jnp.int32, (n, 2 * c), 1)
    second = sc >= c
    sc = sc % c
    if d == 0:
        strict = same & ((ci % c) < (ri % c))
        strict_c = sc < tc
        incl_c = sc <= tc
    else:
        strict = same & ((ci % c) > (ri % c))
        strict_c = sc > tc
        incl_c = strict_c
    yield None
    lab = jnp.where(strict, _dot(xa, xb, _NT), 0.0)
    pa = _dot(xa, bk, _NT)
    pr = _dot(xr, bk, _NT)
    lak_c = jnp.where(strict_c & second, pa, 0.0).astype(BF16)
    mr_c = jnp.where(incl_c, pr, 0.0).astype(BF16)
    yield None

    t = (yield from _neumann_inverse(lab)).astype(BF16)
    own = (lax.broadcasted_iota(jnp.int32, (n, w), 0) // c ==
           lax.broadcasted_iota(jnp.int32, (n, w), 1) // HEAD_DIM)
    vv = jnp.concatenate([v_b, v_b], axis=0)
    q = jnp.where(own, _dot(lak_c, vv), 0.0)
    ta = _dot(t, xa)
    yield None
    u0 = _dot(t, q.astype(BF16))
    ta_c = _unstack_heads(ta).astype(BF16)
    u0_c = _unstack_heads(u0)
    bk_t = jnp.concatenate([bh, kh], axis=0).T.astype(BF16)
    w_col = jnp.exp(jnp.sum(ldec.T, axis=1, keepdims=True))
    yield None

    h = h_ref[...]
    h_hi, h_lo = _split_bf16(h, 2)
    u_c = (_dot(ta_c, h_hi) + _dot(ta_c, h_lo)) + u0_c
    uv = jnp.concatenate([u_c.astype(BF16), v_b], axis=0)
    rt_b = rt.astype(BF16)
    y = (_dot(rt_b, h_hi) + _dot(rt_b, h_lo)) + _unstack_heads(jnp.where(own, _dot(mr_c, uv), 0.0))
    lane_k = lax.broadcasted_iota(jnp.int32, (w, w), 1)
    row_k = lax.broadcasted_iota(jnp.int32, (w, w), 0)
    bdm = (lane_k // HEAD_DIM) == (row_k // HEAD_DIM)
    h_ref[...] = w_col * h + jnp.where(bdm, _dot(bk_t, uv), 0.0)
    return y


def _run_interleaved(gens, delays):
    results = [None] * len(gens)
    live = list(range(len(gens)))
    rnd = 0
    while live:
        for i in list(live):
            if rnd < delays[i]:
                continue
            try:
                next(gens[i])
            except StopIteration as stop:
                results[i] = stop.value
                live.remove(i)
        rnd += 1
    return results


def _rw_body(xf_ref, xb_ref, w0_ref, w2_ref, a0_ref, a2_ref, kk_ref, ka_ref,
             yf_ref, yb_ref, hf_ref, hb_ref):
    ch = pl.program_id(1)

    @pl.when(ch == 0)
    def _():
        hf_ref[...] = jnp.zeros_like(hf_ref)
        hb_ref[...] = jnp.zeros_like(hb_ref)

    prm = (w0_ref, w2_ref, a0_ref, a2_ref, kk_ref, ka_ref)
    gens = []
    for bb in range(RW_BATCH_PER_STEP):
        gens.append(_rw_direction(xf_ref[bb], 0, prm, hf_ref.at[bb]))
        gens.append(_rw_direction(xb_ref[bb], 1, prm, hb_ref.at[bb]))
    delays = [RW_STAGGER_ROUNDS * (i // 2) for i in range(len(gens))]
    ys = _run_interleaved(gens, delays)
    for bb in range(RW_BATCH_PER_STEP):
        yf_ref[bb] = ys[2 * bb]
        yb_ref[bb] = ys[2 * bb + 1]


def _rw_call(xs, prm, batch, seq):
    nc = seq // CHUNK
    bps = RW_BATCH_PER_STEP

    def main_f(b, c):
        return (b, c, 0)

    def main_b(b, c):
        return (b, nc - 1 - c, 0)

    def full(shape):
        return pl.BlockSpec(shape, lambda b, c: (0,) * len(shape))

    w0, w2, a0, a2, kk, ka = prm
    return pl.pallas_call(
        _rw_body,
        grid=(batch // bps, nc),
        in_specs=[pl.BlockSpec((bps, CHUNK, RW_PAD), main_f),
                  pl.BlockSpec((bps, CHUNK, RW_PAD), main_b),
                  full(w0.shape), full(w2.shape), full(a0.shape),
                  full(a2.shape), full(kk.shape), full(ka.shape)],
        out_specs=[pl.BlockSpec((bps, CHUNK, RW_WIDTH), main_f),
                   pl.BlockSpec((bps, CHUNK, RW_WIDTH), main_b)],
        out_shape=[jax.ShapeDtypeStruct((batch, seq, RW_WIDTH), F32),
                   jax.ShapeDtypeStruct((batch, seq, RW_WIDTH), F32)],
        scratch_shapes=[pltpu.VMEM((bps, RW_WIDTH, RW_WIDTH), F32),
                        pltpu.VMEM((bps, RW_WIDTH, RW_WIDTH), F32)],
        compiler_params=_cparams(2),
        name="rwkv_scan",
    )(xs, xs, w0, w2, a0, a2, kk, ka)


def _rwout_body(x_ref, yf_ref, yb_ref, a0_ref, a2_ref, ka_ref, rk_ref,
                gg_ref, gb_ref, g2_ref, o_ref):
    w = RW_WIDTH
    xs = x_ref[...]
    r = xs[:, 0:w]
    k = xs[:, w:2 * w]
    v = xs[:, 2 * w:3 * w]
    lora = xs[:, 3 * w:3 * w + LANES]
    gblk = xs[:, 3 * w + LANES:3 * w + 2 * LANES]
    iclr0 = _sigmoid(a0_ref[0:1, :] + _dot(lora.astype(BF16), a2_ref[0]))
    k_dir0 = k * (1.0 + (iclr0 - 1.0) * ka_ref[...])
    gate = _dot(_sigmoid(gblk).astype(BF16), g2_ref[...])
    hsum = _head_sum_matrix(w, HEAD_DIM)
    y = yf_ref[...] + yb_ref[...]
    mean = _dot_exact_rhs(y, hsum) * (1.0 / HEAD_DIM)
    yc = y - mean
    var = _dot_exact_rhs(yc * yc, hsum) * (1.0 / HEAD_DIM)
    yn = yc * lax.rsqrt(var + RW_GN_EPS) * gg_ref[...] + gb_ref[...]
    bonus = _dot_exact_rhs(r * k_dir0 * rk_ref[...], hsum) * v
    o_ref[...] = ((yn + bonus) * gate).astype(BF16)


def _rwout_call(xs, yf, yb, prm):
    n = xs.shape[0]
    a0, a2, ka, rk, gg, gb, g2 = prm

    def full(shape):
        return pl.BlockSpec(shape, lambda i: (0,) * len(shape))

    return pl.pallas_call(
        _rwout_body,
        grid=(n // ROW_TILE,),
        in_specs=[pl.BlockSpec((ROW_TILE, RW_PAD), lambda i: (i, 0)),
                  pl.BlockSpec((ROW_TILE, RW_WIDTH), lambda i: (i, 0)),
                  pl.BlockSpec((ROW_TILE, RW_WIDTH), lambda i: (i, 0)),
                  full(a0.shape), full(a2.shape), full(ka.shape), full(rk.shape),
                  full(gg.shape), full(gb.shape), full(g2.shape)],
        out_specs=pl.BlockSpec((ROW_TILE, RW_WIDTH), lambda i: (i, 0)),
        out_shape=jax.ShapeDtypeStruct((n, RW_WIDTH), BF16),
        compiler_params=_cparams(1),
        name="rwkv_out",
    )(xs, yf, yb, a0, a2, ka, rk, gg, gb, g2)


def _na_body(q_ref, k_ref, v_ref, bias_ref, o_ref, *, rows):
    g = pl.program_id(1)
    w = NA_WIDTH
    kr = NA_WIN_ROWS
    nk = kr * GRID_W
    nq = NA_HEADS * GRID_W
    qc = lax.broadcasted_iota(jnp.int32, (nq, nk), 0) % GRID_W
    kc = lax.broadcasted_iota(jnp.int32, (nq, nk), 1) % GRID_W
    cs = jnp.clip(qc - NA_WIN_COLS // 2, 0, GRID_W - NA_WIN_COLS)
    col_ok = (kc >= cs) & (kc < cs + NA_WIN_COLS)
    lane_head = lax.broadcasted_iota(jnp.int32, (1, w), 1) // HEAD_DIM
    head_masks = [lane_head == h for h in range(NA_HEADS)]
    own_head = (lax.broadcasted_iota(jnp.int32, (nq, w), 0) // GRID_W ==
                lax.broadcasted_iota(jnp.int32, (nq, w), 1) // HEAD_DIM)
    for j in range(NA_ROWS_PER_STEP):
        i = g * NA_ROWS_PER_STEP + j
        row_start = jnp.clip(i - kr // 2, 0, rows - kr)
        delta = i - row_start
        start = pl.multiple_of(row_start * GRID_W, GRID_W)
        q = q_ref[j * GRID_W:(j + 1) * GRID_W, :] * (HEAD_DIM ** -0.5)
        q_st = jnp.concatenate([jnp.where(m, q, jnp.zeros_like(q)) for m in head_masks], axis=0)
        kwin = k_ref[pl.ds(start, nk), :]
        vwin = v_ref[pl.ds(start, nk), :]
        s = jnp.where(col_ok, _dot(q_st, kwin, _NT) + bias_ref[delta], NEG_INF)
        m = jnp.max(s, axis=-1, keepdims=True)
        e = jnp.exp(s - m)
        inv = 1.0 / jnp.sum(e, axis=-1, keepdims=True)
        o_st = jnp.where(own_head, _dot(e.astype(BF16), vwin) * inv, 0.0)
        o_ref[j * GRID_W:(j + 1) * GRID_W, :] = _unstack_heads(o_st).astype(BF16)


def _na_call(p_na, bias, batch, seq):
    n = p_na.shape[0]
    rows = seq // GRID_W
    steps = rows // NA_ROWS_PER_STEP
    blk = NA_ROWS_PER_STEP * GRID_W
    return pl.pallas_call(
        functools.partial(_na_body, rows=rows),
        grid=(batch, steps),
        in_specs=[pl.BlockSpec((blk, NA_WIDTH), lambda b, i: (b * steps + i, 0)),
                  pl.BlockSpec((seq, NA_WIDTH), lambda b, i: (b, 1)),
                  pl.BlockSpec((seq, NA_WIDTH), lambda b, i: (b, 2)),
                  pl.BlockSpec(bias.shape, lambda b, i: (0, 0, 0))],
        out_specs=pl.BlockSpec((blk, NA_WIDTH), lambda b, i: (b * steps + i, 0)),
        out_shape=jax.ShapeDtypeStruct((n, NA_WIDTH), BF16),
        compiler_params=_cparams(2),
        name="na_attn",
    )(p_na, p_na, p_na, bias)


def _mlaproj_body(p_ref, qg_ref, kvg_ref, wqa_ref, wqb_ref, wk_ref, wvt_ref,
                  cq_ref, sq_ref, ck_ref, sk_ref, q_ref, k_ref, vt_ref):
    p = p_ref[...]
    cq = p[:, 0:MLA_Q_RANK]
    ckv = p[:, MLA_Q_RANK:MLA_Q_RANK + MLA_KV_RANK]
    kpe = p[:, MLA_Q_RANK + MLA_KV_RANK:MLA_Q_RANK + MLA_KV_RANK + LANES]
    kpe_rot = p[:, MLA_Q_RANK + MLA_KV_RANK + LANES:MLA_PAD]
    xq = cq * lax.rsqrt(jnp.mean(cq * cq, axis=-1, keepdims=True) + RMS_EPS) * qg_ref[...]
    xq = xq.astype(BF16)
    xkv = ckv * lax.rsqrt(jnp.mean(ckv * ckv, axis=-1, keepdims=True) + RMS_EPS) * kvg_ref[...]
    xkv = xkv.astype(BF16)
    qa = _dot(xq, wqa_ref[...])
    qb = _dot(xq, wqb_ref[...])
    scale = (MLA_NOPE_DIM + MLA_ROPE_DIM) ** -0.5 * LOG2_E
    kn = _dot(xkv, wk_ref[...])
    kpe_r = kpe * ck_ref[...] + kpe_rot * sk_ref[...]
    cq_t = cq_ref[...]
    sq_t = sq_ref[...]
    for h in range(MLA_HEADS):
        sl = slice(h * MLA_HEAD_PAD, (h + 1) * MLA_HEAD_PAD)
        q_ref[:, sl] = ((qa[:, sl] * cq_t + qb[:, sl] * sq_t) * scale).astype(BF16)
        k_ref[:, sl] = (kn[:, sl] + kpe_r).astype(BF16)
    rowi = lax.broadcasted_iota(jnp.int32, (MLA_HEADS * MLA_HEAD_PAD, 1), 0) % MLA_HEAD_PAD
    ones_rows = jnp.where(rowi >= MLA_V_DIM, 1.0, 0.0).astype(F32)
    vt_ref[...] = (_dot(wvt_ref[...], xkv, _NT) + ones_rows).astype(BF16)


def _mlaproj_call(p_mla, prm, tables, seq):
    n = p_mla.shape[0]
    qg, kvg, wqa, wqb, wk, wv = prm
    tps = seq // ROW_TILE
    hw = MLA_HEADS * MLA_HEAD_PAD

    def full(shape):
        return pl.BlockSpec(shape, lambda i: (0,) * len(shape))

    tab = pl.BlockSpec((ROW_TILE, LANES), lambda i: (i % tps, 0))
    return pl.pallas_call(
        _mlaproj_body,
        grid=(n // ROW_TILE,),
        in_specs=[pl.BlockSpec((ROW_TILE, MLA_PAD), lambda i: (i, 0)),
                  full(qg.shape), full(kvg.shape), full(wqa.shape), full(wqb.shape),
                  full(wk.shape), full(wv.shape), tab, tab, tab, tab],
        out_specs=[pl.BlockSpec((ROW_TILE, hw), lambda i: (i, 0)),
                   pl.BlockSpec((ROW_TILE, hw), lambda i: (i, 0)),
                   pl.BlockSpec((hw, ROW_TILE), lambda i: (0, i))],
        out_shape=[jax.ShapeDtypeStruct((n, hw), BF16),
                   jax.ShapeDtypeStruct((n, hw), BF16),
                   jax.ShapeDtypeStruct((hw, n), BF16)],
        compiler_params=_cparams(1),
        name="mla_proj",
    )(p_mla, qg, kvg, wqa, wqb, wk, wv, *tables)


def _column_max(x):
    while x.shape[0] > 8:
        rows = x.shape[0]
        fold = 8 if rows % 64 == 0 else rows // 8
        x = jnp.max(x.reshape(fold, rows // fold, x.shape[1]), axis=0)
    return jnp.max(x, axis=0, keepdims=True)


def _mlaattn_body(q_ref, k_ref, vt_ref, o_ref):
    seq = k_ref.shape[0]
    n_chunks = seq // MLA_KEY_CHUNK
    head = lambda h: slice(h * MLA_HEAD_PAD, (h + 1) * MLA_HEAD_PAD)
    keys = lambda c: slice(c * MLA_KEY_CHUNK, (c + 1) * MLA_KEY_CHUNK)

    items = [(t, h) for t in range(MLA_Q_PER_STEP) for h in range(MLA_HEADS_PER_STEP)]
    rows = lambda t: slice(t * Q_TILE, (t + 1) * Q_TILE)

    def scores(item, c):
        t, h = item
        return _dot(k_ref[keys(c), head(h)], q_ref[rows(t), head(h)], _NT)

    ni = len(items)
    outs = []
    st = {i: [scores(items[i], c) for c in range(n_chunks)] for i in range(min(2, ni))}
    col_max = {0: _column_max(functools.reduce(jnp.maximum, st[0]))}
    for i, (t, h) in enumerate(items):
        if i + 2 < ni:
            st[i + 2] = []
        ot = None
        for c in range(n_chunks):
            if i + 2 < ni:
                st[i + 2].append(scores(items[i + 2], c))
            et = jnp.exp2(st[i][c] - col_max[i]).astype(BF16)
            part = _dot(vt_ref[head(h), keys(c)], et)
            ot = part if ot is None else ot + part
        del st[i]
        if i + 1 < ni:
            col_max[i + 1] = _column_max(functools.reduce(jnp.maximum, st[i + 1]))
        outs.append(ot[0:MLA_V_DIM, :] / ot[MLA_V_DIM:MLA_V_DIM + 1, :])
        if h == MLA_HEADS_PER_STEP - 1:
            o_ref[rows(t), :] = jnp.concatenate(outs, axis=0).T.astype(BF16)
            outs = []


def _mlaattn_call(q, k, v, batch, seq):
    n = q.shape[0]
    q_rows = Q_TILE * MLA_Q_PER_STEP
    qt = seq // q_rows
    groups = MLA_HEADS // MLA_HEADS_PER_STEP
    wide = MLA_HEADS_PER_STEP * MLA_HEAD_PAD
    return pl.pallas_call(
        _mlaattn_body,
        grid=(batch, groups, qt),
        in_specs=[pl.BlockSpec((q_rows, wide), lambda b, hg, i: (b * qt + i, hg)),
                  pl.BlockSpec((seq, wide), lambda b, hg, i: (b, hg)),
                  pl.BlockSpec((wide, seq), lambda b, hg, i: (hg, b))],
        out_specs=pl.BlockSpec((q_rows, MLA_HEADS_PER_STEP * MLA_V_DIM),
                               lambda b, hg, i: (b * qt + i, hg)),
        out_shape=jax.ShapeDtypeStruct((n, MLA_WIDTH), BF16),
        compiler_params=_cparams(3),
        name="mla_attn",
    )(q, k, v)


def _outproj_body(rw_ref, na_ref, mla_ref, h_ref, w_ref, g_ref, b_ref, o_ref, *, alpha):
    mix = _dot(rw_ref[...], w_ref[0:RW_WIDTH, :])
    mix = mix + _dot(na_ref[...], w_ref[RW_WIDTH:RW_WIDTH + NA_WIDTH, :])
    mix = mix + _dot(mla_ref[...], w_ref[RW_WIDTH + NA_WIDTH:, :])
    o_ref[...] = _layer_norm(alpha * h_ref[...] + mix, g_ref[...], b_ref[...])


def _outproj_call(o_rw, o_na, o_mla, h, w, g, b, alpha):
    n = h.shape[0]
    row = lambda width: pl.BlockSpec((MM_ROW_TILE, width), lambda i: (i, 0))
    return pl.pallas_call(
        functools.partial(_outproj_body, alpha=alpha),
        grid=(n // MM_ROW_TILE,),
        in_specs=[row(RW_WIDTH), row(NA_WIDTH), row(MLA_WIDTH), row(D_MODEL),
                  pl.BlockSpec((D_MODEL, D_MODEL), lambda i: (0, 0)),
                  pl.BlockSpec((1, D_MODEL), lambda i: (0, 0)),
                  pl.BlockSpec((1, D_MODEL), lambda i: (0, 0))],
        out_specs=row(D_MODEL),
        out_shape=jax.ShapeDtypeStruct((n, D_MODEL), F32),
        compiler_params=_cparams(1),
        name="out_proj",
    )(o_rw, o_na, o_mla, h, w, g, b)


def _ffn_body(h_ref, wg_ref, wu_ref, wd_ref, g_ref, b_ref, o_ref, *, alpha):
    h = h_ref[...]
    hb = h.astype(BF16)
    acc = alpha * h
    for j in range(D_FF // FFN_TILE):
        cols = slice(j * FFN_TILE, (j + 1) * FFN_TILE)
        gt = _dot(hb, wg_ref[:, cols])
        up = _dot(hb, wu_ref[:, cols])
        act = (gt * _sigmoid(gt) * up).astype(BF16)
        acc = acc + _dot(act, wd_ref[cols, :])
    o_ref[...] = _layer_norm(acc, g_ref[...], b_ref[...])


def _ffn_call(h, wg, wu, wd, g, b, alpha):
    n = h.shape[0]
    resident = lambda shape: pl.BlockSpec(shape, lambda i: (0, 0), pipeline_mode=pl.Buffered(1))
    return pl.pallas_call(
        functools.partial(_ffn_body, alpha=alpha),
        grid=(n // MM_ROW_TILE,),
        in_specs=[pl.BlockSpec((MM_ROW_TILE, D_MODEL), lambda i: (i, 0)),
                  resident((D_MODEL, D_FF)), resident((D_MODEL, D_FF)), resident((D_FF, D_MODEL)),
                  pl.BlockSpec((1, D_MODEL), lambda i: (0, 0)),
                  pl.BlockSpec((1, D_MODEL), lambda i: (0, 0))],
        out_specs=pl.BlockSpec((MM_ROW_TILE, D_MODEL), lambda i: (i, 0)),
        out_shape=jax.ShapeDtypeStruct((n, D_MODEL), F32),
        compiler_params=_cparams(1),
        name="ffn",
    )(h, wg, wu, wd, g, b)


def _rope_rotate_cols(w_pe):
    half = MLA_ROPE_DIM // 2
    return jnp.concatenate([-w_pe[..., half:], w_pe[..., :half]], axis=-1)


def _prep_in_proj(w_in):
    d = w_in.shape[0]
    rw = w_in[:, :RW_IN]
    na = w_in[:, RW_IN:RW_IN + NA_IN]
    mla = w_in[:, RW_IN + NA_IN:]
    cq_ckv = mla[:, :MLA_Q_RANK + MLA_KV_RANK]
    kpe = mla[:, MLA_Q_RANK + MLA_KV_RANK:]
    z = lambda c: jnp.zeros((d, c), w_in.dtype)
    tail = LANES - MLA_NOPE_DIM - MLA_ROPE_DIM
    w = jnp.concatenate([rw, z(RW_PAD - RW_IN), na, cq_ckv,
                         z(MLA_NOPE_DIM), kpe, z(tail),
                         z(MLA_NOPE_DIM), _rope_rotate_cols(kpe), z(tail)], axis=1)
    return w.astype(BF16)


def _prep_lora(w2, row_offset):
    out = jnp.zeros((2, LANES, w2.shape[-1]), w2.dtype)
    rank = w2.shape[1]
    for d in range(2):
        out = out.at[d, row_offset + d * rank:row_offset + (d + 1) * rank].set(w2[d])
    return out.astype(BF16)


def _prep_mla_weights(w_q_b, w_kv_b):
    qd = MLA_NOPE_DIM + MLA_ROPE_DIM
    wq = w_q_b.reshape(MLA_Q_RANK, MLA_HEADS, qd)
    tail = MLA_HEAD_PAD - qd
    zq = lambda c: jnp.zeros((MLA_Q_RANK, MLA_HEADS, c), w_q_b.dtype)
    wqa = jnp.concatenate([wq, zq(tail)], axis=-1).reshape(MLA_Q_RANK, -1)
    wqb = jnp.concatenate([zq(MLA_NOPE_DIM), _rope_rotate_cols(wq[..., MLA_NOPE_DIM:]), zq(tail)],
                          axis=-1).reshape(MLA_Q_RANK, -1)
    wkv = w_kv_b.reshape(MLA_KV_RANK, MLA_HEADS, MLA_NOPE_DIM + MLA_V_DIM)
    zk = jnp.zeros((MLA_KV_RANK, MLA_HEADS, MLA_HEAD_PAD - MLA_NOPE_DIM), w_kv_b.dtype)
    wk = jnp.concatenate([wkv[..., :MLA_NOPE_DIM], zk], axis=-1).reshape(MLA_KV_RANK, -1)
    zv = jnp.zeros((MLA_KV_RANK, MLA_HEADS, MLA_HEAD_PAD - MLA_V_DIM), w_kv_b.dtype)
    wv = jnp.concatenate([wkv[..., MLA_NOPE_DIM:], zv], axis=-1).reshape(MLA_KV_RANK, -1)
    return wqa.astype(BF16), wqb.astype(BF16), wk.astype(BF16), wv.T.astype(BF16)


def _rope_tables(seq):
    inv_freq = jnp.power(ROPE_THETA, -jnp.arange(0, MLA_ROPE_DIM, 2, dtype=F32) / MLA_ROPE_DIM)
    ang = jnp.arange(seq, dtype=F32)[:, None] * inv_freq[None, :]
    cos2 = jnp.concatenate([jnp.cos(ang), jnp.cos(ang)], axis=-1)
    sin2 = jnp.concatenate([jnp.sin(ang), jnp.sin(ang)], axis=-1)
    tail = LANES - MLA_NOPE_DIM - MLA_ROPE_DIM
    one = jnp.ones((seq, MLA_NOPE_DIM), F32)
    z = lambda c: jnp.zeros((seq, c), F32)
    cq = jnp.concatenate([one, cos2, z(tail)], axis=-1)
    sq = jnp.concatenate([z(MLA_NOPE_DIM), sin2, z(tail)], axis=-1)
    ck = jnp.concatenate([z(MLA_NOPE_DIM), cos2, z(tail)], axis=-1)
    return cq, sq, ck, sq


def _na_bias_table(rpb, rows):
    kr = min(NA_WIN_ROWS, rows)
    heads, n_dr, n_dc = rpb.shape
    edge = GRID_W - 1 - (NA_WIN_COLS - 1)
    ext = jnp.concatenate([jnp.broadcast_to(rpb[..., :1], (heads, n_dr, edge)), rpb,
                           jnp.broadcast_to(rpb[..., -1:], (heads, n_dr, edge))], axis=-1)
    span = 2 * GRID_W - 1
    period = jnp.concatenate([ext, jnp.zeros((heads, n_dr, 1), ext.dtype)], axis=-1)
    tiled = jnp.tile(period, (1, 1, GRID_W))[..., :GRID_W * span]
    toep = tiled.reshape(heads, n_dr, GRID_W, span)[..., GRID_W - 1:]
    slabs = []
    for delta in range(kr):
        lo = NA_WIN_ROWS - 1 - delta
        s = toep[:, lo:lo + kr]
        slabs.append(jnp.transpose(s, (0, 2, 1, 3)).reshape(heads * GRID_W, kr * GRID_W))
    return jnp.stack(slabs, axis=0).astype(F32)


def kernel(x, ln_in_g, ln_in_b, w_in, rw_mu, rw_w0, rw_w2, rw_a0, rw_a2, rw_g2, rw_k_k, rw_k_a,
           rw_r_k, rw_gn_g, rw_gn_b, na_rpb, mla_q_norm_g, mla_w_q_b, mla_kv_norm_g, mla_w_kv_b,
           w_out, ln1_g, ln1_b, w_ffn_gate, w_ffn_up, w_ffn_down, ln2_g, ln2_b):
    batch, seq, d = x.shape
    depth = w_in.shape[0]
    assert d == D_MODEL and seq % ROW_TILE == 0 and seq % CHUNK == 0 and seq % GRID_W == 0
    assert seq // GRID_W >= NA_WIN_ROWS and D_FF % FFN_TILE == 0 and seq % (Q_TILE * MLA_Q_PER_STEP) == 0
    assert seq % MM_ROW_TILE == 0
    assert (seq // GRID_W) % NA_ROWS_PER_STEP == 0 and MLA_V_DIM * 2 == MLA_HEAD_PAD
    assert batch % RW_BATCH_PER_STEP == 0
    n = batch * seq
    alpha = float((2 * depth) ** 0.25)
    row = lambda a: a.reshape(1, -1)
    tables = _rope_tables(seq)

    h = x.reshape(n, d)
    for l in range(depth):
        mu = jnp.pad(rw_mu[l], ((0, 0), (0, RW_PAD - RW_IN)))
        w_l = _prep_in_proj(w_in[l])
        if l == 0:
            h, xs, p_na, p_mla = _inproj_call(h, w_l, mu, seq, ln=(row(ln_in_g), row(ln_in_b)))
        else:
            xs, p_na, p_mla = _inproj_call(h, w_l, mu, seq)

        a2 = _prep_lora(rw_a2[l], 2 * RW_DECAY_RANK)
        scan_prm = (rw_w0[l], _prep_lora(rw_w2[l], 0), rw_a0[l], a2,
                    row(rw_k_k[l]), row(rw_k_a[l]))
        y_f, y_b = _rw_call(xs.reshape(batch, seq, RW_PAD), scan_prm, batch, seq)
        y_f = y_f.reshape(n, RW_WIDTH)
        y_b = y_b.reshape(n, RW_WIDTH)
        g2 = jnp.pad(rw_g2[l], ((0, LANES - RW_GATE_RANK), (0, 0))).astype(BF16)
        out_prm = (rw_a0[l], a2, row(rw_k_a[l]), row(rw_r_k[l]),
                   row(rw_gn_g[l]), row(rw_gn_b[l]), g2)
        o_rw = _rwout_call(xs, y_f, y_b, out_prm)

        o_na = _na_call(p_na, _na_bias_table(na_rpb[l], seq // GRID_W), batch, seq)

        mla_prm = (row(mla_q_norm_g[l]), row(mla_kv_norm_g[l]),
                   *_prep_mla_weights(mla_w_q_b[l], mla_w_kv_b[l]))
        q, k, v = _mlaproj_call(p_mla, mla_prm, tables, seq)
        o_mla = _mlaattn_call(q, k, v, batch, seq)

        h = _outproj_call(o_rw, o_na, o_mla, h, w_out[l].astype(BF16),
                          row(ln1_g[l]), row(ln1_b[l]), alpha)
        h = _ffn_call(h, w_ffn_gate[l].astype(BF16), w_ffn_up[l].astype(BF16),
                      w_ffn_down[l].astype(BF16), row(ln2_g[l]), row(ln2_b[l]), alpha)
    return h.reshape(batch, seq, d)
```

```python
import functools

import jax
import jax.numpy as jnp
import numpy as np
from jax import lax
from jax.experimental import pallas as pl
from jax.experimental.pallas import tpu as pltpu

F32 = jnp.float32
BF16 = jnp.bfloat16

D_MODEL = 1024
HEAD_DIM = 64
GRID_W = 64
RW_HEADS = 4
RW_WIDTH = 256
RW_DECAY_RANK = 32
RW_ICLR_RANK = 32
RW_GATE_RANK = 64
RW_GN_EPS = 64e-5
NA_HEADS = 4
NA_WIDTH = 256
NA_WIN_ROWS = 8
NA_WIN_COLS = 16
MLA_HEADS = 8
MLA_Q_RANK = 256
MLA_KV_RANK = 128
MLA_NOPE_DIM = 64
MLA_ROPE_DIM = 32
MLA_V_DIM = 64
MLA_WIDTH = 512
ROPE_THETA = 10000.0
RW_IN = 960
NA_IN = 768
MLA_IN = 416
D_FF = 2816
LN_EPS = 1e-5
RMS_EPS = 1e-6
NEG_INF = -1e30
LOG2_E = 1.4426950408889634

LANES = 128
RW_PAD = 1024
MLA_PAD = 640
IN_PAD = RW_PAD + NA_IN + MLA_PAD
MLA_HEAD_PAD = 128
CHUNK = 64
RW_BATCH_PER_STEP = 4
RW_STAGGER_ROUNDS = 2
ROW_TILE = 512
MM_ROW_TILE = 1024
FFN_TILE = 256
Q_TILE = 256
MLA_HEADS_PER_STEP = 8
MLA_Q_PER_STEP = 2
MLA_KEY_CHUNK = 1024
NA_ROWS_PER_STEP = 8
VMEM_LIMIT = 56 * 1024 * 1024


def _cparams(n_axes):
    return pltpu.CompilerParams(dimension_semantics=("arbitrary",) * n_axes,
                                vmem_limit_bytes=VMEM_LIMIT)


def _split_bf16(x, parts):
    out = []
    rem = x
    for i in range(parts):
        p = rem.astype(BF16)
        out.append(p)
        if i + 1 < parts:
            rem = rem - p.astype(F32)
    return out


_NN = (((1,), (0,)), ((), ()))
_NT = (((1,), (1,)), ((), ()))


def _dot(a, b, dims=_NN):
    return lax.dot_general(a, b, dims, preferred_element_type=F32)


def _dot1(a, b, dims=_NN):
    return _dot(a.astype(BF16), b.astype(BF16), dims)


def _dot_exact_rhs(a, b_bf16, dims=_NN):
    a0, a1, a2 = _split_bf16(a, 3)
    return _dot(a0, b_bf16, dims) + (_dot(a1, b_bf16, dims) + _dot(a2, b_bf16, dims))


def _sigmoid(x):
    return 1.0 / (1.0 + jnp.exp(-x))


def _softplus(x):
    return jnp.maximum(x, 0.0) + jnp.log(1.0 + jnp.exp(-jnp.abs(x)))


def _layer_norm(x, g, b):
    mu = jnp.mean(x, axis=-1, keepdims=True)
    xc = x - mu
    var = jnp.mean(xc * xc, axis=-1, keepdims=True)
    return xc * lax.rsqrt(var + LN_EPS) * g + b


def _head_sum_matrix(width, head):
    r = lax.broadcasted_iota(jnp.int32, (width, width), 0) // head
    c = lax.broadcasted_iota(jnp.int32, (width, width), 1) // head
    return jnp.where(r == c, 1.0, 0.0).astype(BF16)


def _token_shift(x, prev_row, next_row, mu0, mu1):
    rows = x.shape[0]
    ridx = lax.broadcasted_iota(jnp.int32, x.shape, 0)
    xp = jnp.where(ridx == 0, prev_row, pltpu.roll(x, 1, 0))
    xn = jnp.where(ridx == rows - 1, next_row, pltpu.roll(x, rows - 1, 0))
    return x + mu0 * (xp - x) + mu1 * (xn - x)


def _inproj_body(*refs, tiles_per_seq, apply_ln):
    if apply_ln:
        x_ref, xp_ref, xn_ref, g_ref, b_ref, w_ref, mu_ref = refs[:7]
        mla_in = refs[7:17]
        h_ref, rw_ref, na_ref = refs[17:20]
        mla_out = refs[20:23]
        norm = lambda t: _layer_norm(t, g_ref[...], b_ref[...])
    else:
        x_ref, xp_ref, xn_ref, w_ref, mu_ref = refs[:5]
        mla_in = refs[5:15]
        rw_ref, na_ref = refs[15:17]
        mla_out = refs[17:20]
        norm = lambda t: t
    i = pl.program_id(0)
    h = norm(x_ref[...])
    if apply_ln:
        h_ref[...] = h
    hb = h.astype(BF16)
    w_rw = w_ref[:, 0:RW_PAD]
    p_rw = _dot(hb, w_rw)
    zero = jnp.zeros((1, RW_PAD), F32)
    p_prev = _dot(norm(xp_ref[...]).astype(BF16), w_rw)[7:8, :]
    p_next = _dot(norm(xn_ref[...]).astype(BF16), w_rw)[0:1, :]
    p_prev = jnp.where(i % tiles_per_seq == 0, zero, p_prev)
    p_next = jnp.where(i % tiles_per_seq == tiles_per_seq - 1, zero, p_next)
    rw_ref[...] = _token_shift(p_rw, p_prev, p_next, mu_ref[0:1, :], mu_ref[1:2, :])
    na_ref[...] = _dot(hb, w_ref[:, RW_PAD:RW_PAD + NA_IN]).astype(BF16)
    _mla_project(_dot(hb, w_ref[:, RW_PAD + NA_IN:IN_PAD]), mla_in, mla_out)


def _inproj_call(x, w, mu, mla_prm, tables, seq, ln=None):
    n = x.shape[0]
    t8 = ROW_TILE // 8
    last8 = n // 8 - 1
    tps = seq // ROW_TILE
    hw = MLA_HEADS * MLA_HEAD_PAD
    full = lambda a: pl.BlockSpec(a.shape, lambda i: (0,) * a.ndim)
    row = lambda width: pl.BlockSpec((ROW_TILE, width), lambda i: (i, 0))
    tab = pl.BlockSpec((ROW_TILE, LANES), lambda i: (i % tps, 0))
    in_specs = [row(D_MODEL),
                pl.BlockSpec((8, D_MODEL), lambda i: (jnp.maximum(i * t8 - 1, 0), 0)),
                pl.BlockSpec((8, D_MODEL), lambda i: (jnp.minimum((i + 1) * t8, last8), 0))]
    args = [x, x, x]
    out_specs = [row(RW_PAD), row(NA_IN), row(hw), row(hw),
                 pl.BlockSpec((hw, ROW_TILE), lambda i: (0, i))]
    out_shape = [jax.ShapeDtypeStruct((n, RW_PAD), F32),
                 jax.ShapeDtypeStruct((n, NA_IN), BF16),
                 jax.ShapeDtypeStruct((n, hw), BF16),
                 jax.ShapeDtypeStruct((n, hw), BF16),
                 jax.ShapeDtypeStruct((hw, n), BF16)]
    if ln is not None:
        in_specs += [full(ln[0]), full(ln[1])]
        args += list(ln)
        out_specs = [row(D_MODEL)] + out_specs
        out_shape = [jax.ShapeDtypeStruct((n, D_MODEL), F32)] + out_shape
    in_specs += [full(w), full(mu)] + [full(a) for a in mla_prm] + [tab] * len(tables)
    args += [w, mu, *mla_prm, *tables]
    return pl.pallas_call(
        functools.partial(_inproj_body, tiles_per_seq=tps, apply_ln=ln is not None),
        grid=(n // ROW_TILE,),
        in_specs=in_specs,
        out_specs=out_specs,
        out_shape=out_shape,
        compiler_params=_cparams(1),
        name="in_proj",
    )(*args)


def _stack_heads(x, head_masks):
    xb = x.astype(BF16)
    return jnp.concatenate([jnp.where(m, xb, jnp.zeros_like(xb)) for m in head_masks], axis=0)


def _unstack_heads(x):
    c = x.shape[0] // 4
    return (x[0:c] + x[c:2 * c]) + (x[2 * c:3 * c] + x[3 * c:4 * c])


def _neumann_inverse(l):
    n = l.shape[0]
    eye = jnp.where(lax.broadcasted_iota(jnp.int32, (n, n), 0) ==
                    lax.broadcasted_iota(jnp.int32, (n, n), 1), 1.0, 0.0).astype(F32)
    t = eye + l
    lb = l.astype(BF16)
    p = _dot(lb, lb)
    yield None
    steps = int(np.log2(CHUNK)) - 1
    for i in range(steps):
        pb = p.astype(BF16)
        if i + 1 < steps:
            prod = _dot(jnp.concatenate([t.astype(BF16), pb], axis=0), pb)
            t = t + prod[0:n]
            p = prod[n:2 * n]
        else:
            t = t + _dot(t.astype(BF16), pb)
        yield None
    return t


def _rw_direction(xs, d, prm, h_ref):
    w0_ref, w2_ref, a0_ref, a2_ref, kk_ref, ka_ref = prm
    c = CHUNK
    w = RW_WIDTH
    r = xs[:, 0:w]
    k = xs[:, w:2 * w]
    v = xs[:, 2 * w:3 * w]
    lora = xs[:, 3 * w:3 * w + LANES]
    yield None
    lw = _dot(jnp.tanh(lora).astype(BF16), w2_ref[d])
    la = _dot(lora.astype(BF16), a2_ref[d])
    log_w = -_softplus(-(w0_ref[d:d + 1, :] + lw)) - 0.5
    ldec = -jnp.exp(log_w)
    iclr = _sigmoid(a0_ref[d:d + 1, :] + la)
    yield None
    kk = k * kk_ref[...]
    ss = _dot_exact_rhs(kk * kk, _head_sum_matrix(w, HEAD_DIM))
    kk = kk / jnp.maximum(jnp.sqrt(ss), 1e-12)
    k_dir = k * (1.0 + (iclr - 1.0) * ka_ref[...])
    a = -kk
    b = kk * iclr
    yield None

    ti = lax.broadcasted_iota(jnp.int32, (c, c), 0)
    si = lax.broadcasted_iota(jnp.int32, (c, c), 1)
    cum = jnp.where(si <= ti, 1.0, 0.0).astype(BF16)
    l0, l1, l2 = _split_bf16(ldec, 3)
    cl_incl = _dot(cum, l0) + (_dot(cum, l1) + _dot(cum, l2))
    cl_excl = cl_incl - ldec
    tot = cl_incl[c - 1:c, :]
    yield None
    if d == 0:
        e_a = jnp.exp(cl_excl)
        e_d = jnp.exp(-cl_incl)
        e_r = jnp.exp(cl_incl)
        e_h = jnp.exp(tot - cl_incl)
    else:
        e_a = jnp.exp(tot - cl_incl)
        e_d = jnp.exp(cl_excl - tot)
        e_r = e_a
        e_h = jnp.exp(cl_excl)
    at = a * e_a
    bt = b * e_d
    kt = k_dir * e_d
    rt = r * e_r
    bh = b * e_h
    kh = k_dir * e_h
    yield None

    lane_head = lax.broadcasted_iota(jnp.int32, (1, w), 1) // HEAD_DIM
    head_masks = [lane_head == h for h in range(RW_HEADS)]
    xa = _stack_heads(at, head_masks)
    xr = _stack_heads(rt, head_masks)
    xb = _stack_heads(bt, head_masks)
    bk = jnp.concatenate([bt, kt], axis=0).astype(BF16)
    v_b = v.astype(BF16)

    n = RW_HEADS * c
    ri = lax.broadcasted_iota(jnp.int32, (n, n), 0)
    ci = lax.broadcasted_iota(jnp.int32, (n, n), 1)
    same = (ri // c) == (ci // c)
    tc = lax.broadcasted_iota(jnp.int32, (n, 2 * c), 0) % c
    sc = lax.broadcasted_iota(jnp.int32, (n, 2 * c), 1)
    second = sc >= c
    sc = sc % c
    if d == 0:
        strict = same & ((ci % c) < (ri % c))
        strict_c = sc < tc
        incl_c = sc <= tc
    else:
        strict = same & ((ci % c) > (ri % c))
        strict_c = sc > tc
        incl_c = strict_c
    yield None
    lab = jnp.where(strict, _dot(xa, xb, _NT), 0.0)
    pa = _dot(xa, bk, _NT)
    pr = _dot(xr, bk, _NT)
    lak_c = jnp.where(strict_c & second, pa, 0.0).astype(BF16)
    mr_c = jnp.where(incl_c, pr, 0.0).astype(BF16)
    yield None

    t = (yield from _neumann_inverse(lab)).astype(BF16)
    own = (lax.broadcasted_iota(jnp.int32, (n, w), 0) // c ==
           lax.broadcasted_iota(jnp.int32, (n, w), 1) // HEAD_DIM)
    vv = jnp.concatenate([v_b, v_b], axis=0)
    q = jnp.where(own, _dot(lak_c, vv), 0.0)
    ta = _dot(t, xa)
    yield None
    u0 = _dot(t, q.astype(BF16))
    ta_c = _unstack_heads(ta).astype(BF16)
    u0_c = _unstack_heads(u0)
    bk_t = jnp.concatenate([bh, kh], axis=0).T.astype(BF16)
    w_col = jnp.exp(jnp.sum(ldec.T, axis=1, keepdims=True))
    yield None

    h = h_ref[...]
    h_hi, h_lo = _split_bf16(h, 2)
    u_c = (_dot(ta_c, h_hi) + _dot(ta_c, h_lo)) + u0_c
    uv = jnp.concatenate([u_c.astype(BF16), v_b], axis=0)
    rt_b = rt.astype(BF16)
    y = (_dot(rt_b, h_hi) + _dot(rt_b, h_lo)) + _unstack_heads(jnp.where(own, _dot(mr_c, uv), 0.0))
    lane_k = lax.broadcasted_iota(jnp.int32, (w, w), 1)
    row_k = lax.broadcasted_iota(jnp.int32, (w, w), 0)
    bdm = (lane_k // HEAD_DIM) == (row_k // HEAD_DIM)
    h_ref[...] = w_col * h + jnp.where(bdm, _dot(bk_t, uv), 0.0)
    return y


def _run_interleaved(gens, delays):
    results = [None] * len(gens)
    live = list(range(len(gens)))
    rnd = 0
    while live:
        for i in list(live):
            if rnd < delays[i]:
                continue
            try:
                next(gens[i])
            except StopIteration as stop:
                results[i] = stop.value
                live.remove(i)
        rnd += 1
    return results


def _rw_body(xf_ref, xb_ref, w0_ref, w2_ref, a0_ref, a2_ref, kk_ref, ka_ref,
             yf_ref, yb_ref, hf_ref, hb_ref):
    ch = pl.program_id(1)

    @pl.when(ch == 0)
    def _():
        hf_ref[...] = jnp.zeros_like(hf_ref)
        hb_ref[...] = jnp.zeros_like(hb_ref)

    prm = (w0_ref, w2_ref, a0_ref, a2_ref, kk_ref, ka_ref)
    gens = []
    for bb in range(RW_BATCH_PER_STEP):
        gens.append(_rw_direction(xf_ref[bb], 0, prm, hf_ref.at[bb]))
        gens.append(_rw_direction(xb_ref[bb], 1, prm, hb_ref.at[bb]))
    delays = [RW_STAGGER_ROUNDS * (i // 2) for i in range(len(gens))]
    ys = _run_interleaved(gens, delays)
    for bb in range(RW_BATCH_PER_STEP):
        yf_ref[bb] = ys[2 * bb]
        yb_ref[bb] = ys[2 * bb + 1]


def _rw_call(xs, prm, batch, seq):
    nc = seq // CHUNK
    bps = RW_BATCH_PER_STEP

    def main_f(b, c):
        return (b, c, 0)

    def main_b(b, c):
        return (b, nc - 1 - c, 0)

    def full(shape):
        return pl.BlockSpec(shape, lambda b, c: (0,) * len(shape))

    w0, w2, a0, a2, kk, ka = prm
    return pl.pallas_call(
        _rw_body,
        grid=(batch // bps, nc),
        in_specs=[pl.BlockSpec((bps, CHUNK, RW_PAD), main_f),
                  pl.BlockSpec((bps, CHUNK, RW_PAD), main_b),
                  full(w0.shape), full(w2.shape), full(a0.shape),
                  full(a2.shape), full(kk.shape), full(ka.shape)],
        out_specs=[pl.BlockSpec((bps, CHUNK, RW_WIDTH), main_f),
                   pl.BlockSpec((bps, CHUNK, RW_WIDTH), main_b)],
        out_shape=[jax.ShapeDtypeStruct((batch, seq, RW_WIDTH), F32),
                   jax.ShapeDtypeStruct((batch, seq, RW_WIDTH), F32)],
        scratch_shapes=[pltpu.VMEM((bps, RW_WIDTH, RW_WIDTH), F32),
                        pltpu.VMEM((bps, RW_WIDTH, RW_WIDTH), F32)],
        compiler_params=_cparams(2),
        name="rwkv_scan",
    )(xs, xs, w0, w2, a0, a2, kk, ka)


def _rw_output(xs, y, a0_ref, a2_ref, ka_ref, rk_ref, gg_ref, gb_ref, g2_ref):
    w = RW_WIDTH
    r = xs[:, 0:w]
    k = xs[:, w:2 * w]
    v = xs[:, 2 * w:3 * w]
    lora = xs[:, 3 * w:3 * w + LANES]
    gblk = xs[:, 3 * w + LANES:3 * w + 2 * LANES]
    iclr0 = _sigmoid(a0_ref[0:1, :] + _dot(lora.astype(BF16), a2_ref[0]))
    k_dir0 = k * (1.0 + (iclr0 - 1.0) * ka_ref[...])
    gate = _dot(_sigmoid(gblk).astype(BF16), g2_ref[...])
    hsum = _head_sum_matrix(w, HEAD_DIM)
    mean = _dot_exact_rhs(y, hsum) * (1.0 / HEAD_DIM)
    yc = y - mean
    var = _dot_exact_rhs(yc * yc, hsum) * (1.0 / HEAD_DIM)
    yn = yc * lax.rsqrt(var + RW_GN_EPS) * gg_ref[...] + gb_ref[...]
    bonus = _dot_exact_rhs(r * k_dir0 * rk_ref[...], hsum) * v
    return ((yn + bonus) * gate).astype(BF16)


def _na_body(q_ref, k_ref, v_ref, bias_ref, o_ref, *, rows):
    g = pl.program_id(1)
    w = NA_WIDTH
    kr = NA_WIN_ROWS
    nk = kr * GRID_W
    nq = NA_HEADS * GRID_W
    qc = lax.broadcasted_iota(jnp.int32, (nq, nk), 0) % GRID_W
    kc = lax.broadcasted_iota(jnp.int32, (nq, nk), 1) % GRID_W
    cs = jnp.clip(qc - NA_WIN_COLS // 2, 0, GRID_W - NA_WIN_COLS)
    col_ok = (kc >= cs) & (kc < cs + NA_WIN_COLS)
    lane_head = lax.broadcasted_iota(jnp.int32, (1, w), 1) // HEAD_DIM
    head_masks = [lane_head == h for h in range(NA_HEADS)]
    own_head = (lax.broadcasted_iota(jnp.int32, (nq, w), 0) // GRID_W ==
                lax.broadcasted_iota(jnp.int32, (nq, w), 1) // HEAD_DIM)
    for j in range(NA_ROWS_PER_STEP):
        i = g * NA_ROWS_PER_STEP + j
        row_start = jnp.clip(i - kr // 2, 0, rows - kr)
        delta = i - row_start
        start = pl.multiple_of(row_start * GRID_W, GRID_W)
        q = q_ref[j * GRID_W:(j + 1) * GRID_W, :] * (HEAD_DIM ** -0.5)
        q_st = jnp.concatenate([jnp.where(m, q, jnp.zeros_like(q)) for m in head_masks], axis=0)
        kwin = k_ref[pl.ds(start, nk), :]
        vwin = v_ref[pl.ds(start, nk), :]
        s = jnp.where(col_ok, _dot(q_st, kwin, _NT) + bias_ref[delta], NEG_INF)
        m = jnp.max(s, axis=-1, keepdims=True)
        e = jnp.exp(s - m)
        inv = 1.0 / jnp.sum(e, axis=-1, keepdims=True)
        o_st = jnp.where(own_head, _dot(e.astype(BF16), vwin) * inv, 0.0)
        o_ref[j * GRID_W:(j + 1) * GRID_W, :] = _unstack_heads(o_st).astype(BF16)


def _na_call(p_na, bias, batch, seq):
    n = p_na.shape[0]
    rows = seq // GRID_W
    steps = rows // NA_ROWS_PER_STEP
    blk = NA_ROWS_PER_STEP * GRID_W
    return pl.pallas_call(
        functools.partial(_na_body, rows=rows),
        grid=(batch, steps),
        in_specs=[pl.BlockSpec((blk, NA_WIDTH), lambda b, i: (b * steps + i, 0)),
                  pl.BlockSpec((seq, NA_WIDTH), lambda b, i: (b, 1)),
                  pl.BlockSpec((seq, NA_WIDTH), lambda b, i: (b, 2)),
                  pl.BlockSpec(bias.shape, lambda b, i: (0, 0, 0))],
        out_specs=pl.BlockSpec((blk, NA_WIDTH), lambda b, i: (b * steps + i, 0)),
        out_shape=jax.ShapeDtypeStruct((n, NA_WIDTH), BF16),
        compiler_params=_cparams(2),
        name="na_attn",
    )(p_na, p_na, p_na, bias)


def _mla_project(p, in_refs, out_refs):
    qg_ref, kvg_ref, wqa_ref, wqb_ref, wk_ref, wvt_ref, cq_ref, sq_ref, ck_ref, sk_ref = in_refs
    q_ref, k_ref, vt_ref = out_refs
    cq = p[:, 0:MLA_Q_RANK]
    ckv = p[:, MLA_Q_RANK:MLA_Q_RANK + MLA_KV_RANK]
    kpe = p[:, MLA_Q_RANK + MLA_KV_RANK:MLA_Q_RANK + MLA_KV_RANK + LANES]
    kpe_rot = p[:, MLA_Q_RANK + MLA_KV_RANK + LANES:MLA_PAD]
    xq = cq * lax.rsqrt(jnp.mean(cq * cq, axis=-1, keepdims=True) + RMS_EPS) * qg_ref[...]
    xq = xq.astype(BF16)
    xkv = ckv * lax.rsqrt(jnp.mean(ckv * ckv, axis=-1, keepdims=True) + RMS_EPS) * kvg_ref[...]
    xkv = xkv.astype(BF16)
    qa = _dot(xq, wqa_ref[...])
    qb = _dot(xq, wqb_ref[...])
    scale = (MLA_NOPE_DIM + MLA_ROPE_DIM) ** -0.5 * LOG2_E
    kn = _dot(xkv, wk_ref[...])
    kpe_r = kpe * ck_ref[...] + kpe_rot * sk_ref[...]
    cq_t = cq_ref[...]
    sq_t = sq_ref[...]
    for h in range(MLA_HEADS):
        sl = slice(h * MLA_HEAD_PAD, (h + 1) * MLA_HEAD_PAD)
        q_ref[:, sl] = ((qa[:, sl] * cq_t + qb[:, sl] * sq_t) * scale).astype(BF16)
        k_ref[:, sl] = (kn[:, sl] + kpe_r).astype(BF16)
    rowi = lax.broadcasted_iota(jnp.int32, (MLA_HEADS * MLA_HEAD_PAD, 1), 0) % MLA_HEAD_PAD
    ones_rows = jnp.where(rowi >= MLA_V_DIM, 1.0, 0.0).astype(F32)
    vt_ref[...] = (_dot(wvt_ref[...], xkv, _NT) + ones_rows).astype(BF16)


def _column_max(x):
    while x.shape[0] > 8:
        rows = x.shape[0]
        fold = 8 if rows % 64 == 0 else rows // 8
        x = jnp.max(x.reshape(fold, rows // fold, x.shape[1]), axis=0)
    return jnp.max(x, axis=0, keepdims=True)


def _mlaattn_body(q_ref, k_ref, vt_ref, o_ref):
    seq = k_ref.shape[0]
    n_chunks = seq // MLA_KEY_CHUNK
    head = lambda h: slice(h * MLA_HEAD_PAD, (h + 1) * MLA_HEAD_PAD)
    keys = lambda c: slice(c * MLA_KEY_CHUNK, (c + 1) * MLA_KEY_CHUNK)

    items = [(t, h) for t in range(MLA_Q_PER_STEP) for h in range(MLA_HEADS_PER_STEP)]
    rows = lambda t: slice(t * Q_TILE, (t + 1) * Q_TILE)

    def scores(item, c):
        t, h = item
        return _dot(k_ref[keys(c), head(h)], q_ref[rows(t), head(h)], _NT)

    ni = len(items)
    outs = []
    st = {i: [scores(items[i], c) for c in range(n_chunks)] for i in range(min(2, ni))}
    col_max = {0: _column_max(functools.reduce(jnp.maximum, st[0]))}
    for i, (t, h) in enumerate(items):
        if i + 2 < ni:
            st[i + 2] = []
        ot = None
        for c in range(n_chunks):
            if i + 2 < ni:
                st[i + 2].append(scores(items[i + 2], c))
            et = jnp.exp2(st[i][c] - col_max[i]).astype(BF16)
            part = _dot(vt_ref[head(h), keys(c)], et)
            ot = part if ot is None else ot + part
        del st[i]
        if i + 1 < ni:
            col_max[i + 1] = _column_max(functools.reduce(jnp.maximum, st[i + 1]))
        outs.append(ot[0:MLA_V_DIM, :] / ot[MLA_V_DIM:MLA_V_DIM + 1, :])
        if h == MLA_HEADS_PER_STEP - 1:
            o_ref[rows(t), :] = jnp.concatenate(outs, axis=0).T.astype(BF16)
            outs = []


def _mlaattn_call(q, k, v, batch, seq):
    n = q.shape[0]
    q_rows = Q_TILE * MLA_Q_PER_STEP
    qt = seq // q_rows
    groups = MLA_HEADS // MLA_HEADS_PER_STEP
    wide = MLA_HEADS_PER_STEP * MLA_HEAD_PAD
    return pl.pallas_call(
        _mlaattn_body,
        grid=(batch, groups, qt),
        in_specs=[pl.BlockSpec((q_rows, wide), lambda b, hg, i: (b * qt + i, hg)),
                  pl.BlockSpec((seq, wide), lambda b, hg, i: (b, hg)),
                  pl.BlockSpec((wide, seq), lambda b, hg, i: (hg, b))],
        out_specs=pl.BlockSpec((q_rows, MLA_HEADS_PER_STEP * MLA_V_DIM),
                               lambda b, hg, i: (b * qt + i, hg)),
        out_shape=jax.ShapeDtypeStruct((n, MLA_WIDTH), BF16),
        compiler_params=_cparams(3),
        name="mla_attn",
    )(q, k, v)


def _outproj_body(xs_ref, yf_ref, yb_ref, na_ref, mla_ref, h_ref, a0_ref, a2_ref, ka_ref, rk_ref,
                  gg_ref, gb_ref, g2_ref, w_ref, g_ref, b_ref, o_ref, *, alpha):
    o_rw = _rw_output(xs_ref[...], yf_ref[...] + yb_ref[...],
                      a0_ref, a2_ref, ka_ref, rk_ref, gg_ref, gb_ref, g2_ref)
    mix = _dot(o_rw, w_ref[0:RW_WIDTH, :])
    mix = mix + _dot(na_ref[...], w_ref[RW_WIDTH:RW_WIDTH + NA_WIDTH, :])
    mix = mix + _dot(mla_ref[...], w_ref[RW_WIDTH + NA_WIDTH:, :])
    o_ref[...] = _layer_norm(alpha * h_ref[...] + mix, g_ref[...], b_ref[...])


def _outproj_call(xs, y_f, y_b, o_na, o_mla, h, rw_prm, w, g, b, alpha):
    n = h.shape[0]
    row = lambda width: pl.BlockSpec((ROW_TILE, width), lambda i: (i, 0))
    full = lambda a: pl.BlockSpec(a.shape, lambda i: (0,) * a.ndim)
    return pl.pallas_call(
        functools.partial(_outproj_body, alpha=alpha),
        grid=(n // ROW_TILE,),
        in_specs=[row(RW_PAD), row(RW_WIDTH), row(RW_WIDTH), row(NA_WIDTH), row(MLA_WIDTH),
                  row(D_MODEL)] + [full(a) for a in rw_prm] + [full(w), full(g), full(b)],
        out_specs=row(D_MODEL),
        out_shape=jax.ShapeDtypeStruct((n, D_MODEL), F32),
        compiler_params=_cparams(1),
        name="out_proj",
    )(xs, y_f, y_b, o_na, o_mla, h, *rw_prm, w, g, b)


def _ffn_body(h_ref, wg_ref, wu_ref, wd_ref, g_ref, b_ref, o_ref, *, alpha):
    h = h_ref[...]
    hb = h.astype(BF16)
    acc = alpha * h
    for j in range(D_FF // FFN_TILE):
        cols = slice(j * FFN_TILE, (j + 1) * FFN_TILE)
        gt = _dot(hb, wg_ref[:, cols])
        up = _dot(hb, wu_ref[:, cols])
        act = (gt * _sigmoid(gt) * up).astype(BF16)
        acc = acc + _dot(act, wd_ref[cols, :])
    o_ref[...] = _layer_norm(acc, g_ref[...], b_ref[...])


def _ffn_call(h, wg, wu, wd, g, b, alpha):
    n = h.shape[0]
    resident = lambda shape: pl.BlockSpec(shape, lambda i: (0, 0), pipeline_mode=pl.Buffered(1))
    return pl.pallas_call(
        functools.partial(_ffn_body, alpha=alpha),
        grid=(n // MM_ROW_TILE,),
        in_specs=[pl.BlockSpec((MM_ROW_TILE, D_MODEL), lambda i: (i, 0)),
                  resident((D_MODEL, D_FF)), resident((D_MODEL, D_FF)), resident((D_FF, D_MODEL)),
                  pl.BlockSpec((1, D_MODEL), lambda i: (0, 0)),
                  pl.BlockSpec((1, D_MODEL), lambda i: (0, 0))],
        out_specs=pl.BlockSpec((MM_ROW_TILE, D_MODEL), lambda i: (i, 0)),
        out_shape=jax.ShapeDtypeStruct((n, D_MODEL), F32),
        compiler_params=_cparams(1),
        name="ffn",
    )(h, wg, wu, wd, g, b)


def _rope_rotate_cols(w_pe):
    half = MLA_ROPE_DIM // 2
    return jnp.concatenate([-w_pe[..., half:], w_pe[..., :half]], axis=-1)


def _prep_in_proj(w_in):
    d = w_in.shape[0]
    rw = w_in[:, :RW_IN]
    na = w_in[:, RW_IN:RW_IN + NA_IN]
    mla = w_in[:, RW_IN + NA_IN:]
    cq_ckv = mla[:, :MLA_Q_RANK + MLA_KV_RANK]
    kpe = mla[:, MLA_Q_RANK + MLA_KV_RANK:]
    z = lambda c: jnp.zeros((d, c), w_in.dtype)
    tail = LANES - MLA_NOPE_DIM - MLA_ROPE_DIM
    w = jnp.concatenate([rw, z(RW_PAD - RW_IN), na, cq_ckv,
                         z(MLA_NOPE_DIM), kpe, z(tail),
                         z(MLA_NOPE_DIM), _rope_rotate_cols(kpe), z(tail)], axis=1)
    return w.astype(BF16)


def _prep_lora(w2, row_offset):
    out = jnp.zeros((2, LANES, w2.shape[-1]), w2.dtype)
    rank = w2.shape[1]
    for d in range(2):
        out = out.at[d, row_offset + d * rank:row_offset + (d + 1) * rank].set(w2[d])
    return out.astype(BF16)


def _prep_mla_weights(w_q_b, w_kv_b):
    qd = MLA_NOPE_DIM + MLA_ROPE_DIM
    wq = w_q_b.reshape(MLA_Q_RANK, MLA_HEADS, qd)
    tail = MLA_HEAD_PAD - qd
    zq = lambda c: jnp.zeros((MLA_Q_RANK, MLA_HEADS, c), w_q_b.dtype)
    wqa = jnp.concatenate([wq, zq(tail)], axis=-1).reshape(MLA_Q_RANK, -1)
    wqb = jnp.concatenate([zq(MLA_NOPE_DIM), _rope_rotate_cols(wq[..., MLA_NOPE_DIM:]), zq(tail)],
                          axis=-1).reshape(MLA_Q_RANK, -1)
    wkv = w_kv_b.reshape(MLA_KV_RANK, MLA_HEADS, MLA_NOPE_DIM + MLA_V_DIM)
    zk = jnp.zeros((MLA_KV_RANK, MLA_HEADS, MLA_HEAD_PAD - MLA_NOPE_DIM), w_kv_b.dtype)
    wk = jnp.concatenate([wkv[..., :MLA_NOPE_DIM], zk], axis=-1).reshape(MLA_KV_RANK, -1)
    zv = jnp.zeros((MLA_KV_RANK, MLA_HEADS, MLA_HEAD_PAD - MLA_V_DIM), w_kv_b.dtype)
    wv = jnp.concatenate([wkv[..., MLA_NOPE_DIM:], zv], axis=-1).reshape(MLA_KV_RANK, -1)
    return wqa.astype(BF16), wqb.astype(BF16), wk.astype(BF16), wv.T.astype(BF16)


def _rope_tables(seq):
    inv_freq = jnp.power(ROPE_THETA, -jnp.arange(0, MLA_ROPE_DIM, 2, dtype=F32) / MLA_ROPE_DIM)
    ang = jnp.arange(seq, dtype=F32)[:, None] * inv_freq[None, :]
    cos2 = jnp.concatenate([jnp.cos(ang), jnp.cos(ang)], axis=-1)
    sin2 = jnp.concatenate([jnp.sin(ang), jnp.sin(ang)], axis=-1)
    tail = LANES - MLA_NOPE_DIM - MLA_ROPE_DIM
    one = jnp.ones((seq, MLA_NOPE_DIM), F32)
    z = lambda c: jnp.zeros((seq, c), F32)
    cq = jnp.concatenate([one, cos2, z(tail)], axis=-1)
    sq = jnp.concatenate([z(MLA_NOPE_DIM), sin2, z(tail)], axis=-1)
    ck = jnp.concatenate([z(MLA_NOPE_DIM), cos2, z(tail)], axis=-1)
    return cq, sq, ck, sq


def _na_bias_table(rpb, rows):
    kr = min(NA_WIN_ROWS, rows)
    heads, n_dr, n_dc = rpb.shape
    edge = GRID_W - 1 - (NA_WIN_COLS - 1)
    ext = jnp.concatenate([jnp.broadcast_to(rpb[..., :1], (heads, n_dr, edge)), rpb,
                           jnp.broadcast_to(rpb[..., -1:], (heads, n_dr, edge))], axis=-1)
    span = 2 * GRID_W - 1
    period = jnp.concatenate([ext, jnp.zeros((heads, n_dr, 1), ext.dtype)], axis=-1)
    tiled = jnp.tile(period, (1, 1, GRID_W))[..., :GRID_W * span]
    toep = tiled.reshape(heads, n_dr, GRID_W, span)[..., GRID_W - 1:]
    slabs = []
    for delta in range(kr):
        lo = NA_WIN_ROWS - 1 - delta
        s = toep[:, lo:lo + kr]
        slabs.append(jnp.transpose(s, (0, 2, 1, 3)).reshape(heads * GRID_W, kr * GRID_W))
    return jnp.stack(slabs, axis=0).astype(F32)


def kernel(x, ln_in_g, ln_in_b, w_in, rw_mu, rw_w0, rw_w2, rw_a0, rw_a2, rw_g2, rw_k_k, rw_k_a,
           rw_r_k, rw_gn_g, rw_gn_b, na_rpb, mla_q_norm_g, mla_w_q_b, mla_kv_norm_g, mla_w_kv_b,
           w_out, ln1_g, ln1_b, w_ffn_gate, w_ffn_up, w_ffn_down, ln2_g, ln2_b):
    batch, seq, d = x.shape
    depth = w_in.shape[0]
    assert d == D_MODEL and seq % ROW_TILE == 0 and seq % CHUNK == 0 and seq % GRID_W == 0
    assert seq // GRID_W >= NA_WIN_ROWS and D_FF % FFN_TILE == 0 and seq % (Q_TILE * MLA_Q_PER_STEP) == 0
    assert seq % MM_ROW_TILE == 0
    assert (seq // GRID_W) % NA_ROWS_PER_STEP == 0 and MLA_V_DIM * 2 == MLA_HEAD_PAD
    assert batch % RW_BATCH_PER_STEP == 0
    n = batch * seq
    alpha = float((2 * depth) ** 0.25)
    row = lambda a: a.reshape(1, -1)
    tables = _rope_tables(seq)

    h = x.reshape(n, d)
    for l in range(depth):
        mu = jnp.pad(rw_mu[l], ((0, 0), (0, RW_PAD - RW_IN)))
        w_l = _prep_in_proj(w_in[l])
        mla_prm = (row(mla_q_norm_g[l]), row(mla_kv_norm_g[l]),
                   *_prep_mla_weights(mla_w_q_b[l], mla_w_kv_b[l]))
        if l == 0:
            h, xs, p_na, q, k, v = _inproj_call(h, w_l, mu, mla_prm, tables, seq,
                                                ln=(row(ln_in_g), row(ln_in_b)))
        else:
            xs, p_na, q, k, v = _inproj_call(h, w_l, mu, mla_prm, tables, seq)

        a2 = _prep_lora(rw_a2[l], 2 * RW_DECAY_RANK)
        scan_prm = (rw_w0[l], _prep_lora(rw_w2[l], 0), rw_a0[l], a2,
                    row(rw_k_k[l]), row(rw_k_a[l]))
        y_f, y_b = _rw_call(xs.reshape(batch, seq, RW_PAD), scan_prm, batch, seq)
        y_f = y_f.reshape(n, RW_WIDTH)
        y_b = y_b.reshape(n, RW_WIDTH)
        g2 = jnp.pad(rw_g2[l], ((0, LANES - RW_GATE_RANK), (0, 0))).astype(BF16)
        out_prm = (rw_a0[l], a2, row(rw_k_a[l]), row(rw_r_k[l]),
                   row(rw_gn_g[l]), row(rw_gn_b[l]), g2)

        o_na = _na_call(p_na, _na_bias_table(na_rpb[l], seq // GRID_W), batch, seq)

        o_mla = _mlaattn_call(q, k, v, batch, seq)

        h = _outproj_call(xs, y_f, y_b, o_na, o_mla, h, out_prm, w_out[l].astype(BF16),
                          row(ln1_g[l]), row(ln1_b[l]), alpha)
        h = _ffn_call(h, w_ffn_gate[l].astype(BF16), w_ffn_up[l].astype(BF16),
                      w_ffn_down[l].astype(BF16), row(ln2_g[l]), row(ln2_b[l]), alpha)
    return h.reshape(batch, seq, d)
```

```python
import functools

import jax
import jax.numpy as jnp
import numpy as np
from jax import lax
from jax.experimental import pallas as pl
from jax.experimental.pallas import tpu as pltpu

F32 = jnp.float32
BF16 = jnp.bfloat16

D_MODEL = 1024
HEAD_DIM = 64
GRID_W = 64
RW_HEADS = 4
RW_WIDTH = 256
RW_DECAY_RANK = 32
RW_ICLR_RANK = 32
RW_GATE_RANK = 64
RW_GN_EPS = 64e-5
NA_HEADS = 4
NA_WIDTH = 256
NA_WIN_ROWS = 8
NA_WIN_COLS = 16
MLA_HEADS = 8
MLA_Q_RANK = 256
MLA_KV_RANK = 128
MLA_NOPE_DIM = 64
MLA_ROPE_DIM = 32
MLA_V_DIM = 64
MLA_WIDTH = 512
ROPE_THETA = 10000.0
RW_IN = 960
NA_IN = 768
MLA_IN = 416
D_FF = 2816
LN_EPS = 1e-5
RMS_EPS = 1e-6
NEG_INF = -1e30
LOG2_E = 1.4426950408889634

LANES = 128
RW_PAD = 1024
MLA_PAD = 640
IN_PAD = RW_PAD + NA_IN + MLA_PAD
MLA_HEAD_PAD = 128
CHUNK = 64
RW_BATCH_PER_STEP = 4
RW_STAGGER_ROUNDS = 2
ROW_TILE = 512
MM_ROW_TILE = 1024
FFN_TILE = 256
Q_TILE = 256
MLA_HEADS_PER_STEP = 8
MLA_Q_PER_STEP = 2
MLA_KEY_CHUNK = 1024
NA_ROWS_PER_STEP = 8
VMEM_LIMIT = 56 * 1024 * 1024


def _cparams(n_axes):
    return pltpu.CompilerParams(dimension_semantics=("arbitrary",) * n_axes,
                                vmem_limit_bytes=VMEM_LIMIT)


def _split_bf16(x, parts):
    out = []
    rem = x
    for i in range(parts):
        p = rem.astype(BF16)
        out.append(p)
        if i + 1 < parts:
            rem = rem - p.astype(F32)
    return out


_NN = (((1,), (0,)), ((), ()))
_NT = (((1,), (1,)), ((), ()))


def _dot(a, b, dims=_NN):
    return lax.dot_general(a, b, dims, preferred_element_type=F32)


def _dot1(a, b, dims=_NN):
    return _dot(a.astype(BF16), b.astype(BF16), dims)


def _dot_exact_rhs(a, b_bf16, dims=_NN):
    a0, a1, a2 = _split_bf16(a, 3)
    return _dot(a0, b_bf16, dims) + (_dot(a1, b_bf16, dims) + _dot(a2, b_bf16, dims))


def _sigmoid(x):
    return 1.0 / (1.0 + jnp.exp(-x))


def _softplus(x):
    return jnp.maximum(x, 0.0) + jnp.log(1.0 + jnp.exp(-jnp.abs(x)))


def _layer_norm(x, g, b):
    mu = jnp.mean(x, axis=-1, keepdims=True)
    xc = x - mu
    var = jnp.mean(xc * xc, axis=-1, keepdims=True)
    return xc * lax.rsqrt(var + LN_EPS) * g + b


def _head_sum_matrix(width, head):
    r = lax.broadcasted_iota(jnp.int32, (width, width), 0) // head
    c = lax.broadcasted_iota(jnp.int32, (width, width), 1) // head
    return jnp.where(r == c, 1.0, 0.0).astype(BF16)


def _token_shift(x, prev_row, next_row, mu0, mu1):
    rows = x.shape[0]
    ridx = lax.broadcasted_iota(jnp.int32, x.shape, 0)
    xp = jnp.where(ridx == 0, prev_row, pltpu.roll(x, 1, 0))
    xn = jnp.where(ridx == rows - 1, next_row, pltpu.roll(x, rows - 1, 0))
    return x + mu0 * (xp - x) + mu1 * (xn - x)


def _inproj_body(*refs, tiles_per_seq, apply_ln):
    if apply_ln:
        x_ref, xp_ref, xn_ref, g_ref, b_ref, w_ref, mu_ref = refs[:7]
        mla_in = refs[7:17]
        h_ref, rw_ref, na_ref = refs[17:20]
        mla_out = refs[20:23]
        norm = lambda t: _layer_norm(t, g_ref[...], b_ref[...])
    else:
        x_ref, xp_ref, xn_ref, w_ref, mu_ref = refs[:5]
        mla_in = refs[5:15]
        rw_ref, na_ref = refs[15:17]
        mla_out = refs[17:20]
        norm = lambda t: t
    i = pl.program_id(0)
    h = norm(x_ref[...])
    if apply_ln:
        h_ref[...] = h
    hb = h.astype(BF16)
    w_rw = w_ref[:, 0:RW_PAD]
    p_rw = _dot(hb, w_rw)
    zero = jnp.zeros((1, RW_PAD), F32)
    p_prev = _dot(norm(xp_ref[...]).astype(BF16), w_rw)[7:8, :]
    p_next = _dot(norm(xn_ref[...]).astype(BF16), w_rw)[0:1, :]
    p_prev = jnp.where(i % tiles_per_seq == 0, zero, p_prev)
    p_next = jnp.where(i % tiles_per_seq == tiles_per_seq - 1, zero, p_next)
    rw_ref[...] = _token_shift(p_rw, p_prev, p_next, mu_ref[0:1, :], mu_ref[1:2, :])
    na_ref[...] = _dot(hb, w_ref[:, RW_PAD:RW_PAD + NA_IN]).astype(BF16)
    _mla_project(_dot(hb, w_ref[:, RW_PAD + NA_IN:IN_PAD]), mla_in, mla_out)


def _inproj_call(x, w, mu, mla_prm, tables, seq, ln=None):
    n = x.shape[0]
    t8 = ROW_TILE // 8
    last8 = n // 8 - 1
    tps = seq // ROW_TILE
    hw = MLA_HEADS * MLA_HEAD_PAD
    full = lambda a: pl.BlockSpec(a.shape, lambda i: (0,) * a.ndim)
    row = lambda width: pl.BlockSpec((ROW_TILE, width), lambda i: (i, 0))
    tab = pl.BlockSpec((ROW_TILE, LANES), lambda i: (i % tps, 0))
    in_specs = [row(D_MODEL),
                pl.BlockSpec((8, D_MODEL), lambda i: (jnp.maximum(i * t8 - 1, 0), 0)),
                pl.BlockSpec((8, D_MODEL), lambda i: (jnp.minimum((i + 1) * t8, last8), 0))]
    args = [x, x, x]
    out_specs = [row(RW_PAD), row(NA_IN), row(hw), row(hw),
                 pl.BlockSpec((hw, ROW_TILE), lambda i: (0, i))]
    out_shape = [jax.ShapeDtypeStruct((n, RW_PAD), F32),
                 jax.ShapeDtypeStruct((n, NA_IN), BF16),
                 jax.ShapeDtypeStruct((n, hw), BF16),
                 jax.ShapeDtypeStruct((n, hw), BF16),
                 jax.ShapeDtypeStruct((hw, n), BF16)]
    if ln is not None:
        in_specs += [full(ln[0]), full(ln[1])]
        args += list(ln)
        out_specs = [row(D_MODEL)] + out_specs
        out_shape = [jax.ShapeDtypeStruct((n, D_MODEL), F32)] + out_shape
    in_specs += [full(w), full(mu)] + [full(a) for a in mla_prm] + [tab] * len(tables)
    args += [w, mu, *mla_prm, *tables]
    return pl.pallas_call(
        functools.partial(_inproj_body, tiles_per_seq=tps, apply_ln=ln is not None),
        grid=(n // ROW_TILE,),
        in_specs=in_specs,
        out_specs=out_specs,
        out_shape=out_shape,
        compiler_params=_cparams(1),
        name="in_proj",
    )(*args)


def _unstack_heads(x):
    c = x.shape[0] // 4
    return (x[0:c] + x[c:2 * c]) + (x[2 * c:3 * c] + x[3 * c:4 * c])


def _neumann_inverse(ls):
    n = ls[0].shape[0]
    eye = jnp.where(lax.broadcasted_iota(jnp.int32, (n, n), 0) ==
                    lax.broadcasted_iota(jnp.int32, (n, n), 1), 1.0, 0.0).astype(F32)
    ts = [eye + l for l in ls]
    ps = []
    for l in ls:
        lb = l.astype(BF16)
        ps.append(_dot(lb, lb))
    yield None
    steps = int(np.log2(CHUNK)) - 1
    for i in range(steps):
        for j in range(len(ls)):
            pb = ps[j].astype(BF16)
            if i + 1 < steps:
                prod = _dot(jnp.concatenate([ts[j].astype(BF16), pb], axis=0), pb)
                ts[j] = ts[j] + prod[0:n]
                ps[j] = prod[n:2 * n]
            else:
                ts[j] = ts[j] + _dot(ts[j].astype(BF16), pb)
        yield None
    return ts


def _rw_direction(xs, d, prm, h_ref):
    w0_ref, w2_ref, a0_ref, a2_ref, kk_ref, ka_ref = prm
    c = CHUNK
    w = RW_WIDTH
    r = xs[:, 0:w]
    k = xs[:, w:2 * w]
    v = xs[:, 2 * w:3 * w]
    lora = xs[:, 3 * w:3 * w + LANES]
    yield None
    lw = _dot(jnp.tanh(lora).astype(BF16), w2_ref[d])
    la = _dot(lora.astype(BF16), a2_ref[d])
    log_w = -_softplus(-(w0_ref[d:d + 1, :] + lw)) - 0.5
    ldec = -jnp.exp(log_w)
    iclr = _sigmoid(a0_ref[d:d + 1, :] + la)
    yield None
    kk = k * kk_ref[...]
    ss = _dot_exact_rhs(kk * kk, _head_sum_matrix(w, HEAD_DIM))
    kk = kk / jnp.maximum(jnp.sqrt(ss), 1e-12)
    k_dir = k * (1.0 + (iclr - 1.0) * ka_ref[...])
    a = -kk
    b = kk * iclr
    yield None

    ti = lax.broadcasted_iota(jnp.int32, (c, c), 0)
    si = lax.broadcasted_iota(jnp.int32, (c, c), 1)
    cum = jnp.where(si <= ti, 1.0, 0.0).astype(BF16)
    l0, l1, l2 = _split_bf16(ldec, 3)
    cl_incl = _dot(cum, l0) + (_dot(cum, l1) + _dot(cum, l2))
    cl_excl = cl_incl - ldec
    tot = cl_incl[c - 1:c, :]
    yield None
    if d == 0:
        e_a = jnp.exp(cl_excl)
        e_d = jnp.exp(-cl_incl)
        e_r = jnp.exp(cl_incl)
        e_h = jnp.exp(tot - cl_incl)
    else:
        e_a = jnp.exp(tot - cl_incl)
        e_d = jnp.exp(cl_excl - tot)
        e_r = e_a
        e_h = jnp.exp(cl_excl)
    at = a * e_a
    bt = b * e_d
    kt = k_dir * e_d
    rt = r * e_r
    bh = b * e_h
    kh = k_dir * e_h
    yield None

    pairs = RW_HEADS // 2
    grp = 2 * HEAD_DIM
    n2 = 2 * c
    lanes = lambda p: slice(p * grp, (p + 1) * grp)
    first_head = lax.broadcasted_iota(jnp.int32, (1, grp), 1) < HEAD_DIM

    def stack_pair(x, p):
        xb = x[:, lanes(p)].astype(BF16)
        z = jnp.zeros_like(xb)
        return jnp.concatenate([jnp.where(first_head, xb, z), jnp.where(first_head, z, xb)], axis=0)

    fold = lambda m: m[0:c] + m[c:n2]
    bk = jnp.concatenate([bt, kt], axis=0).astype(BF16)
    v_b = v.astype(BF16)

    ri = lax.broadcasted_iota(jnp.int32, (n2, n2), 0)
    ci = lax.broadcasted_iota(jnp.int32, (n2, n2), 1)
    same = (ri // c) == (ci // c)
    tc = ri % c
    sc = ci % c
    second = ci >= c
    own = (ri // c) == (ci // HEAD_DIM)
    if d == 0:
        strict = same & (sc < tc)
        strict_c = sc < tc
        incl_c = sc <= tc
    else:
        strict = same & (sc > tc)
        strict_c = sc > tc
        incl_c = strict_c
    yield None
    xa, labs, lak_c, mr_c = [], [], [], []
    for p in range(pairs):
        xa_p = stack_pair(at, p)
        bk_p = bk[:, lanes(p)]
        labs.append(jnp.where(strict, _dot(xa_p, stack_pair(bt, p), _NT), 0.0))
        pa = _dot(xa_p, bk_p, _NT)
        pr = _dot(stack_pair(rt, p), bk_p, _NT)
        xa.append(xa_p)
        lak_c.append(jnp.where(strict_c & second, pa, 0.0).astype(BF16))
        mr_c.append(jnp.where(incl_c, pr, 0.0).astype(BF16))
    yield None

    ts = yield from _neumann_inverse(labs)
    ts = [t.astype(BF16) for t in ts]
    qs, tas = [], []
    for p in range(pairs):
        vv = jnp.concatenate([v_b[:, lanes(p)]] * 2, axis=0)
        qs.append(jnp.where(own, _dot(lak_c[p], vv), 0.0))
        tas.append(_dot(ts[p], xa[p]))
    yield None
    u0s = [_dot(ts[p], qs[p].astype(BF16)) for p in range(pairs)]
    ta_c = jnp.concatenate([fold(m) for m in tas], axis=1).astype(BF16)
    u0_c = jnp.concatenate([fold(m) for m in u0s], axis=1)
    bk_t = jnp.concatenate([bh, kh], axis=0).T.astype(BF16)
    w_col = jnp.exp(jnp.sum(ldec.T, axis=1, keepdims=True))
    yield None

    h = h_ref[...]
    h_hi, h_lo = _split_bf16(h, 2)
    u_c = (_dot(ta_c, h_hi) + _dot(ta_c, h_lo)) + u0_c
    uv = jnp.concatenate([u_c.astype(BF16), v_b], axis=0)
    rt_b = rt.astype(BF16)
    intra = jnp.concatenate(
        [fold(jnp.where(own, _dot(mr_c[p], uv[:, lanes(p)]), 0.0)) for p in range(pairs)], axis=1)
    y = (_dot(rt_b, h_hi) + _dot(rt_b, h_lo)) + intra
    lane_k = lax.broadcasted_iota(jnp.int32, (w, w), 1)
    row_k = lax.broadcasted_iota(jnp.int32, (w, w), 0)
    bdm = (lane_k // HEAD_DIM) == (row_k // HEAD_DIM)
    h_ref[...] = w_col * h + jnp.where(bdm, _dot(bk_t, uv), 0.0)
    return y


def _run_interleaved(gens, delays):
    results = [None] * len(gens)
    live = list(range(len(gens)))
    rnd = 0
    while live:
        for i in list(live):
            if rnd < delays[i]:
                continue
            try:
                next(gens[i])
            except StopIteration as stop:
                results[i] = stop.value
                live.remove(i)
        rnd += 1
    return results


def _rw_body(xf_ref, xb_ref, w0_ref, w2_ref, a0_ref, a2_ref, kk_ref, ka_ref,
             yf_ref, yb_ref, hf_ref, hb_ref):
    ch = pl.program_id(1)

    @pl.when(ch == 0)
    def _():
        hf_ref[...] = jnp.zeros_like(hf_ref)
        hb_ref[...] = jnp.zeros_like(hb_ref)

    prm = (w0_ref, w2_ref, a0_ref, a2_ref, kk_ref, ka_ref)
    gens = []
    for bb in range(RW_BATCH_PER_STEP):
        gens.append(_rw_direction(xf_ref[bb], 0, prm, hf_ref.at[bb]))
        gens.append(_rw_direction(xb_ref[bb], 1, prm, hb_ref.at[bb]))
    delays = [RW_STAGGER_ROUNDS * (i // 2) for i in range(len(gens))]
    ys = _run_interleaved(gens, delays)
    for bb in range(RW_BATCH_PER_STEP):
        yf_ref[bb] = ys[2 * bb]
        yb_ref[bb] = ys[2 * bb + 1]


def _rw_call(xs, prm, batch, seq):
    nc = seq // CHUNK
    bps = RW_BATCH_PER_STEP

    def main_f(b, c):
        return (b, c, 0)

    def main_b(b, c):
        return (b, nc - 1 - c, 0)

    def full(shape):
        return pl.BlockSpec(shape, lambda b, c: (0,) * len(shape))

    w0, w2, a0, a2, kk, ka = prm
    return pl.pallas_call(
        _rw_body,
        grid=(batch // bps, nc),
        in_specs=[pl.BlockSpec((bps, CHUNK, RW_PAD), main_f),
                  pl.BlockSpec((bps, CHUNK, RW_PAD), main_b),
                  full(w0.shape), full(w2.shape), full(a0.shape),
                  full(a2.shape), full(kk.shape), full(ka.shape)],
        out_specs=[pl.BlockSpec((bps, CHUNK, RW_WIDTH), main_f),
                   pl.BlockSpec((bps, CHUNK, RW_WIDTH), main_b)],
        out_shape=[jax.ShapeDtypeStruct((batch, seq, RW_WIDTH), F32),
                   jax.ShapeDtypeStruct((batch, seq, RW_WIDTH), F32)],
        scratch_shapes=[pltpu.VMEM((bps, RW_WIDTH, RW_WIDTH), F32),
                        pltpu.VMEM((bps, RW_WIDTH, RW_WIDTH), F32)],
        compiler_params=_cparams(2),
        name="rwkv_scan",
    )(xs, xs, w0, w2, a0, a2, kk, ka)


def _rw_output(xs, y, a0_ref, a2_ref, ka_ref, rk_ref, gg_ref, gb_ref, g2_ref):
    w = RW_WIDTH
    r = xs[:, 0:w]
    k = xs[:, w:2 * w]
    v = xs[:, 2 * w:3 * w]
    lora = xs[:, 3 * w:3 * w + LANES]
    gblk = xs[:, 3 * w + LANES:3 * w + 2 * LANES]
    iclr0 = _sigmoid(a0_ref[0:1, :] + _dot(lora.astype(BF16), a2_ref[0]))
    k_dir0 = k * (1.0 + (iclr0 - 1.0) * ka_ref[...])
    gate = _dot(_sigmoid(gblk).astype(BF16), g2_ref[...])
    hsum = _head_sum_matrix(w, HEAD_DIM)
    mean = _dot_exact_rhs(y, hsum) * (1.0 / HEAD_DIM)
    yc = y - mean
    var = _dot_exact_rhs(yc * yc, hsum) * (1.0 / HEAD_DIM)
    yn = yc * lax.rsqrt(var + RW_GN_EPS) * gg_ref[...] + gb_ref[...]
    bonus = _dot_exact_rhs(r * k_dir0 * rk_ref[...], hsum) * v
    return ((yn + bonus) * gate).astype(BF16)


def _na_body(q_ref, k_ref, v_ref, bias_ref, o_ref, *, rows):
    g = pl.program_id(1)
    w = NA_WIDTH
    kr = NA_WIN_ROWS
    nk = kr * GRID_W
    nq = NA_HEADS * GRID_W
    qc = lax.broadcasted_iota(jnp.int32, (nq, nk), 0) % GRID_W
    kc = lax.broadcasted_iota(jnp.int32, (nq, nk), 1) % GRID_W
    cs = jnp.clip(qc - NA_WIN_COLS // 2, 0, GRID_W - NA_WIN_COLS)
    col_ok = (kc >= cs) & (kc < cs + NA_WIN_COLS)
    lane_head = lax.broadcasted_iota(jnp.int32, (1, w), 1) // HEAD_DIM
    head_masks = [lane_head == h for h in range(NA_HEADS)]
    own_head = (lax.broadcasted_iota(jnp.int32, (nq, w), 0) // GRID_W ==
                lax.broadcasted_iota(jnp.int32, (nq, w), 1) // HEAD_DIM)
    for j in range(NA_ROWS_PER_STEP):
        i = g * NA_ROWS_PER_STEP + j
        row_start = jnp.clip(i - kr // 2, 0, rows - kr)
        delta = i - row_start
        start = pl.multiple_of(row_start * GRID_W, GRID_W)
        q = q_ref[j * GRID_W:(j + 1) * GRID_W, :] * (HEAD_DIM ** -0.5)
        q_st = jnp.concatenate([jnp.where(m, q, jnp.zeros_like(q)) for m in head_masks], axis=0)
        kwin = k_ref[pl.ds(start, nk), :]
        vwin = v_ref[pl.ds(start, nk), :]
        s = jnp.where(col_ok, _dot(q_st, kwin, _NT) + bias_ref[delta], NEG_INF)
        m = jnp.max(s, axis=-1, keepdims=True)
        e = jnp.exp(s - m)
        inv = 1.0 / jnp.sum(e, axis=-1, keepdims=True)
        o_st = jnp.where(own_head, _dot(e.astype(BF16), vwin) * inv, 0.0)
        o_ref[j * GRID_W:(j + 1) * GRID_W, :] = _unstack_heads(o_st).astype(BF16)


def _na_call(p_na, bias, batch, seq):
    n = p_na.shape[0]
    rows = seq // GRID_W
    steps = rows // NA_ROWS_PER_STEP
    blk = NA_ROWS_PER_STEP * GRID_W
    return pl.pallas_call(
        functools.partial(_na_body, rows=rows),
        grid=(batch, steps),
        in_specs=[pl.BlockSpec((blk, NA_WIDTH), lambda b, i: (b * steps + i, 0)),
                  pl.BlockSpec((seq, NA_WIDTH), lambda b, i: (b, 1)),
                  pl.BlockSpec((seq, NA_WIDTH), lambda b, i: (b, 2)),
                  pl.BlockSpec(bias.shape, lambda b, i: (0, 0, 0))],
        out_specs=pl.BlockSpec((blk, NA_WIDTH), lambda b, i: (b * steps + i, 0)),
        out_shape=jax.ShapeDtypeStruct((n, NA_WIDTH), BF16),
        compiler_params=_cparams(2),
        name="na_attn",
    )(p_na, p_na, p_na, bias)


def _mla_project(p, in_refs, out_refs):
    qg_ref, kvg_ref, wqa_ref, wqb_ref, wk_ref, wvt_ref, cq_ref, sq_ref, ck_ref, sk_ref = in_refs
    q_ref, k_ref, vt_ref = out_refs
    cq = p[:, 0:MLA_Q_RANK]
    ckv = p[:, MLA_Q_RANK:MLA_Q_RANK + MLA_KV_RANK]
    kpe = p[:, MLA_Q_RANK + MLA_KV_RANK:MLA_Q_RANK + MLA_KV_RANK + LANES]
    kpe_rot = p[:, MLA_Q_RANK + MLA_KV_RANK + LANES:MLA_PAD]
    xq = cq * lax.rsqrt(jnp.mean(cq * cq, axis=-1, keepdims=True) + RMS_EPS) * qg_ref[...]
    xq = xq.astype(BF16)
    xkv = ckv * lax.rsqrt(jnp.mean(ckv * ckv, axis=-1, keepdims=True) + RMS_EPS) * kvg_ref[...]
    xkv = xkv.astype(BF16)
    qa = _dot(xq, wqa_ref[...])
    qb = _dot(xq, wqb_ref[...])
    scale = (MLA_NOPE_DIM + MLA_ROPE_DIM) ** -0.5 * LOG2_E
    kn = _dot(xkv, wk_ref[...])
    kpe_r = kpe * ck_ref[...] + kpe_rot * sk_ref[...]
    cq_t = cq_ref[...]
    sq_t = sq_ref[...]
    for h in range(MLA_HEADS):
        sl = slice(h * MLA_HEAD_PAD, (h + 1) * MLA_HEAD_PAD)
        q_ref[:, sl] = ((qa[:, sl] * cq_t + qb[:, sl] * sq_t) * scale).astype(BF16)
        k_ref[:, sl] = (kn[:, sl] + kpe_r).astype(BF16)
    rowi = lax.broadcasted_iota(jnp.int32, (MLA_HEADS * MLA_HEAD_PAD, 1), 0) % MLA_HEAD_PAD
    ones_rows = jnp.where(rowi >= MLA_V_DIM, 1.0, 0.0).astype(F32)
    vt_ref[...] = (_dot(wvt_ref[...], xkv, _NT) + ones_rows).astype(BF16)


def _column_max(x):
    while x.shape[0] > 8:
        rows = x.shape[0]
        fold = 8 if rows % 64 == 0 else rows // 8
        x = jnp.max(x.reshape(fold, rows // fold, x.shape[1]), axis=0)
    return jnp.max(x, axis=0, keepdims=True)


def _mlaattn_body(q_ref, k_ref, vt_ref, o_ref):
    seq = k_ref.shape[0]
    n_chunks = seq // MLA_KEY_CHUNK
    head = lambda h: slice(h * MLA_HEAD_PAD, (h + 1) * MLA_HEAD_PAD)
    keys = lambda c: slice(c * MLA_KEY_CHUNK, (c + 1) * MLA_KEY_CHUNK)

    items = [(t, h) for t in range(MLA_Q_PER_STEP) for h in range(MLA_HEADS_PER_STEP)]
    rows = lambda t: slice(t * Q_TILE, (t + 1) * Q_TILE)

    def scores(item, c):
        t, h = item
        return _dot(k_ref[keys(c), head(h)], q_ref[rows(t), head(h)], _NT)

    ni = len(items)
    outs = []
    st = {i: [scores(items[i], c) for c in range(n_chunks)] for i in range(min(2, ni))}
    col_max = {0: _column_max(functools.reduce(jnp.maximum, st[0]))}
    for i, (t, h) in enumerate(items):
        if i + 2 < ni:
            st[i + 2] = []
        ot = None
        for c in range(n_chunks):
            if i + 2 < ni:
                st[i + 2].append(scores(items[i + 2], c))
            et = jnp.exp2(st[i][c] - col_max[i]).astype(BF16)
            part = _dot(vt_ref[head(h), keys(c)], et)
            ot = part if ot is None else ot + part
        del st[i]
        if i + 1 < ni:
            col_max[i + 1] = _column_max(functools.reduce(jnp.maximum, st[i + 1]))
        outs.append(ot[0:MLA_V_DIM, :] / ot[MLA_V_DIM:MLA_V_DIM + 1, :])
        if h == MLA_HEADS_PER_STEP - 1:
            o_ref[rows(t), :] = jnp.concatenate(outs, axis=0).T.astype(BF16)
            outs = []


def _mlaattn_call(q, k, v, batch, seq):
    n = q.shape[0]
    q_rows = Q_TILE * MLA_Q_PER_STEP
    qt = seq // q_rows
    groups = MLA_HEADS // MLA_HEADS_PER_STEP
    wide = MLA_HEADS_PER_STEP * MLA_HEAD_PAD
    return pl.pallas_call(
        _mlaattn_body,
        grid=(batch, groups, qt),
        in_specs=[pl.BlockSpec((q_rows, wide), lambda b, hg, i: (b * qt + i, hg)),
                  pl.BlockSpec((seq, wide), lambda b, hg, i: (b, hg)),
                  pl.BlockSpec((wide, seq), lambda b, hg, i: (hg, b))],
        out_specs=pl.BlockSpec((q_rows, MLA_HEADS_PER_STEP * MLA_V_DIM),
                               lambda b, hg, i: (b * qt + i, hg)),
        out_shape=jax.ShapeDtypeStruct((n, MLA_WIDTH), BF16),
        compiler_params=_cparams(3),
        name="mla_attn",
    )(q, k, v)


def _outproj_body(xs_ref, yf_ref, yb_ref, na_ref, mla_ref, h_ref, a0_ref, a2_ref, ka_ref, rk_ref,
                  gg_ref, gb_ref, g2_ref, w_ref, g_ref, b_ref, o_ref, *, alpha):
    o_rw = _rw_output(xs_ref[...], yf_ref[...] + yb_ref[...],
                      a0_ref, a2_ref, ka_ref, rk_ref, gg_ref, gb_ref, g2_ref)
    mix = _dot(o_rw, w_ref[0:RW_WIDTH, :])
    mix = mix + _dot(na_ref[...], w_ref[RW_WIDTH:RW_WIDTH + NA_WIDTH, :])
    mix = mix + _dot(mla_ref[...], w_ref[RW_WIDTH + NA_WIDTH:, :])
    o_ref[...] = _layer_norm(alpha * h_ref[...] + mix, g_ref[...], b_ref[...])


def _outproj_call(xs, y_f, y_b, o_na, o_mla, h, rw_prm, w, g, b, alpha):
    n = h.shape[0]
    row = lambda width: pl.BlockSpec((ROW_TILE, width), lambda i: (i, 0))
    full = lambda a: pl.BlockSpec(a.shape, lambda i: (0,) * a.ndim)
    return pl.pallas_call(
        functools.partial(_outproj_body, alpha=alpha),
        grid=(n // ROW_TILE,),
        in_specs=[row(RW_PAD), row(RW_WIDTH), row(RW_WIDTH), row(NA_WIDTH), row(MLA_WIDTH),
                  row(D_MODEL)] + [full(a) for a in rw_prm] + [full(w), full(g), full(b)],
        out_specs=row(D_MODEL),
        out_shape=jax.ShapeDtypeStruct((n, D_MODEL), F32),
        compiler_params=_cparams(1),
        name="out_proj",
    )(xs, y_f, y_b, o_na, o_mla, h, *rw_prm, w, g, b)


def _ffn_body(h_ref, wg_ref, wu_ref, wd_ref, g_ref, b_ref, o_ref, *, alpha):
    h = h_ref[...]
    hb = h.astype(BF16)
    acc = alpha * h
    for j in range(D_FF // FFN_TILE):
        cols = slice(j * FFN_TILE, (j + 1) * FFN_TILE)
        gt = _dot(hb, wg_ref[:, cols])
        up = _dot(hb, wu_ref[:, cols])
        act = (gt * _sigmoid(gt) * up).astype(BF16)
        acc = acc + _dot(act, wd_ref[cols, :])
    o_ref[...] = _layer_norm(acc, g_ref[...], b_ref[...])


def _ffn_call(h, wg, wu, wd, g, b, alpha):
    n = h.shape[0]
    resident = lambda shape: pl.BlockSpec(shape, lambda i: (0, 0), pipeline_mode=pl.Buffered(1))
    return pl.pallas_call(
        functools.partial(_ffn_body, alpha=alpha),
        grid=(n // MM_ROW_TILE,),
        in_specs=[pl.BlockSpec((MM_ROW_TILE, D_MODEL), lambda i: (i, 0)),
                  resident((D_MODEL, D_FF)), resident((D_MODEL, D_FF)), resident((D_FF, D_MODEL)),
                  pl.BlockSpec((1, D_MODEL), lambda i: (0, 0)),
                  pl.BlockSpec((1, D_MODEL), lambda i: (0, 0))],
        out_specs=pl.BlockSpec((MM_ROW_TILE, D_MODEL), lambda i: (i, 0)),
        out_shape=jax.ShapeDtypeStruct((n, D_MODEL), F32),
        compiler_params=_cparams(1),
        name="ffn",
    )(h, wg, wu, wd, g, b)


def _rope_rotate_cols(w_pe):
    half = MLA_ROPE_DIM // 2
    return jnp.concatenate([-w_pe[..., half:], w_pe[..., :half]], axis=-1)


def _prep_in_proj(w_in):
    d = w_in.shape[0]
    rw = w_in[:, :RW_IN]
    na = w_in[:, RW_IN:RW_IN + NA_IN]
    mla = w_in[:, RW_IN + NA_IN:]
    cq_ckv = mla[:, :MLA_Q_RANK + MLA_KV_RANK]
    kpe = mla[:, MLA_Q_RANK + MLA_KV_RANK:]
    z = lambda c: jnp.zeros((d, c), w_in.dtype)
    tail = LANES - MLA_NOPE_DIM - MLA_ROPE_DIM
    w = jnp.concatenate([rw, z(RW_PAD - RW_IN), na, cq_ckv,
                         z(MLA_NOPE_DIM), kpe, z(tail),
                         z(MLA_NOPE_DIM), _rope_rotate_cols(kpe), z(tail)], axis=1)
    return w.astype(BF16)


def _prep_lora(w2, row_offset):
    out = jnp.zeros((2, LANES, w2.shape[-1]), w2.dtype)
    rank = w2.shape[1]
    for d in range(2):
        out = out.at[d, row_offset + d * rank:row_offset + (d + 1) * rank].set(w2[d])
    return out.astype(BF16)


def _prep_mla_weights(w_q_b, w_kv_b):
    qd = MLA_NOPE_DIM + MLA_ROPE_DIM
    wq = w_q_b.reshape(MLA_Q_RANK, MLA_HEADS, qd)
    tail = MLA_HEAD_PAD - qd
    zq = lambda c: jnp.zeros((MLA_Q_RANK, MLA_HEADS, c), w_q_b.dtype)
    wqa = jnp.concatenate([wq, zq(tail)], axis=-1).reshape(MLA_Q_RANK, -1)
    wqb = jnp.concatenate([zq(MLA_NOPE_DIM), _rope_rotate_cols(wq[..., MLA_NOPE_DIM:]), zq(tail)],
                          axis=-1).reshape(MLA_Q_RANK, -1)
    wkv = w_kv_b.reshape(MLA_KV_RANK, MLA_HEADS, MLA_NOPE_DIM + MLA_V_DIM)
    zk = jnp.zeros((MLA_KV_RANK, MLA_HEADS, MLA_HEAD_PAD - MLA_NOPE_DIM), w_kv_b.dtype)
    wk = jnp.concatenate([wkv[..., :MLA_NOPE_DIM], zk], axis=-1).reshape(MLA_KV_RANK, -1)
    zv = jnp.zeros((MLA_KV_RANK, MLA_HEADS, MLA_HEAD_PAD - MLA_V_DIM), w_kv_b.dtype)
    wv = jnp.concatenate([wkv[..., MLA_NOPE_DIM:], zv], axis=-1).reshape(MLA_KV_RANK, -1)
    return wqa.astype(BF16), wqb.astype(BF16), wk.astype(BF16), wv.T.astype(BF16)


def _rope_tables(seq):
    inv_freq = jnp.power(ROPE_THETA, -jnp.arange(0, MLA_ROPE_DIM, 2, dtype=F32) / MLA_ROPE_DIM)
    ang = jnp.arange(seq, dtype=F32)[:, None] * inv_freq[None, :]
    cos2 = jnp.concatenate([jnp.cos(ang), jnp.cos(ang)], axis=-1)
    sin2 = jnp.concatenate([jnp.sin(ang), jnp.sin(ang)], axis=-1)
    tail = LANES - MLA_NOPE_DIM - MLA_ROPE_DIM
    one = jnp.ones((seq, MLA_NOPE_DIM), F32)
    z = lambda c: jnp.zeros((seq, c), F32)
    cq = jnp.concatenate([one, cos2, z(tail)], axis=-1)
    sq = jnp.concatenate([z(MLA_NOPE_DIM), sin2, z(tail)], axis=-1)
    ck = jnp.concatenate([z(MLA_NOPE_DIM), cos2, z(tail)], axis=-1)
    return cq, sq, ck, sq


def _na_bias_table(rpb, rows):
    kr = min(NA_WIN_ROWS, rows)
    heads, n_dr, n_dc = rpb.shape
    edge = GRID_W - 1 - (NA_WIN_COLS - 1)
    ext = jnp.concatenate([jnp.broadcast_to(rpb[..., :1], (heads, n_dr, edge)), rpb,
                           jnp.broadcast_to(rpb[..., -1:], (heads, n_dr, edge))], axis=-1)
    span = 2 * GRID_W - 1
    period = jnp.concatenate([ext, jnp.zeros((heads, n_dr, 1), ext.dtype)], axis=-1)
    tiled = jnp.tile(period, (1, 1, GRID_W))[..., :GRID_W * span]
    toep = tiled.reshape(heads, n_dr, GRID_W, span)[..., GRID_W - 1:]
    slabs = []
    for delta in range(kr):
        lo = NA_WIN_ROWS - 1 - delta
        s = toep[:, lo:lo + kr]
        slabs.append(jnp.transpose(s, (0, 2, 1, 3)).reshape(heads * GRID_W, kr * GRID_W))
    return jnp.stack(slabs, axis=0).astype(F32)


def kernel(x, ln_in_g, ln_in_b, w_in, rw_mu, rw_w0, rw_w2, rw_a0, rw_a2, rw_g2, rw_k_k, rw_k_a,
           rw_r_k, rw_gn_g, rw_gn_b, na_rpb, mla_q_norm_g, mla_w_q_b, mla_kv_norm_g, mla_w_kv_b,
           w_out, ln1_g, ln1_b, w_ffn_gate, w_ffn_up, w_ffn_down, ln2_g, ln2_b):
    batch, seq, d = x.shape
    depth = w_in.shape[0]
    assert d == D_MODEL and seq % ROW_TILE == 0 and seq % CHUNK == 0 and seq % GRID_W == 0
    assert seq // GRID_W >= NA_WIN_ROWS and D_FF % FFN_TILE == 0 and seq % (Q_TILE * MLA_Q_PER_STEP) == 0
    assert seq % MM_ROW_TILE == 0
    assert (seq // GRID_W) % NA_ROWS_PER_STEP == 0 and MLA_V_DIM * 2 == MLA_HEAD_PAD
    assert batch % RW_BATCH_PER_STEP == 0 and CHUNK == HEAD_DIM and RW_HEADS % 2 == 0
    n = batch * seq
    alpha = float((2 * depth) ** 0.25)
    row = lambda a: a.reshape(1, -1)
    tables = _rope_tables(seq)

    h = x.reshape(n, d)
    for l in range(depth):
        mu = jnp.pad(rw_mu[l], ((0, 0), (0, RW_PAD - RW_IN)))
        w_l = _prep_in_proj(w_in[l])
        mla_prm = (row(mla_q_norm_g[l]), row(mla_kv_norm_g[l]),
                   *_prep_mla_weights(mla_w_q_b[l], mla_w_kv_b[l]))
        if l == 0:
            h, xs, p_na, q, k, v = _inproj_call(h, w_l, mu, mla_prm, tables, seq,
                                                ln=(row(ln_in_g), row(ln_in_b)))
        else:
            xs, p_na, q, k, v = _inproj_call(h, w_l, mu, mla_prm, tables, seq)

        a2 = _prep_lora(rw_a2[l], 2 * RW_DECAY_RANK)
        scan_prm = (rw_w0[l], _prep_lora(rw_w2[l], 0), rw_a0[l], a2,
                    row(rw_k_k[l]), row(rw_k_a[l]))
        y_f, y_b = _rw_call(xs.reshape(batch, seq, RW_PAD), scan_prm, batch, seq)
        y_f = y_f.reshape(n, RW_WIDTH)
        y_b = y_b.reshape(n, RW_WIDTH)
        g2 = jnp.pad(rw_g2[l], ((0, LANES - RW_GATE_RANK), (0, 0))).astype(BF16)
        out_prm = (rw_a0[l], a2, row(rw_k_a[l]), row(rw_r_k[l]),
                   row(rw_gn_g[l]), row(rw_gn_b[l]), g2)

        o_na = _na_call(p_na, _na_bias_table(na_rpb[l], seq // GRID_W), batch, seq)

        o_mla = _mlaattn_call(q, k, v, batch, seq)

        h = _outproj_call(xs, y_f, y_b, o_na, o_mla, h, out_prm, w_out[l].astype(BF16),
                          row(ln1_g[l]), row(ln1_b[l]), alpha)
        h = _ffn_call(h, w_ffn_gate[l].astype(BF16), w_ffn_up[l].astype(BF16),
                      w_ffn_down[l].astype(BF16), row(ln2_g[l]), row(ln2_b[l]), alpha)
    return h.reshape(batch, seq, d)
```

```python
import functools

import jax
import jax.numpy as jnp
import numpy as np
from jax import lax
from jax.experimental import pallas as pl
from jax.experimental.pallas import tpu as pltpu

F32 = jnp.float32
BF16 = jnp.bfloat16

D_MODEL = 1024
HEAD_DIM = 64
GRID_W = 64
RW_HEADS = 4
RW_WIDTH = 256
RW_DECAY_RANK = 32
RW_ICLR_RANK = 32
RW_GATE_RANK = 64
RW_GN_EPS = 64e-5
NA_HEADS = 4
NA_WIDTH = 256
NA_WIN_ROWS = 8
NA_WIN_COLS = 16
MLA_HEADS = 8
MLA_Q_RANK = 256
MLA_KV_RANK = 128
MLA_NOPE_DIM = 64
MLA_ROPE_DIM = 32
MLA_V_DIM = 64
MLA_WIDTH = 512
ROPE_THETA = 10000.0
RW_IN = 960
NA_IN = 768
MLA_IN = 416
D_FF = 2816
LN_EPS = 1e-5
RMS_EPS = 1e-6
NEG_INF = -1e30
LOG2_E = 1.4426950408889634

LANES = 128
RW_PAD = 1024
MLA_PAD = 640
IN_PAD = RW_PAD + NA_IN + MLA_PAD
MLA_HEAD_PAD = 128
CHUNK = 64
RW_BATCH_PER_STEP = 8
RW_STAGGER_ROUNDS = 1
ROW_TILE = 512
MM_ROW_TILE = 1024
FFN_TILE = 256
Q_TILE = 256
MLA_HEADS_PER_STEP = 8
MLA_Q_PER_STEP = 2
MLA_KEY_CHUNK = 1024
NA_ROWS_PER_STEP = 8
VMEM_LIMIT = 56 * 1024 * 1024


def _cparams(n_axes):
    return pltpu.CompilerParams(dimension_semantics=("arbitrary",) * n_axes,
                                vmem_limit_bytes=VMEM_LIMIT)


def _split_bf16(x, parts):
    out = []
    rem = x
    for i in range(parts):
        p = rem.astype(BF16)
        out.append(p)
        if i + 1 < parts:
            rem = rem - p.astype(F32)
    return out


_NN = (((1,), (0,)), ((), ()))
_NT = (((1,), (1,)), ((), ()))


def _dot(a, b, dims=_NN):
    return lax.dot_general(a, b, dims, preferred_element_type=F32)


def _dot1(a, b, dims=_NN):
    return _dot(a.astype(BF16), b.astype(BF16), dims)


def _dot_exact_rhs(a, b_bf16, dims=_NN):
    a0, a1, a2 = _split_bf16(a, 3)
    return _dot(a0, b_bf16, dims) + (_dot(a1, b_bf16, dims) + _dot(a2, b_bf16, dims))


def _sigmoid(x):
    return 1.0 / (1.0 + jnp.exp(-x))


def _softplus(x):
    return jnp.maximum(x, 0.0) + jnp.log(1.0 + jnp.exp(-jnp.abs(x)))


def _layer_norm(x, g, b):
    mu = jnp.mean(x, axis=-1, keepdims=True)
    xc = x - mu
    var = jnp.mean(xc * xc, axis=-1, keepdims=True)
    return xc * lax.rsqrt(var + LN_EPS) * g + b


def _head_sum_matrix(width, head):
    r = lax.broadcasted_iota(jnp.int32, (width, width), 0) // head
    c = lax.broadcasted_iota(jnp.int32, (width, width), 1) // head
    return jnp.where(r == c, 1.0, 0.0).astype(BF16)


def _token_shift(x, prev_row, next_row, mu0, mu1):
    rows = x.shape[0]
    ridx = lax.broadcasted_iota(jnp.int32, x.shape, 0)
    xp = jnp.where(ridx == 0, prev_row, pltpu.roll(x, 1, 0))
    xn = jnp.where(ridx == rows - 1, next_row, pltpu.roll(x, rows - 1, 0))
    return x + mu0 * (xp - x) + mu1 * (xn - x)


def _inproj_body(*refs, tiles_per_seq, apply_ln):
    if apply_ln:
        x_ref, xp_ref, xn_ref, g_ref, b_ref, w_ref, mu_ref = refs[:7]
        mla_in = refs[7:17]
        h_ref, rw_ref, na_ref = refs[17:20]
        mla_out = refs[20:23]
        norm = lambda t: _layer_norm(t, g_ref[...], b_ref[...])
    else:
        x_ref, xp_ref, xn_ref, w_ref, mu_ref = refs[:5]
        mla_in = refs[5:15]
        rw_ref, na_ref = refs[15:17]
        mla_out = refs[17:20]
        norm = lambda t: t
    i = pl.program_id(0)
    h = norm(x_ref[...])
    if apply_ln:
        h_ref[...] = h
    hb = h.astype(BF16)
    w_rw = w_ref[:, 0:RW_PAD]
    p_rw = _dot(hb, w_rw)
    zero = jnp.zeros((1, RW_PAD), F32)
    p_prev = _dot(norm(xp_ref[...]).astype(BF16), w_rw)[7:8, :]
    p_next = _dot(norm(xn_ref[...]).astype(BF16), w_rw)[0:1, :]
    p_prev = jnp.where(i % tiles_per_seq == 0, zero, p_prev)
    p_next = jnp.where(i % tiles_per_seq == tiles_per_seq - 1, zero, p_next)
    rw_ref[...] = _token_shift(p_rw, p_prev, p_next, mu_ref[0:1, :], mu_ref[1:2, :])
    na_ref[...] = _dot(hb, w_ref[:, RW_PAD:RW_PAD + NA_IN]).astype(BF16)
    _mla_project(_dot(hb, w_ref[:, RW_PAD + NA_IN:IN_PAD]), mla_in, mla_out)


def _inproj_call(x, w, mu, mla_prm, tables, seq, ln=None):
    n = x.shape[0]
    t8 = ROW_TILE // 8
    last8 = n // 8 - 1
    tps = seq // ROW_TILE
    hw = MLA_HEADS * MLA_HEAD_PAD
    full = lambda a: pl.BlockSpec(a.shape, lambda i: (0,) * a.ndim)
    row = lambda width: pl.BlockSpec((ROW_TILE, width), lambda i: (i, 0))
    tab = pl.BlockSpec((ROW_TILE, LANES), lambda i: (i % tps, 0))
    in_specs = [row(D_MODEL),
                pl.BlockSpec((8, D_MODEL), lambda i: (jnp.maximum(i * t8 - 1, 0), 0)),
                pl.BlockSpec((8, D_MODEL), lambda i: (jnp.minimum((i + 1) * t8, last8), 0))]
    args = [x, x, x]
    out_specs = [row(RW_PAD), row(NA_IN), row(hw), row(hw),
                 pl.BlockSpec((hw, ROW_TILE), lambda i: (0, i))]
    out_shape = [jax.ShapeDtypeStruct((n, RW_PAD), F32),
                 jax.ShapeDtypeStruct((n, NA_IN), BF16),
                 jax.ShapeDtypeStruct((n, hw), BF16),
                 jax.ShapeDtypeStruct((n, hw), BF16),
                 jax.ShapeDtypeStruct((hw, n), BF16)]
    if ln is not None:
        in_specs += [full(ln[0]), full(ln[1])]
        args += list(ln)
        out_specs = [row(D_MODEL)] + out_specs
        out_shape = [jax.ShapeDtypeStruct((n, D_MODEL), F32)] + out_shape
    in_specs += [full(w), full(mu)] + [full(a) for a in mla_prm] + [tab] * len(tables)
    args += [w, mu, *mla_prm, *tables]
    return pl.pallas_call(
        functools.partial(_inproj_body, tiles_per_seq=tps, apply_ln=ln is not None),
        grid=(n // ROW_TILE,),
        in_specs=in_specs,
        out_specs=out_specs,
        out_shape=out_shape,
        compiler_params=_cparams(1),
        name="in_proj",
    )(*args)


def _unstack_heads(x):
    c = x.shape[0] // 4
    return (x[0:c] + x[c:2 * c]) + (x[2 * c:3 * c] + x[3 * c:4 * c])


def _neumann_inverse(ls):
    n = ls[0].shape[0]
    eye = jnp.where(lax.broadcasted_iota(jnp.int32, (n, n), 0) ==
                    lax.broadcasted_iota(jnp.int32, (n, n), 1), 1.0, 0.0).astype(F32)
    ts = [eye + l for l in ls]
    ps = []
    for l in ls:
        lb = l.astype(BF16)
        ps.append(_dot(lb, lb))
    yield None
    steps = int(np.log2(CHUNK)) - 1
    for i in range(steps):
        for j in range(len(ls)):
            pb = ps[j].astype(BF16)
            if i + 1 < steps:
                prod = _dot(jnp.concatenate([ts[j].astype(BF16), pb], axis=0), pb)
                ts[j] = ts[j] + prod[0:n]
                ps[j] = prod[n:2 * n]
            else:
                ts[j] = ts[j] + _dot(ts[j].astype(BF16), pb)
        yield None
    return ts


def _rw_direction(xs, d, prm, h_ref):
    w0_ref, w2_ref, a0_ref, a2_ref, kk_ref, ka_ref = prm
    c = CHUNK
    w = RW_WIDTH
    r = xs[:, 0:w]
    k = xs[:, w:2 * w]
    v = xs[:, 2 * w:3 * w]
    lora = xs[:, 3 * w:3 * w + LANES]
    yield None
    lw = _dot(jnp.tanh(lora).astype(BF16), w2_ref[d])
    la = _dot(lora.astype(BF16), a2_ref[d])
    log_w = -_softplus(-(w0_ref[d:d + 1, :] + lw)) - 0.5
    ldec = -jnp.exp(log_w)
    iclr = _sigmoid(a0_ref[d:d + 1, :] + la)
    yield None
    kk = k * kk_ref[...]
    ss = _dot_exact_rhs(kk * kk, _head_sum_matrix(w, HEAD_DIM))
    kk = kk / jnp.maximum(jnp.sqrt(ss), 1e-12)
    k_dir = k * (1.0 + (iclr - 1.0) * ka_ref[...])
    a = -kk
    b = kk * iclr
    yield None

    ti = lax.broadcasted_iota(jnp.int32, (c, c), 0)
    si = lax.broadcasted_iota(jnp.int32, (c, c), 1)
    cum = jnp.where(si <= ti, 1.0, 0.0).astype(BF16)
    l0, l1, l2 = _split_bf16(ldec, 3)
    cl_incl = _dot(cum, l0) + (_dot(cum, l1) + _dot(cum, l2))
    cl_excl = cl_incl - ldec
    tot = cl_incl[c - 1:c, :]
    yield None
    if d == 0:
        e_a = jnp.exp(cl_excl)
        e_d = jnp.exp(-cl_incl)
        e_r = jnp.exp(cl_incl)
        e_h = jnp.exp(tot - cl_incl)
    else:
        e_a = jnp.exp(tot - cl_incl)
        e_d = jnp.exp(cl_excl - tot)
        e_r = e_a
        e_h = jnp.exp(cl_excl)
    at = a * e_a
    bt = b * e_d
    kt = k_dir * e_d
    rt = r * e_r
    bh = b * e_h
    kh = k_dir * e_h
    yield None

    pairs = RW_HEADS // 2
    grp = 2 * HEAD_DIM
    n2 = 2 * c
    lanes = lambda p: slice(p * grp, (p + 1) * grp)
    first_head = lax.broadcasted_iota(jnp.int32, (1, grp), 1) < HEAD_DIM

    def stack_pair(x, p):
        xb = x[:, lanes(p)].astype(BF16)
        z = jnp.zeros_like(xb)
        return jnp.concatenate([jnp.where(first_head, xb, z), jnp.where(first_head, z, xb)], axis=0)

    fold = lambda m: m[0:c] + m[c:n2]
    bk = jnp.concatenate([bt, kt], axis=0).astype(BF16)
    v_b = v.astype(BF16)

    ri = lax.broadcasted_iota(jnp.int32, (n2, n2), 0)
    ci = lax.broadcasted_iota(jnp.int32, (n2, n2), 1)
    same = (ri // c) == (ci // c)
    tc = ri % c
    sc = ci % c
    second = ci >= c
    own = (ri // c) == (ci // HEAD_DIM)
    if d == 0:
        strict = same & (sc < tc)
        strict_c = sc < tc
        incl_c = sc <= tc
    else:
        strict = same & (sc > tc)
        strict_c = sc > tc
        incl_c = strict_c
    yield None
    xa, labs, lak_c, mr_c = [], [], [], []
    for p in range(pairs):
        xa_p = stack_pair(at, p)
        bk_p = bk[:, lanes(p)]
        labs.append(jnp.where(strict, _dot(xa_p, stack_pair(bt, p), _NT), 0.0))
        pa = _dot(xa_p, bk_p, _NT)
        pr = _dot(stack_pair(rt, p), bk_p, _NT)
        xa.append(xa_p)
        lak_c.append(jnp.where(strict_c & second, pa, 0.0).astype(BF16))
        mr_c.append(jnp.where(incl_c, pr, 0.0).astype(BF16))
    yield None

    ts = yield from _neumann_inverse(labs)
    ts = [t.astype(BF16) for t in ts]
    qs, tas = [], []
    for p in range(pairs):
        vv = jnp.concatenate([v_b[:, lanes(p)]] * 2, axis=0)
        qs.append(jnp.where(own, _dot(lak_c[p], vv), 0.0))
        tas.append(_dot(ts[p], xa[p]))
    yield None
    u0s = [_dot(ts[p], qs[p].astype(BF16)) for p in range(pairs)]
    ta_c = jnp.concatenate([fold(m) for m in tas], axis=1).astype(BF16)
    u0_c = jnp.concatenate([fold(m) for m in u0s], axis=1)
    bk_t = jnp.concatenate([bh, kh], axis=0).T.astype(BF16)
    w_col = jnp.exp(jnp.sum(ldec.T, axis=1, keepdims=True))
    yield None

    h = h_ref[...]
    h_hi, h_lo = _split_bf16(h, 2)
    u_c = (_dot(ta_c, h_hi) + _dot(ta_c, h_lo)) + u0_c
    uv = jnp.concatenate([u_c.astype(BF16), v_b], axis=0)
    rt_b = rt.astype(BF16)
    intra = jnp.concatenate(
        [fold(jnp.where(own, _dot(mr_c[p], uv[:, lanes(p)]), 0.0)) for p in range(pairs)], axis=1)
    y = (_dot(rt_b, h_hi) + _dot(rt_b, h_lo)) + intra
    lane_k = lax.broadcasted_iota(jnp.int32, (w, w), 1)
    row_k = lax.broadcasted_iota(jnp.int32, (w, w), 0)
    bdm = (lane_k // HEAD_DIM) == (row_k // HEAD_DIM)
    h_ref[...] = w_col * h + jnp.where(bdm, _dot(bk_t, uv), 0.0)
    return y


def _run_interleaved(gens, delays):
    results = [None] * len(gens)
    live = list(range(len(gens)))
    rnd = 0
    while live:
        for i in list(live):
            if rnd < delays[i]:
                continue
            try:
                next(gens[i])
            except StopIteration as stop:
                results[i] = stop.value
                live.remove(i)
        rnd += 1
    return results


def _rw_body(xf_ref, xb_ref, w0_ref, w2_ref, a0_ref, a2_ref, kk_ref, ka_ref,
             yf_ref, yb_ref, hf_ref, hb_ref):
    ch = pl.program_id(1)

    @pl.when(ch == 0)
    def _():
        hf_ref[...] = jnp.zeros_like(hf_ref)
        hb_ref[...] = jnp.zeros_like(hb_ref)

    prm = (w0_ref, w2_ref, a0_ref, a2_ref, kk_ref, ka_ref)
    gens = []
    for bb in range(RW_BATCH_PER_STEP):
        gens.append(_rw_direction(xf_ref[bb], 0, prm, hf_ref.at[bb]))
        gens.append(_rw_direction(xb_ref[bb], 1, prm, hb_ref.at[bb]))
    delays = [RW_STAGGER_ROUNDS * (i // 2) for i in range(len(gens))]
    ys = _run_interleaved(gens, delays)
    for bb in range(RW_BATCH_PER_STEP):
        yf_ref[bb] = ys[2 * bb]
        yb_ref[bb] = ys[2 * bb + 1]


def _rw_call(xs, prm, batch, seq):
    nc = seq // CHUNK
    bps = RW_BATCH_PER_STEP

    def main_f(b, c):
        return (b, c, 0)

    def main_b(b, c):
        return (b, nc - 1 - c, 0)

    def full(shape):
        return pl.BlockSpec(shape, lambda b, c: (0,) * len(shape))

    w0, w2, a0, a2, kk, ka = prm
    return pl.pallas_call(
        _rw_body,
        grid=(batch // bps, nc),
        in_specs=[pl.BlockSpec((bps, CHUNK, RW_PAD), main_f),
                  pl.BlockSpec((bps, CHUNK, RW_PAD), main_b),
                  full(w0.shape), full(w2.shape), full(a0.shape),
                  full(a2.shape), full(kk.shape), full(ka.shape)],
        out_specs=[pl.BlockSpec((bps, CHUNK, RW_WIDTH), main_f),
                   pl.BlockSpec((bps, CHUNK, RW_WIDTH), main_b)],
        out_shape=[jax.ShapeDtypeStruct((batch, seq, RW_WIDTH), F32),
                   jax.ShapeDtypeStruct((batch, seq, RW_WIDTH), F32)],
        scratch_shapes=[pltpu.VMEM((bps, RW_WIDTH, RW_WIDTH), F32),
                        pltpu.VMEM((bps, RW_WIDTH, RW_WIDTH), F32)],
        compiler_params=_cparams(2),
        name="rwkv_scan",
    )(xs, xs, w0, w2, a0, a2, kk, ka)


def _rw_output(xs, y, a0_ref, a2_ref, ka_ref, rk_ref, gg_ref, gb_ref, g2_ref):
    w = RW_WIDTH
    r = xs[:, 0:w]
    k = xs[:, w:2 * w]
    v = xs[:, 2 * w:3 * w]
    lora = xs[:, 3 * w:3 * w + LANES]
    gblk = xs[:, 3 * w + LANES:3 * w + 2 * LANES]
    iclr0 = _sigmoid(a0_ref[0:1, :] + _dot(lora.astype(BF16), a2_ref[0]))
    k_dir0 = k * (1.0 + (iclr0 - 1.0) * ka_ref[...])
    gate = _dot(_sigmoid(gblk).astype(BF16), g2_ref[...])
    hsum = _head_sum_matrix(w, HEAD_DIM)
    mean = _dot_exact_rhs(y, hsum) * (1.0 / HEAD_DIM)
    yc = y - mean
    var = _dot_exact_rhs(yc * yc, hsum) * (1.0 / HEAD_DIM)
    yn = yc * lax.rsqrt(var + RW_GN_EPS) * gg_ref[...] + gb_ref[...]
    bonus = _dot_exact_rhs(r * k_dir0 * rk_ref[...], hsum) * v
    return ((yn + bonus) * gate).astype(BF16)


def _na_body(q_ref, k_ref, v_ref, bias_ref, o_ref, *, rows):
    g = pl.program_id(1)
    w = NA_WIDTH
    kr = NA_WIN_ROWS
    nk = kr * GRID_W
    nq = NA_HEADS * GRID_W
    qc = lax.broadcasted_iota(jnp.int32, (nq, nk), 0) % GRID_W
    kc = lax.broadcasted_iota(jnp.int32, (nq, nk), 1) % GRID_W
    cs = jnp.clip(qc - NA_WIN_COLS // 2, 0, GRID_W - NA_WIN_COLS)
    col_ok = (kc >= cs) & (kc < cs + NA_WIN_COLS)
    lane_head = lax.broadcasted_iota(jnp.int32, (1, w), 1) // HEAD_DIM
    head_masks = [lane_head == h for h in range(NA_HEADS)]
    own_head = (lax.broadcasted_iota(jnp.int32, (nq, w), 0) // GRID_W ==
                lax.broadcasted_iota(jnp.int32, (nq, w), 1) // HEAD_DIM)
    for j in range(NA_ROWS_PER_STEP):
        i = g * NA_ROWS_PER_STEP + j
        row_start = jnp.clip(i - kr // 2, 0, rows - kr)
        delta = i - row_start
        start = pl.multiple_of(row_start * GRID_W, GRID_W)
        q = q_ref[j * GRID_W:(j + 1) * GRID_W, :] * (HEAD_DIM ** -0.5)
        q_st = jnp.concatenate([jnp.where(m, q, jnp.zeros_like(q)) for m in head_masks], axis=0)
        kwin = k_ref[pl.ds(start, nk), :]
        vwin = v_ref[pl.ds(start, nk), :]
        s = jnp.where(col_ok, _dot(q_st, kwin, _NT) + bias_ref[delta], NEG_INF)
        m = jnp.max(s, axis=-1, keepdims=True)
        e = jnp.exp(s - m)
        inv = 1.0 / jnp.sum(e, axis=-1, keepdims=True)
        o_st = jnp.where(own_head, _dot(e.astype(BF16), vwin) * inv, 0.0)
        o_ref[j * GRID_W:(j + 1) * GRID_W, :] = _unstack_heads(o_st).astype(BF16)


def _na_call(p_na, bias, batch, seq):
    n = p_na.shape[0]
    rows = seq // GRID_W
    steps = rows // NA_ROWS_PER_STEP
    blk = NA_ROWS_PER_STEP * GRID_W
    return pl.pallas_call(
        functools.partial(_na_body, rows=rows),
        grid=(batch, steps),
        in_specs=[pl.BlockSpec((blk, NA_WIDTH), lambda b, i: (b * steps + i, 0)),
                  pl.BlockSpec((seq, NA_WIDTH), lambda b, i: (b, 1)),
                  pl.BlockSpec((seq, NA_WIDTH), lambda b, i: (b, 2)),
                  pl.BlockSpec(bias.shape, lambda b, i: (0, 0, 0))],
        out_specs=pl.BlockSpec((blk, NA_WIDTH), lambda b, i: (b * steps + i, 0)),
        out_shape=jax.ShapeDtypeStruct((n, NA_WIDTH), BF16),
        compiler_params=_cparams(2),
        name="na_attn",
    )(p_na, p_na, p_na, bias)


def _mla_project(p, in_refs, out_refs):
    qg_ref, kvg_ref, wqa_ref, wqb_ref, wk_ref, wvt_ref, cq_ref, sq_ref, ck_ref, sk_ref = in_refs
    q_ref, k_ref, vt_ref = out_refs
    cq = p[:, 0:MLA_Q_RANK]
    ckv = p[:, MLA_Q_RANK:MLA_Q_RANK + MLA_KV_RANK]
    kpe = p[:, MLA_Q_RANK + MLA_KV_RANK:MLA_Q_RANK + MLA_KV_RANK + LANES]
    kpe_rot = p[:, MLA_Q_RANK + MLA_KV_RANK + LANES:MLA_PAD]
    xq = cq * lax.rsqrt(jnp.mean(cq * cq, axis=-1, keepdims=True) + RMS_EPS) * qg_ref[...]
    xq = xq.astype(BF16)
    xkv = ckv * lax.rsqrt(jnp.mean(ckv * ckv, axis=-1, keepdims=True) + RMS_EPS) * kvg_ref[...]
    xkv = xkv.astype(BF16)
    qa = _dot(xq, wqa_ref[...])
    qb = _dot(xq, wqb_ref[...])
    scale = (MLA_NOPE_DIM + MLA_ROPE_DIM) ** -0.5 * LOG2_E
    kn = _dot(xkv, wk_ref[...])
    kpe_r = kpe * ck_ref[...] + kpe_rot * sk_ref[...]
    cq_t = cq_ref[...]
    sq_t = sq_ref[...]
    for h in range(MLA_HEADS):
        sl = slice(h * MLA_HEAD_PAD, (h + 1) * MLA_HEAD_PAD)
        q_ref[:, sl] = ((qa[:, sl] * cq_t + qb[:, sl] * sq_t) * scale).astype(BF16)
        k_ref[:, sl] = (kn[:, sl] + kpe_r).astype(BF16)
    rowi = lax.broadcasted_iota(jnp.int32, (MLA_HEADS * MLA_HEAD_PAD, 1), 0) % MLA_HEAD_PAD
    ones_rows = jnp.where(rowi >= MLA_V_DIM, 1.0, 0.0).astype(F32)
    vt_ref[...] = (_dot(wvt_ref[...], xkv, _NT) + ones_rows).astype(BF16)


def _column_max(x):
    while x.shape[0] > 8:
        rows = x.shape[0]
        fold = 8 if rows % 64 == 0 else rows // 8
        x = jnp.max(x.reshape(fold, rows // fold, x.shape[1]), axis=0)
    return jnp.max(x, axis=0, keepdims=True)


def _mlaattn_body(q_ref, k_ref, vt_ref, o_ref):
    seq = k_ref.shape[0]
    n_chunks = seq // MLA_KEY_CHUNK
    head = lambda h: slice(h * MLA_HEAD_PAD, (h + 1) * MLA_HEAD_PAD)
    keys = lambda c: slice(c * MLA_KEY_CHUNK, (c + 1) * MLA_KEY_CHUNK)

    items = [(t, h) for t in range(MLA_Q_PER_STEP) for h in range(MLA_HEADS_PER_STEP)]
    rows = lambda t: slice(t * Q_TILE, (t + 1) * Q_TILE)

    def scores(item, c):
        t, h = item
        return _dot(k_ref[keys(c), head(h)], q_ref[rows(t), head(h)], _NT)

    ni = len(items)
    outs = []
    st = {i: [scores(items[i], c) for c in range(n_chunks)] for i in range(min(2, ni))}
    col_max = {0: _column_max(functools.reduce(jnp.maximum, st[0]))}
    for i, (t, h) in enumerate(items):
        if i + 2 < ni:
            st[i + 2] = []
        ot = None
        for c in range(n_chunks):
            if i + 2 < ni:
                st[i + 2].append(scores(items[i + 2], c))
            et = jnp.exp2(st[i][c] - col_max[i]).astype(BF16)
            part = _dot(vt_ref[head(h), keys(c)], et)
            ot = part if ot is None else ot + part
        del st[i]
        if i + 1 < ni:
            col_max[i + 1] = _column_max(functools.reduce(jnp.maximum, st[i + 1]))
        outs.append(ot[0:MLA_V_DIM, :] / ot[MLA_V_DIM:MLA_V_DIM + 1, :])
        if h == MLA_HEADS_PER_STEP - 1:
            o_ref[rows(t), :] = jnp.concatenate(outs, axis=0).T.astype(BF16)
            outs = []


def _mlaattn_call(q, k, v, batch, seq):
    n = q.shape[0]
    q_rows = Q_TILE * MLA_Q_PER_STEP
    qt = seq // q_rows
    groups = MLA_HEADS // MLA_HEADS_PER_STEP
    wide = MLA_HEADS_PER_STEP * MLA_HEAD_PAD
    return pl.pallas_call(
        _mlaattn_body,
        grid=(batch, groups, qt),
        in_specs=[pl.BlockSpec((q_rows, wide), lambda b, hg, i: (b * qt + i, hg)),
                  pl.BlockSpec((seq, wide), lambda b, hg, i: (b, hg)),
                  pl.BlockSpec((wide, seq), lambda b, hg, i: (hg, b))],
        out_specs=pl.BlockSpec((q_rows, MLA_HEADS_PER_STEP * MLA_V_DIM),
                               lambda b, hg, i: (b * qt + i, hg)),
        out_shape=jax.ShapeDtypeStruct((n, MLA_WIDTH), BF16),
        compiler_params=_cparams(3),
        name="mla_attn",
    )(q, k, v)


def _outproj_body(xs_ref, yf_ref, yb_ref, na_ref, mla_ref, h_ref, a0_ref, a2_ref, ka_ref, rk_ref,
                  gg_ref, gb_ref, g2_ref, w_ref, g_ref, b_ref, o_ref, *, alpha):
    o_rw = _rw_output(xs_ref[...], yf_ref[...] + yb_ref[...],
                      a0_ref, a2_ref, ka_ref, rk_ref, gg_ref, gb_ref, g2_ref)
    mix = _dot(o_rw, w_ref[0:RW_WIDTH, :])
    mix = mix + _dot(na_ref[...], w_ref[RW_WIDTH:RW_WIDTH + NA_WIDTH, :])
    mix = mix + _dot(mla_ref[...], w_ref[RW_WIDTH + NA_WIDTH:, :])
    o_ref[...] = _layer_norm(alpha * h_ref[...] + mix, g_ref[...], b_ref[...])


def _outproj_call(xs, y_f, y_b, o_na, o_mla, h, rw_prm, w, g, b, alpha):
    n = h.shape[0]
    row = lambda width: pl.BlockSpec((ROW_TILE, width), lambda i: (i, 0))
    full = lambda a: pl.BlockSpec(a.shape, lambda i: (0,) * a.ndim)
    return pl.pallas_call(
        functools.partial(_outproj_body, alpha=alpha),
        grid=(n // ROW_TILE,),
        in_specs=[row(RW_PAD), row(RW_WIDTH), row(RW_WIDTH), row(NA_WIDTH), row(MLA_WIDTH),
                  row(D_MODEL)] + [full(a) for a in rw_prm] + [full(w), full(g), full(b)],
        out_specs=row(D_MODEL),
        out_shape=jax.ShapeDtypeStruct((n, D_MODEL), F32),
        compiler_params=_cparams(1),
        name="out_proj",
    )(xs, y_f, y_b, o_na, o_mla, h, *rw_prm, w, g, b)


def _ffn_body(h_ref, wg_ref, wu_ref, wd_ref, g_ref, b_ref, o_ref, *, alpha):
    h = h_ref[...]
    hb = h.astype(BF16)
    acc = alpha * h
    for j in range(D_FF // FFN_TILE):
        cols = slice(j * FFN_TILE, (j + 1) * FFN_TILE)
        gt = _dot(hb, wg_ref[:, cols])
        up = _dot(hb, wu_ref[:, cols])
        act = (gt * _sigmoid(gt) * up).astype(BF16)
        acc = acc + _dot(act, wd_ref[cols, :])
    o_ref[...] = _layer_norm(acc, g_ref[...], b_ref[...])


def _ffn_call(h, wg, wu, wd, g, b, alpha):
    n = h.shape[0]
    resident = lambda shape: pl.BlockSpec(shape, lambda i: (0, 0), pipeline_mode=pl.Buffered(1))
    return pl.pallas_call(
        functools.partial(_ffn_body, alpha=alpha),
        grid=(n // MM_ROW_TILE,),
        in_specs=[pl.BlockSpec((MM_ROW_TILE, D_MODEL), lambda i: (i, 0)),
                  resident((D_MODEL, D_FF)), resident((D_MODEL, D_FF)), resident((D_FF, D_MODEL)),
                  pl.BlockSpec((1, D_MODEL), lambda i: (0, 0)),
                  pl.BlockSpec((1, D_MODEL), lambda i: (0, 0))],
        out_specs=pl.BlockSpec((MM_ROW_TILE, D_MODEL), lambda i: (i, 0)),
        out_shape=jax.ShapeDtypeStruct((n, D_MODEL), F32),
        compiler_params=_cparams(1),
        name="ffn",
    )(h, wg, wu, wd, g, b)


def _rope_rotate_cols(w_pe):
    half = MLA_ROPE_DIM // 2
    return jnp.concatenate([-w_pe[..., half:], w_pe[..., :half]], axis=-1)


def _prep_in_proj(w_in):
    d = w_in.shape[0]
    rw = w_in[:, :RW_IN]
    na = w_in[:, RW_IN:RW_IN + NA_IN]
    mla = w_in[:, RW_IN + NA_IN:]
    cq_ckv = mla[:, :MLA_Q_RANK + MLA_KV_RANK]
    kpe = mla[:, MLA_Q_RANK + MLA_KV_RANK:]
    z = lambda c: jnp.zeros((d, c), w_in.dtype)
    tail = LANES - MLA_NOPE_DIM - MLA_ROPE_DIM
    w = jnp.concatenate([rw, z(RW_PAD - RW_IN), na, cq_ckv,
                         z(MLA_NOPE_DIM), kpe, z(tail),
                         z(MLA_NOPE_DIM), _rope_rotate_cols(kpe), z(tail)], axis=1)
    return w.astype(BF16)


def _prep_lora(w2, row_offset):
    out = jnp.zeros((2, LANES, w2.shape[-1]), w2.dtype)
    rank = w2.shape[1]
    for d in range(2):
        out = out.at[d, row_offset + d * rank:row_offset + (d + 1) * rank].set(w2[d])
    return out.astype(BF16)


def _prep_mla_weights(w_q_b, w_kv_b):
    qd = MLA_NOPE_DIM + MLA_ROPE_DIM
    wq = w_q_b.reshape(MLA_Q_RANK, MLA_HEADS, qd)
    tail = MLA_HEAD_PAD - qd
    zq = lambda c: jnp.zeros((MLA_Q_RANK, MLA_HEADS, c), w_q_b.dtype)
    wqa = jnp.concatenate([wq, zq(tail)], axis=-1).reshape(MLA_Q_RANK, -1)
    wqb = jnp.concatenate([zq(MLA_NOPE_DIM), _rope_rotate_cols(wq[..., MLA_NOPE_DIM:]), zq(tail)],
                          axis=-1).reshape(MLA_Q_RANK, -1)
    wkv = w_kv_b.reshape(MLA_KV_RANK, MLA_HEADS, MLA_NOPE_DIM + MLA_V_DIM)
    zk = jnp.zeros((MLA_KV_RANK, MLA_HEADS, MLA_HEAD_PAD - MLA_NOPE_DIM), w_kv_b.dtype)
    wk = jnp.concatenate([wkv[..., :MLA_NOPE_DIM], zk], axis=-1).reshape(MLA_KV_RANK, -1)
    zv = jnp.zeros((MLA_KV_RANK, MLA_HEADS, MLA_HEAD_PAD - MLA_V_DIM), w_kv_b.dtype)
    wv = jnp.concatenate([wkv[..., MLA_NOPE_DIM:], zv], axis=-1).reshape(MLA_KV_RANK, -1)
    return wqa.astype(BF16), wqb.astype(BF16), wk.astype(BF16), wv.T.astype(BF16)


def _rope_tables(seq):
    inv_freq = jnp.power(ROPE_THETA, -jnp.arange(0, MLA_ROPE_DIM, 2, dtype=F32) / MLA_ROPE_DIM)
    ang = jnp.arange(seq, dtype=F32)[:, None] * inv_freq[None, :]
    cos2 = jnp.concatenate([jnp.cos(ang), jnp.cos(ang)], axis=-1)
    sin2 = jnp.concatenate([jnp.sin(ang), jnp.sin(ang)], axis=-1)
    tail = LANES - MLA_NOPE_DIM - MLA_ROPE_DIM
    one = jnp.ones((seq, MLA_NOPE_DIM), F32)
    z = lambda c: jnp.zeros((seq, c), F32)
    cq = jnp.concatenate([one, cos2, z(tail)], axis=-1)
    sq = jnp.concatenate([z(MLA_NOPE_DIM), sin2, z(tail)], axis=-1)
    ck = jnp.concatenate([z(MLA_NOPE_DIM), cos2, z(tail)], axis=-1)
    return cq, sq, ck, sq


def _na_bias_table(rpb, rows):
    kr = min(NA_WIN_ROWS, rows)
    heads, n_dr, n_dc = rpb.shape
    edge = GRID_W - 1 - (NA_WIN_COLS - 1)
    ext = jnp.concatenate([jnp.broadcast_to(rpb[..., :1], (heads, n_dr, edge)), rpb,
                           jnp.broadcast_to(rpb[..., -1:], (heads, n_dr, edge))], axis=-1)
    span = 2 * GRID_W - 1
    period = jnp.concatenate([ext, jnp.zeros((heads, n_dr, 1), ext.dtype)], axis=-1)
    tiled = jnp.tile(period, (1, 1, GRID_W))[..., :GRID_W * span]
    toep = tiled.reshape(heads, n_dr, GRID_W, span)[..., GRID_W - 1:]
    slabs = []
    for delta in range(kr):
        lo = NA_WIN_ROWS - 1 - delta
        s = toep[:, lo:lo + kr]
        slabs.append(jnp.transpose(s, (0, 2, 1, 3)).reshape(heads * GRID_W, kr * GRID_W))
    return jnp.stack(slabs, axis=0).astype(F32)


def kernel(x, ln_in_g, ln_in_b, w_in, rw_mu, rw_w0, rw_w2, rw_a0, rw_a2, rw_g2, rw_k_k, rw_k_a,
           rw_r_k, rw_gn_g, rw_gn_b, na_rpb, mla_q_norm_g, mla_w_q_b, mla_kv_norm_g, mla_w_kv_b,
           w_out, ln1_g, ln1_b, w_ffn_gate, w_ffn_up, w_ffn_down, ln2_g, ln2_b):
    batch, seq, d = x.shape
    depth = w_in.shape[0]
    assert d == D_MODEL and seq % ROW_TILE == 0 and seq % CHUNK == 0 and seq % GRID_W == 0
    assert seq // GRID_W >= NA_WIN_ROWS and D_FF % FFN_TILE == 0 and seq % (Q_TILE * MLA_Q_PER_STEP) == 0
    assert seq % MM_ROW_TILE == 0
    assert (seq // GRID_W) % NA_ROWS_PER_STEP == 0 and MLA_V_DIM * 2 == MLA_HEAD_PAD
    assert batch % RW_BATCH_PER_STEP == 0 and CHUNK == HEAD_DIM and RW_HEADS % 2 == 0
    n = batch * seq
    alpha = float((2 * depth) ** 0.25)
    row = lambda a: a.reshape(1, -1)
    tables = _rope_tables(seq)

    h = x.reshape(n, d)
    for l in range(depth):
        mu = jnp.pad(rw_mu[l], ((0, 0), (0, RW_PAD - RW_IN)))
        w_l = _prep_in_proj(w_in[l])
        mla_prm = (row(mla_q_norm_g[l]), row(mla_kv_norm_g[l]),
                   *_prep_mla_weights(mla_w_q_b[l], mla_w_kv_b[l]))
        if l == 0:
            h, xs, p_na, q, k, v = _inproj_call(h, w_l, mu, mla_prm, tables, seq,
                                                ln=(row(ln_in_g), row(ln_in_b)))
        else:
            xs, p_na, q, k, v = _inproj_call(h, w_l, mu, mla_prm, tables, seq)

        a2 = _prep_lora(rw_a2[l], 2 * RW_DECAY_RANK)
        scan_prm = (rw_w0[l], _prep_lora(rw_w2[l], 0), rw_a0[l], a2,
                    row(rw_k_k[l]), row(rw_k_a[l]))
        y_f, y_b = _rw_call(xs.reshape(batch, seq, RW_PAD), scan_prm, batch, seq)
        y_f = y_f.reshape(n, RW_WIDTH)
        y_b = y_b.reshape(n, RW_WIDTH)
        g2 = jnp.pad(rw_g2[l], ((0, LANES - RW_GATE_RANK), (0, 0))).astype(BF16)
        out_prm = (rw_a0[l], a2, row(rw_k_a[l]), row(rw_r_k[l]),
                   row(rw_gn_g[l]), row(rw_gn_b[l]), g2)

        o_na = _na_call(p_na, _na_bias_table(na_rpb[l], seq // GRID_W), batch, seq)

        o_mla = _mlaattn_call(q, k, v, batch, seq)

        h = _outproj_call(xs, y_f, y_b, o_na, o_mla, h, out_prm, w_out[l].astype(BF16),
                          row(ln1_g[l]), row(ln1_b[l]), alpha)
        h = _ffn_call(h, w_ffn_gate[l].astype(BF16), w_ffn_up[l].astype(BF16),
                      w_ffn_down[l].astype(BF16), row(ln2_g[l]), row(ln2_b[l]), alpha)
    return h.reshape(batch, seq, d)
```

```python
import functools

import jax
import jax.numpy as jnp
import numpy as np
from jax import lax
from jax.experimental import pallas as pl
from jax.experimental.pallas import tpu as pltpu

F32 = jnp.float32
BF16 = jnp.bfloat16

D_MODEL = 1024
HEAD_DIM = 64
GRID_W = 64
RW_HEADS = 4
RW_WIDTH = 256
RW_DECAY_RANK = 32
RW_ICLR_RANK = 32
RW_GATE_RANK = 64
RW_GN_EPS = 64e-5
NA_HEADS = 4
NA_WIDTH = 256
NA_WIN_ROWS = 8
NA_WIN_COLS = 16
MLA_HEADS = 8
MLA_Q_RANK = 256
MLA_KV_RANK = 128
MLA_NOPE_DIM = 64
MLA_ROPE_DIM = 32
MLA_V_DIM = 64
MLA_WIDTH = 512
ROPE_THETA = 10000.0
RW_IN = 960
NA_IN = 768
MLA_IN = 416
D_FF = 2816
LN_EPS = 1e-5
RMS_EPS = 1e-6
NEG_INF = -1e30
LOG2_E = 1.4426950408889634

LANES = 128
RW_PAD = 1024
MLA_PAD = 640
IN_PAD = RW_PAD + NA_IN + MLA_PAD
MLA_HEAD_PAD = 128
CHUNK = 64
RW_BATCH_PER_STEP = 8
RW_STAGGER_ROUNDS = 1
ROW_TILE = 512
MM_ROW_TILE = 1024
FFN_TILE = 256
Q_TILE = 256
MLA_HEADS_PER_STEP = 8
MLA_Q_PER_STEP = 2
MLA_KEY_CHUNK = 1024
NA_ROWS_PER_STEP = 8
VMEM_LIMIT = 56 * 1024 * 1024


def _cparams(n_axes):
    return pltpu.CompilerParams(dimension_semantics=("arbitrary",) * n_axes,
                                vmem_limit_bytes=VMEM_LIMIT)


def _split_bf16(x, parts):
    out = []
    rem = x
    for i in range(parts):
        p = rem.astype(BF16)
        out.append(p)
        if i + 1 < parts:
            rem = rem - p.astype(F32)
    return out


_NN = (((1,), (0,)), ((), ()))
_NT = (((1,), (1,)), ((), ()))


def _dot(a, b, dims=_NN):
    return lax.dot_general(a, b, dims, preferred_element_type=F32)


def _dot1(a, b, dims=_NN):
    return _dot(a.astype(BF16), b.astype(BF16), dims)


def _dot_exact_rhs(a, b_bf16, dims=_NN):
    a0, a1, a2 = _split_bf16(a, 3)
    return _dot(a0, b_bf16, dims) + (_dot(a1, b_bf16, dims) + _dot(a2, b_bf16, dims))


def _sigmoid(x):
    return 1.0 / (1.0 + jnp.exp(-x))


def _softplus(x):
    return jnp.maximum(x, 0.0) + jnp.log(1.0 + jnp.exp(-jnp.abs(x)))


def _layer_norm(x, g, b):
    mu = jnp.mean(x, axis=-1, keepdims=True)
    xc = x - mu
    var = jnp.mean(xc * xc, axis=-1, keepdims=True)
    return xc * lax.rsqrt(var + LN_EPS) * g + b


def _head_sum_matrix(width, head):
    r = lax.broadcasted_iota(jnp.int32, (width, width), 0) // head
    c = lax.broadcasted_iota(jnp.int32, (width, width), 1) // head
    return jnp.where(r == c, 1.0, 0.0).astype(BF16)


def _token_shift(x, prev_row, next_row, mu0, mu1):
    rows = x.shape[0]
    ridx = lax.broadcasted_iota(jnp.int32, x.shape, 0)
    xp = jnp.where(ridx == 0, prev_row, pltpu.roll(x, 1, 0))
    xn = jnp.where(ridx == rows - 1, next_row, pltpu.roll(x, rows - 1, 0))
    return x + mu0 * (xp - x) + mu1 * (xn - x)


def _inproj_body(*refs, tiles_per_seq, apply_ln):
    if apply_ln:
        x_ref, xp_ref, xn_ref, g_ref, b_ref, w_ref, mu_ref = refs[:7]
        mla_in = refs[7:17]
        h_ref, rw_ref, na_ref = refs[17:20]
        mla_out = refs[20:23]
        norm = lambda t: _layer_norm(t, g_ref[...], b_ref[...])
    else:
        x_ref, xp_ref, xn_ref, w_ref, mu_ref = refs[:5]
        mla_in = refs[5:15]
        rw_ref, na_ref = refs[15:17]
        mla_out = refs[17:20]
        norm = lambda t: t
    i = pl.program_id(0)
    h = norm(x_ref[...])
    if apply_ln:
        h_ref[...] = h
    hb = h.astype(BF16)
    w_rw = w_ref[:, 0:RW_PAD]
    p_rw = _dot(hb, w_rw)
    zero = jnp.zeros((1, RW_PAD), F32)
    p_prev = _dot(norm(xp_ref[...]).astype(BF16), w_rw)[7:8, :]
    p_next = _dot(norm(xn_ref[...]).astype(BF16), w_rw)[0:1, :]
    p_prev = jnp.where(i % tiles_per_seq == 0, zero, p_prev)
    p_next = jnp.where(i % tiles_per_seq == tiles_per_seq - 1, zero, p_next)
    rw_ref[...] = _token_shift(p_rw, p_prev, p_next, mu_ref[0:1, :], mu_ref[1:2, :])
    na_ref[...] = _dot(hb, w_ref[:, RW_PAD:RW_PAD + NA_IN]).astype(BF16)
    _mla_project(_dot(hb, w_ref[:, RW_PAD + NA_IN:IN_PAD]), mla_in, mla_out)


def _inproj_call(x, w, mu, mla_prm, tables, seq, ln=None):
    n = x.shape[0]
    t8 = ROW_TILE // 8
    last8 = n // 8 - 1
    tps = seq // ROW_TILE
    hw = MLA_HEADS * MLA_HEAD_PAD
    full = lambda a: pl.BlockSpec(a.shape, lambda i: (0,) * a.ndim)
    row = lambda width: pl.BlockSpec((ROW_TILE, width), lambda i: (i, 0))
    tab = pl.BlockSpec((ROW_TILE, LANES), lambda i: (i % tps, 0))
    in_specs = [row(D_MODEL),
                pl.BlockSpec((8, D_MODEL), lambda i: (jnp.maximum(i * t8 - 1, 0), 0)),
                pl.BlockSpec((8, D_MODEL), lambda i: (jnp.minimum((i + 1) * t8, last8), 0))]
    args = [x, x, x]
    out_specs = [row(RW_PAD), row(NA_IN), row(hw), row(hw),
                 pl.BlockSpec((hw, ROW_TILE), lambda i: (0, i))]
    out_shape = [jax.ShapeDtypeStruct((n, RW_PAD), F32),
                 jax.ShapeDtypeStruct((n, NA_IN), BF16),
                 jax.ShapeDtypeStruct((n, hw), BF16),
                 jax.ShapeDtypeStruct((n, hw), BF16),
                 jax.ShapeDtypeStruct((hw, n), BF16)]
    if ln is not None:
        in_specs += [full(ln[0]), full(ln[1])]
        args += list(ln)
        out_specs = [row(D_MODEL)] + out_specs
        out_shape = [jax.ShapeDtypeStruct((n, D_MODEL), F32)] + out_shape
    in_specs += [full(w), full(mu)] + [full(a) for a in mla_prm] + [tab] * len(tables)
    args += [w, mu, *mla_prm, *tables]
    return pl.pallas_call(
        functools.partial(_inproj_body, tiles_per_seq=tps, apply_ln=ln is not None),
        grid=(n // ROW_TILE,),
        in_specs=in_specs,
        out_specs=out_specs,
        out_shape=out_shape,
        compiler_params=_cparams(1),
        name="in_proj",
    )(*args)


def _unstack_heads(x):
    c = x.shape[0] // 4
    return (x[0:c] + x[c:2 * c]) + (x[2 * c:3 * c] + x[3 * c:4 * c])


def _neumann_inverse(ls):
    n = ls[0].shape[0]
    eye = jnp.where(lax.broadcasted_iota(jnp.int32, (n, n), 0) ==
                    lax.broadcasted_iota(jnp.int32, (n, n), 1), 1.0, 0.0).astype(F32)
    ts = [eye + l for l in ls]
    ps = []
    for l in ls:
        lb = l.astype(BF16)
        ps.append(_dot(lb, lb))
    yield None
    steps = int(np.log2(CHUNK)) - 1
    for i in range(steps):
        for j in range(len(ls)):
            pb = ps[j].astype(BF16)
            if i + 1 < steps:
                prod = _dot(jnp.concatenate([ts[j].astype(BF16), pb], axis=0), pb)
                ts[j] = ts[j] + prod[0:n]
                ps[j] = prod[n:2 * n]
            else:
                ts[j] = ts[j] + _dot(ts[j].astype(BF16), pb)
        yield None
    return ts


def _rw_direction(xs, d, prm, h_ref):
    w0_ref, w2_ref, a0_ref, a2_ref, kk_ref, ka_ref = prm
    c = CHUNK
    w = RW_WIDTH
    r = xs[:, 0:w]
    k = xs[:, w:2 * w]
    v = xs[:, 2 * w:3 * w]
    lora = xs[:, 3 * w:3 * w + LANES]
    yield None
    lw = _dot(jnp.tanh(lora).astype(BF16), w2_ref[d])
    la = _dot(lora.astype(BF16), a2_ref[d])
    log_w = -_softplus(-(w0_ref[d:d + 1, :] + lw)) - 0.5
    ldec = -jnp.exp(log_w)
    iclr = _sigmoid(a0_ref[d:d + 1, :] + la)
    yield None
    kk = k * kk_ref[...]
    ss = _dot_exact_rhs(kk * kk, _head_sum_matrix(w, HEAD_DIM))
    kk = kk / jnp.maximum(jnp.sqrt(ss), 1e-12)
    k_dir = k * (1.0 + (iclr - 1.0) * ka_ref[...])
    a = -kk
    b = kk * iclr
    yield None

    ti = lax.broadcasted_iota(jnp.int32, (c, c), 0)
    si = lax.broadcasted_iota(jnp.int32, (c, c), 1)
    cum = jnp.where(si <= ti, 1.0, 0.0).astype(BF16)
    l0, l1, l2 = _split_bf16(ldec, 3)
    cl_incl = _dot(cum, l0) + (_dot(cum, l1) + _dot(cum, l2))
    cl_excl = cl_incl - ldec
    tot = cl_incl[c - 1:c, :]
    yield None
    if d == 0:
        e_a = jnp.exp(cl_excl)
        e_d = jnp.exp(-cl_incl)
        e_r = jnp.exp(cl_incl)
        e_h = jnp.exp(tot - cl_incl)
    else:
        e_a = jnp.exp(tot - cl_incl)
        e_d = jnp.exp(cl_excl - tot)
        e_r = e_a
        e_h = jnp.exp(cl_excl)
    at = a * e_a
    bt = b * e_d
    kt = k_dir * e_d
    rt = r * e_r
    bh = b * e_h
    kh = k_dir * e_h
    yield None

    pairs = RW_HEADS // 2
    grp = 2 * HEAD_DIM
    n2 = 2 * c
    lanes = lambda p: slice(p * grp, (p + 1) * grp)
    first_head = lax.broadcasted_iota(jnp.int32, (1, grp), 1) < HEAD_DIM

    def stack_pair(x, p):
        xb = x[:, lanes(p)].astype(BF16)
        z = jnp.zeros_like(xb)
        return jnp.concatenate([jnp.where(first_head, xb, z), jnp.where(first_head, z, xb)], axis=0)

    fold = lambda m: m[0:c] + m[c:n2]
    bk = jnp.concatenate([bt, kt], axis=0).astype(BF16)
    v_b = v.astype(BF16)

    ri = lax.broadcasted_iota(jnp.int32, (n2, n2), 0)
    ci = lax.broadcasted_iota(jnp.int32, (n2, n2), 1)
    same = (ri // c) == (ci // c)
    tc = ri % c
    sc = ci % c
    second = ci >= c
    own = (ri // c) == (ci // HEAD_DIM)
    if d == 0:
        strict = same & (sc < tc)
        strict_c = sc < tc
        incl_c = sc <= tc
    else:
        strict = same & (sc > tc)
        strict_c = sc > tc
        incl_c = strict_c
    yield None
    xa, labs, lak_c, mr_c = [], [], [], []
    for p in range(pairs):
        xa_p = stack_pair(at, p)
        bk_p = bk[:, lanes(p)]
        labs.append(jnp.where(strict, _dot(xa_p, stack_pair(bt, p), _NT), 0.0))
        pa = _dot(xa_p, bk_p, _NT)
        pr = _dot(stack_pair(rt, p), bk_p, _NT)
        xa.append(xa_p)
        lak_c.append(jnp.where(strict_c & second, pa, 0.0).astype(BF16))
        mr_c.append(jnp.where(incl_c, pr, 0.0).astype(BF16))
    yield None

    ts = yield from _neumann_inverse(labs)
    ts = [t.astype(BF16) for t in ts]
    qs, tas = [], []
    for p in range(pairs):
        vv = jnp.concatenate([v_b[:, lanes(p)]] * 2, axis=0)
        qs.append(jnp.where(own, _dot(lak_c[p], vv), 0.0))
        tas.append(_dot(ts[p], xa[p]))
    yield None
    u0s = [_dot(ts[p], qs[p].astype(BF16)) for p in range(pairs)]
    ta_c = jnp.concatenate([fold(m) for m in tas], axis=1).astype(BF16)
    u0_c = jnp.concatenate([fold(m) for m in u0s], axis=1)
    bk_t = jnp.concatenate([bh, kh], axis=0).T.astype(BF16)
    w_col = jnp.exp(jnp.sum(ldec.T, axis=1, keepdims=True))
    yield None

    h = h_ref[...]
    h_b = h.astype(BF16)
    u_c = _dot(ta_c, h_b) + u0_c
    uv = jnp.concatenate([u_c.astype(BF16), v_b], axis=0)
    rt_b = rt.astype(BF16)
    intra = jnp.concatenate(
        [fold(jnp.where(own, _dot(mr_c[p], uv[:, lanes(p)]), 0.0)) for p in range(pairs)], axis=1)
    y = _dot(rt_b, h_b) + intra
    lane_k = lax.broadcasted_iota(jnp.int32, (w, w), 1)
    row_k = lax.broadcasted_iota(jnp.int32, (w, w), 0)
    bdm = (lane_k // HEAD_DIM) == (row_k // HEAD_DIM)
    h_ref[...] = w_col * h + jnp.where(bdm, _dot(bk_t, uv), 0.0)
    return y


def _run_interleaved(gens, delays):
    results = [None] * len(gens)
    live = list(range(len(gens)))
    rnd = 0
    while live:
        for i in list(live):
            if rnd < delays[i]:
                continue
            try:
                next(gens[i])
            except StopIteration as stop:
                results[i] = stop.value
                live.remove(i)
        rnd += 1
    return results


def _rw_body(xf_ref, xb_ref, w0_ref, w2_ref, a0_ref, a2_ref, kk_ref, ka_ref,
             yf_ref, yb_ref, hf_ref, hb_ref):
    ch = pl.program_id(1)

    @pl.when(ch == 0)
    def _():
        hf_ref[...] = jnp.zeros_like(hf_ref)
        hb_ref[...] = jnp.zeros_like(hb_ref)

    prm = (w0_ref, w2_ref, a0_ref, a2_ref, kk_ref, ka_ref)
    gens = []
    for bb in range(RW_BATCH_PER_STEP):
        gens.append(_rw_direction(xf_ref[bb], 0, prm, hf_ref.at[bb]))
        gens.append(_rw_direction(xb_ref[bb], 1, prm, hb_ref.at[bb]))
    delays = [RW_STAGGER_ROUNDS * (i // 2) for i in range(len(gens))]
    ys = _run_interleaved(gens, delays)
    for bb in range(RW_BATCH_PER_STEP):
        yf_ref[bb] = ys[2 * bb]
        yb_ref[bb] = ys[2 * bb + 1]


def _rw_call(xs, prm, batch, seq):
    nc = seq // CHUNK
    bps = RW_BATCH_PER_STEP

    def main_f(b, c):
        return (b, c, 0)

    def main_b(b, c):
        return (b, nc - 1 - c, 0)

    def full(shape):
        return pl.BlockSpec(shape, lambda b, c: (0,) * len(shape))

    w0, w2, a0, a2, kk, ka = prm
    return pl.pallas_call(
        _rw_body,
        grid=(batch // bps, nc),
        in_specs=[pl.BlockSpec((bps, CHUNK, RW_PAD), main_f),
                  pl.BlockSpec((bps, CHUNK, RW_PAD), main_b),
                  full(w0.shape), full(w2.shape), full(a0.shape),
                  full(a2.shape), full(kk.shape), full(ka.shape)],
        out_specs=[pl.BlockSpec((bps, CHUNK, RW_WIDTH), main_f),
                   pl.BlockSpec((bps, CHUNK, RW_WIDTH), main_b)],
        out_shape=[jax.ShapeDtypeStruct((batch, seq, RW_WIDTH), F32),
                   jax.ShapeDtypeStruct((batch, seq, RW_WIDTH), F32)],
        scratch_shapes=[pltpu.VMEM((bps, RW_WIDTH, RW_WIDTH), F32),
                        pltpu.VMEM((bps, RW_WIDTH, RW_WIDTH), F32)],
        compiler_params=_cparams(2),
        name="rwkv_scan",
    )(xs, xs, w0, w2, a0, a2, kk, ka)


def _rw_output(xs, y, a0_ref, a2_ref, ka_ref, rk_ref, gg_ref, gb_ref, g2_ref):
    w = RW_WIDTH
    r = xs[:, 0:w]
    k = xs[:, w:2 * w]
    v = xs[:, 2 * w:3 * w]
    lora = xs[:, 3 * w:3 * w + LANES]
    gblk = xs[:, 3 * w + LANES:3 * w + 2 * LANES]
    iclr0 = _sigmoid(a0_ref[0:1, :] + _dot(lora.astype(BF16), a2_ref[0]))
    k_dir0 = k * (1.0 + (iclr0 - 1.0) * ka_ref[...])
    gate = _dot(_sigmoid(gblk).astype(BF16), g2_ref[...])
    hsum = _head_sum_matrix(w, HEAD_DIM)
    mean = _dot_exact_rhs(y, hsum) * (1.0 / HEAD_DIM)
    yc = y - mean
    var = _dot_exact_rhs(yc * yc, hsum) * (1.0 / HEAD_DIM)
    yn = yc * lax.rsqrt(var + RW_GN_EPS) * gg_ref[...] + gb_ref[...]
    bonus = _dot_exact_rhs(r * k_dir0 * rk_ref[...], hsum) * v
    return ((yn + bonus) * gate).astype(BF16)


def _na_body(q_ref, k_ref, v_ref, bias_ref, o_ref, *, rows):
    g = pl.program_id(1)
    w = NA_WIDTH
    kr = NA_WIN_ROWS
    nk = kr * GRID_W
    nq = NA_HEADS * GRID_W
    qc = lax.broadcasted_iota(jnp.int32, (nq, nk), 0) % GRID_W
    kc = lax.broadcasted_iota(jnp.int32, (nq, nk), 1) % GRID_W
    cs = jnp.clip(qc - NA_WIN_COLS // 2, 0, GRID_W - NA_WIN_COLS)
    col_ok = (kc >= cs) & (kc < cs + NA_WIN_COLS)
    lane_head = lax.broadcasted_iota(jnp.int32, (1, w), 1) // HEAD_DIM
    head_masks = [lane_head == h for h in range(NA_HEADS)]
    own_head = (lax.broadcasted_iota(jnp.int32, (nq, w), 0) // GRID_W ==
                lax.broadcasted_iota(jnp.int32, (nq, w), 1) // HEAD_DIM)
    for j in range(NA_ROWS_PER_STEP):
        i = g * NA_ROWS_PER_STEP + j
        row_start = jnp.clip(i - kr // 2, 0, rows - kr)
        delta = i - row_start
        start = pl.multiple_of(row_start * GRID_W, GRID_W)
        q = q_ref[j * GRID_W:(j + 1) * GRID_W, :] * (HEAD_DIM ** -0.5)
        q_st = jnp.concatenate([jnp.where(m, q, jnp.zeros_like(q)) for m in head_masks], axis=0)
        kwin = k_ref[pl.ds(start, nk), :]
        vwin = v_ref[pl.ds(start, nk), :]
        s = jnp.where(col_ok, _dot(q_st, kwin, _NT) + bias_ref[delta], NEG_INF)
        m = jnp.max(s, axis=-1, keepdims=True)
        e = jnp.exp(s - m)
        inv = 1.0 / jnp.sum(e, axis=-1, keepdims=True)
        o_st = jnp.where(own_head, _dot(e.astype(BF16), vwin) * inv, 0.0)
        o_ref[j * GRID_W:(j + 1) * GRID_W, :] = _unstack_heads(o_st).astype(BF16)


def _na_call(p_na, bias, batch, seq):
    n = p_na.shape[0]
    rows = seq // GRID_W
    steps = rows // NA_ROWS_PER_STEP
    blk = NA_ROWS_PER_STEP * GRID_W
    return pl.pallas_call(
        functools.partial(_na_body, rows=rows),
        grid=(batch, steps),
        in_specs=[pl.BlockSpec((blk, NA_WIDTH), lambda b, i: (b * steps + i, 0)),
                  pl.BlockSpec((seq, NA_WIDTH), lambda b, i: (b, 1)),
                  pl.BlockSpec((seq, NA_WIDTH), lambda b, i: (b, 2)),
                  pl.BlockSpec(bias.shape, lambda b, i: (0, 0, 0))],
        out_specs=pl.BlockSpec((blk, NA_WIDTH), lambda b, i: (b * steps + i, 0)),
        out_shape=jax.ShapeDtypeStruct((n, NA_WIDTH), BF16),
        compiler_params=_cparams(2),
        name="na_attn",
    )(p_na, p_na, p_na, bias)


def _mla_project(p, in_refs, out_refs):
    qg_ref, kvg_ref, wqa_ref, wqb_ref, wk_ref, wvt_ref, cq_ref, sq_ref, ck_ref, sk_ref = in_refs
    q_ref, k_ref, vt_ref = out_refs
    cq = p[:, 0:MLA_Q_RANK]
    ckv = p[:, MLA_Q_RANK:MLA_Q_RANK + MLA_KV_RANK]
    kpe = p[:, MLA_Q_RANK + MLA_KV_RANK:MLA_Q_RANK + MLA_KV_RANK + LANES]
    kpe_rot = p[:, MLA_Q_RANK + MLA_KV_RANK + LANES:MLA_PAD]
    xq = cq * lax.rsqrt(jnp.mean(cq * cq, axis=-1, keepdims=True) + RMS_EPS) * qg_ref[...]
    xq = xq.astype(BF16)
    xkv = ckv * lax.rsqrt(jnp.mean(ckv * ckv, axis=-1, keepdims=True) + RMS_EPS) * kvg_ref[...]
    xkv = xkv.astype(BF16)
    qa = _dot(xq, wqa_ref[...])
    qb = _dot(xq, wqb_ref[...])
    scale = (MLA_NOPE_DIM + MLA_ROPE_DIM) ** -0.5 * LOG2_E
    kn = _dot(xkv, wk_ref[...])
    kpe_r = kpe * ck_ref[...] + kpe_rot * sk_ref[...]
    cq_t = cq_ref[...]
    sq_t = sq_ref[...]
    for h in range(MLA_HEADS):
        sl = slice(h * MLA_HEAD_PAD, (h + 1) * MLA_HEAD_PAD)
        q_ref[:, sl] = ((qa[:, sl] * cq_t + qb[:, sl] * sq_t) * scale).astype(BF16)
        k_ref[:, sl] = (kn[:, sl] + kpe_r).astype(BF16)
    rowi = lax.broadcasted_iota(jnp.int32, (MLA_HEADS * MLA_HEAD_PAD, 1), 0) % MLA_HEAD_PAD
    ones_rows = jnp.where(rowi >= MLA_V_DIM, 1.0, 0.0).astype(F32)
    vt_ref[...] = (_dot(wvt_ref[...], xkv, _NT) + ones_rows).astype(BF16)


def _column_max(x):
    while x.shape[0] > 8:
        rows = x.shape[0]
        fold = 8 if rows % 64 == 0 else rows // 8
        x = jnp.max(x.reshape(fold, rows // fold, x.shape[1]), axis=0)
    return jnp.max(x, axis=0, keepdims=True)


def _mlaattn_body(q_ref, k_ref, vt_ref, o_ref):
    seq = k_ref.shape[0]
    n_chunks = seq // MLA_KEY_CHUNK
    head = lambda h: slice(h * MLA_HEAD_PAD, (h + 1) * MLA_HEAD_PAD)
    keys = lambda c: slice(c * MLA_KEY_CHUNK, (c + 1) * MLA_KEY_CHUNK)

    items = [(t, h) for t in range(MLA_Q_PER_STEP) for h in range(MLA_HEADS_PER_STEP)]
    rows = lambda t: slice(t * Q_TILE, (t + 1) * Q_TILE)

    def scores(item, c):
        t, h = item
        return _dot(k_ref[keys(c), head(h)], q_ref[rows(t), head(h)], _NT)

    ni = len(items)
    outs = []
    st = {i: [scores(items[i], c) for c in range(n_chunks)] for i in range(min(2, ni))}
    col_max = {0: _column_max(functools.reduce(jnp.maximum, st[0]))}
    for i, (t, h) in enumerate(items):
        if i + 2 < ni:
            st[i + 2] = []
        ot = None
        for c in range(n_chunks):
            if i + 2 < ni:
                st[i + 2].append(scores(items[i + 2], c))
            et = jnp.exp2(st[i][c] - col_max[i]).astype(BF16)
            part = _dot(vt_ref[head(h), keys(c)], et)
            ot = part if ot is None else ot + part
        del st[i]
        if i + 1 < ni:
            col_max[i + 1] = _column_max(functools.reduce(jnp.maximum, st[i + 1]))
        outs.append(ot[0:MLA_V_DIM, :] / ot[MLA_V_DIM:MLA_V_DIM + 1, :])
        if h == MLA_HEADS_PER_STEP - 1:
            o_ref[rows(t), :] = jnp.concatenate(outs, axis=0).T.astype(BF16)
            outs = []


def _mlaattn_call(q, k, v, batch, seq):
    n = q.shape[0]
    q_rows = Q_TILE * MLA_Q_PER_STEP
    qt = seq // q_rows
    groups = MLA_HEADS // MLA_HEADS_PER_STEP
    wide = MLA_HEADS_PER_STEP * MLA_HEAD_PAD
    return pl.pallas_call(
        _mlaattn_body,
        grid=(batch, groups, qt),
        in_specs=[pl.BlockSpec((q_rows, wide), lambda b, hg, i: (b * qt + i, hg)),
                  pl.BlockSpec((seq, wide), lambda b, hg, i: (b, hg)),
                  pl.BlockSpec((wide, seq), lambda b, hg, i: (hg, b))],
        out_specs=pl.BlockSpec((q_rows, MLA_HEADS_PER_STEP * MLA_V_DIM),
                               lambda b, hg, i: (b * qt + i, hg)),
        out_shape=jax.ShapeDtypeStruct((n, MLA_WIDTH), BF16),
        compiler_params=_cparams(3),
        name="mla_attn",
    )(q, k, v)


def _outproj_body(xs_ref, yf_ref, yb_ref, na_ref, mla_ref, h_ref, a0_ref, a2_ref, ka_ref, rk_ref,
                  gg_ref, gb_ref, g2_ref, w_ref, g_ref, b_ref, o_ref, *, alpha):
    o_rw = _rw_output(xs_ref[...], yf_ref[...] + yb_ref[...],
                      a0_ref, a2_ref, ka_ref, rk_ref, gg_ref, gb_ref, g2_ref)
    mix = _dot(o_rw, w_ref[0:RW_WIDTH, :])
    mix = mix + _dot(na_ref[...], w_ref[RW_WIDTH:RW_WIDTH + NA_WIDTH, :])
    mix = mix + _dot(mla_ref[...], w_ref[RW_WIDTH + NA_WIDTH:, :])
    o_ref[...] = _layer_norm(alpha * h_ref[...] + mix, g_ref[...], b_ref[...])


def _outproj_call(xs, y_f, y_b, o_na, o_mla, h, rw_prm, w, g, b, alpha):
    n = h.shape[0]
    row = lambda width: pl.BlockSpec((ROW_TILE, width), lambda i: (i, 0))
    full = lambda a: pl.BlockSpec(a.shape, lambda i: (0,) * a.ndim)
    return pl.pallas_call(
        functools.partial(_outproj_body, alpha=alpha),
        grid=(n // ROW_TILE,),
        in_specs=[row(RW_PAD), row(RW_WIDTH), row(RW_WIDTH), row(NA_WIDTH), row(MLA_WIDTH),
                  row(D_MODEL)] + [full(a) for a in rw_prm] + [full(w), full(g), full(b)],
        out_specs=row(D_MODEL),
        out_shape=jax.ShapeDtypeStruct((n, D_MODEL), F32),
        compiler_params=_cparams(1),
        name="out_proj",
    )(xs, y_f, y_b, o_na, o_mla, h, *rw_prm, w, g, b)


def _ffn_body(h_ref, wg_ref, wu_ref, wd_ref, g_ref, b_ref, o_ref, *, alpha):
    h = h_ref[...]
    hb = h.astype(BF16)
    acc = alpha * h
    for j in range(D_FF // FFN_TILE):
        cols = slice(j * FFN_TILE, (j + 1) * FFN_TILE)
        gt = _dot(hb, wg_ref[:, cols])
        up = _dot(hb, wu_ref[:, cols])
        act = (gt * _sigmoid(gt) * up).astype(BF16)
        acc = acc + _dot(act, wd_ref[cols, :])
    o_ref[...] = _layer_norm(acc, g_ref[...], b_ref[...])


def _ffn_call(h, wg, wu, wd, g, b, alpha):
    n = h.shape[0]
    resident = lambda shape: pl.BlockSpec(shape, lambda i: (0, 0), pipeline_mode=pl.Buffered(1))
    return pl.pallas_call(
        functools.partial(_ffn_body, alpha=alpha),
        grid=(n // MM_ROW_TILE,),
        in_specs=[pl.BlockSpec((MM_ROW_TILE, D_MODEL), lambda i: (i, 0)),
                  resident((D_MODEL, D_FF)), resident((D_MODEL, D_FF)), resident((D_FF, D_MODEL)),
                  pl.BlockSpec((1, D_MODEL), lambda i: (0, 0)),
                  pl.BlockSpec((1, D_MODEL), lambda i: (0, 0))],
        out_specs=pl.BlockSpec((MM_ROW_TILE, D_MODEL), lambda i: (i, 0)),
        out_shape=jax.ShapeDtypeStruct((n, D_MODEL), F32),
        compiler_params=_cparams(1),
        name="ffn",
    )(h, wg, wu, wd, g, b)


def _rope_rotate_cols(w_pe):
    half = MLA_ROPE_DIM // 2
    return jnp.concatenate([-w_pe[..., half:], w_pe[..., :half]], axis=-1)


def _prep_in_proj(w_in):
    d = w_in.shape[0]
    rw = w_in[:, :RW_IN]
    na = w_in[:, RW_IN:RW_IN + NA_IN]
    mla = w_in[:, RW_IN + NA_IN:]
    cq_ckv = mla[:, :MLA_Q_RANK + MLA_KV_RANK]
    kpe = mla[:, MLA_Q_RANK + MLA_KV_RANK:]
    z = lambda c: jnp.zeros((d, c), w_in.dtype)
    tail = LANES - MLA_NOPE_DIM - MLA_ROPE_DIM
    w = jnp.concatenate([rw, z(RW_PAD - RW_IN), na, cq_ckv,
                         z(MLA_NOPE_DIM), kpe, z(tail),
                         z(MLA_NOPE_DIM), _rope_rotate_cols(kpe), z(tail)], axis=1)
    return w.astype(BF16)


def _prep_lora(w2, row_offset):
    out = jnp.zeros((2, LANES, w2.shape[-1]), w2.dtype)
    rank = w2.shape[1]
    for d in range(2):
        out = out.at[d, row_offset + d * rank:row_offset + (d + 1) * rank].set(w2[d])
    return out.astype(BF16)


def _prep_mla_weights(w_q_b, w_kv_b):
    qd = MLA_NOPE_DIM + MLA_ROPE_DIM
    wq = w_q_b.reshape(MLA_Q_RANK, MLA_HEADS, qd)
    tail = MLA_HEAD_PAD - qd
    zq = lambda c: jnp.zeros((MLA_Q_RANK, MLA_HEADS, c), w_q_b.dtype)
    wqa = jnp.concatenate([wq, zq(tail)], axis=-1).reshape(MLA_Q_RANK, -1)
    wqb = jnp.concatenate([zq(MLA_NOPE_DIM), _rope_rotate_cols(wq[..., MLA_NOPE_DIM:]), zq(tail)],
                          axis=-1).reshape(MLA_Q_RANK, -1)
    wkv = w_kv_b.reshape(MLA_KV_RANK, MLA_HEADS, MLA_NOPE_DIM + MLA_V_DIM)
    zk = jnp.zeros((MLA_KV_RANK, MLA_HEADS, MLA_HEAD_PAD - MLA_NOPE_DIM), w_kv_b.dtype)
    wk = jnp.concatenate([wkv[..., :MLA_NOPE_DIM], zk], axis=-1).reshape(MLA_KV_RANK, -1)
    zv = jnp.zeros((MLA_KV_RANK, MLA_HEADS, MLA_HEAD_PAD - MLA_V_DIM), w_kv_b.dtype)
    wv = jnp.concatenate([wkv[..., MLA_NOPE_DIM:], zv], axis=-1).reshape(MLA_KV_RANK, -1)
    return wqa.astype(BF16), wqb.astype(BF16), wk.astype(BF16), wv.T.astype(BF16)


def _rope_tables(seq):
    inv_freq = jnp.power(ROPE_THETA, -jnp.arange(0, MLA_ROPE_DIM, 2, dtype=F32) / MLA_ROPE_DIM)
    ang = jnp.arange(seq, dtype=F32)[:, None] * inv_freq[None, :]
    cos2 = jnp.concatenate([jnp.cos(ang), jnp.cos(ang)], axis=-1)
    sin2 = jnp.concatenate([jnp.sin(ang), jnp.sin(ang)], axis=-1)
    tail = LANES - MLA_NOPE_DIM - MLA_ROPE_DIM
    one = jnp.ones((seq, MLA_NOPE_DIM), F32)
    z = lambda c: jnp.zeros((seq, c), F32)
    cq = jnp.concatenate([one, cos2, z(tail)], axis=-1)
    sq = jnp.concatenate([z(MLA_NOPE_DIM), sin2, z(tail)], axis=-1)
    ck = jnp.concatenate([z(MLA_NOPE_DIM), cos2, z(tail)], axis=-1)
    return cq, sq, ck, sq


def _na_bias_table(rpb, rows):
    kr = min(NA_WIN_ROWS, rows)
    heads, n_dr, n_dc = rpb.shape
    edge = GRID_W - 1 - (NA_WIN_COLS - 1)
    ext = jnp.concatenate([jnp.broadcast_to(rpb[..., :1], (heads, n_dr, edge)), rpb,
                           jnp.broadcast_to(rpb[..., -1:], (heads, n_dr, edge))], axis=-1)
    span = 2 * GRID_W - 1
    period = jnp.concatenate([ext, jnp.zeros((heads, n_dr, 1), ext.dtype)], axis=-1)
    tiled = jnp.tile(period, (1, 1, GRID_W))[..., :GRID_W * span]
    toep = tiled.reshape(heads, n_dr, GRID_W, span)[..., GRID_W - 1:]
    slabs = []
    for delta in range(kr):
        lo = NA_WIN_ROWS - 1 - delta
        s = toep[:, lo:lo + kr]
        slabs.append(jnp.transpose(s, (0, 2, 1, 3)).reshape(heads * GRID_W, kr * GRID_W))
    return jnp.stack(slabs, axis=0).astype(F32)


def kernel(x, ln_in_g, ln_in_b, w_in, rw_mu, rw_w0, rw_w2, rw_a0, rw_a2, rw_g2, rw_k_k, rw_k_a,
           rw_r_k, rw_gn_g, rw_gn_b, na_rpb, mla_q_norm_g, mla_w_q_b, mla_kv_norm_g, mla_w_kv_b,
           w_out, ln1_g, ln1_b, w_ffn_gate, w_ffn_up, w_ffn_down, ln2_g, ln2_b):
    batch, seq, d = x.shape
    depth = w_in.shape[0]
    assert d == D_MODEL and seq % ROW_TILE == 0 and seq % CHUNK == 0 and seq % GRID_W == 0
    assert seq // GRID_W >= NA_WIN_ROWS and D_FF % FFN_TILE == 0 and seq % (Q_TILE * MLA_Q_PER_STEP) == 0
    assert seq % MM_ROW_TILE == 0
    assert (seq // GRID_W) % NA_ROWS_PER_STEP == 0 and MLA_V_DIM * 2 == MLA_HEAD_PAD
    assert batch % RW_BATCH_PER_STEP == 0 and CHUNK == HEAD_DIM and RW_HEADS % 2 == 0
    n = batch * seq
    alpha = float((2 * depth) ** 0.25)
    row = lambda a: a.reshape(1, -1)
    tables = _rope_tables(seq)

    h = x.reshape(n, d)
    for l in range(depth):
        mu = jnp.pad(rw_mu[l], ((0, 0), (0, RW_PAD - RW_IN)))
        w_l = _prep_in_proj(w_in[l])
        mla_prm = (row(mla_q_norm_g[l]), row(mla_kv_norm_g[l]),
                   *_prep_mla_weights(mla_w_q_b[l], mla_w_kv_b[l]))
        if l == 0:
            h, xs, p_na, q, k, v = _inproj_call(h, w_l, mu, mla_prm, tables, seq,
                                                ln=(row(ln_in_g), row(ln_in_b)))
        else:
            xs, p_na, q, k, v = _inproj_call(h, w_l, mu, mla_prm, tables, seq)

        a2 = _prep_lora(rw_a2[l], 2 * RW_DECAY_RANK)
        scan_prm = (rw_w0[l], _prep_lora(rw_w2[l], 0), rw_a0[l], a2,
                    row(rw_k_k[l]), row(rw_k_a[l]))
        y_f, y_b = _rw_call(xs.reshape(batch, seq, RW_PAD), scan_prm, batch, seq)
        y_f = y_f.reshape(n, RW_WIDTH)
        y_b = y_b.reshape(n, RW_WIDTH)
        g2 = jnp.pad(rw_g2[l], ((0, LANES - RW_GATE_RANK), (0, 0))).astype(BF16)
        out_prm = (rw_a0[l], a2, row(rw_k_a[l]), row(rw_r_k[l]),
                   row(rw_gn_g[l]), row(rw_gn_b[l]), g2)

        o_na = _na_call(p_na, _na_bias_table(na_rpb[l], seq // GRID_W), batch, seq)

        o_mla = _mlaattn_call(q, k, v, batch, seq)

        h = _outproj_call(xs, y_f, y_b, o_na, o_mla, h, out_prm, w_out[l].astype(BF16),
                          row(ln1_g[l]), row(ln1_b[l]), alpha)
        h = _ffn_call(h, w_ffn_gate[l].astype(BF16), w_ffn_up[l].astype(BF16),
                      w_ffn_down[l].astype(BF16), row(ln2_g[l]), row(ln2_b[l]), alpha)
    return h.reshape(batch, seq, d)
```
